```python
import jax, jax.numpy as jnp
from jax import lax
import numpy as np

D_MODEL = 2048
BATCH = 8
SEQ = 4096
DEPTH = 1

MIX_WIDTH = D_MODEL
A_WIDTH = MIX_WIDTH // 2
B_WIDTH = MIX_WIDTH - A_WIDTH
CHUNK = 128
A_HEAD_DIM = 128
A_HEADS = A_WIDTH // A_HEAD_DIM
POOL_WINDOWS = (2, 4, 8, 16)
B_GROUPS = len(POOL_WINDOWS)
B_GROUP_DIM = B_WIDTH // B_GROUPS
IN_WIDTH = 2 * A_WIDTH + B_WIDTH
D_FF = 4 * D_MODEL
EPS = 1e-6

kernel_name = "hybrid_sgu_pool_block"


def rmsnorm(x, g):
    xf = x.astype(jnp.float32)
    y = xf * lax.rsqrt(jnp.mean(xf * xf, axis=-1, keepdims=True) + EPS)
    return (y * g.astype(jnp.float32)).astype(x.dtype)


def spatial_gating(u, v, w_s, b_s, g_v):
    bsz, s, _ = u.shape
    n_chunks = s // CHUNK
    vh = rmsnorm(v.reshape(bsz, s, A_HEADS, A_HEAD_DIM), g_v.reshape(A_HEADS, A_HEAD_DIM))
    vh = vh.reshape(bsz, n_chunks, CHUNK, A_HEADS, A_HEAD_DIM)
    causal = jnp.tril(jnp.ones((CHUNK, CHUNK), dtype=bool))
    w = jnp.where(causal[None], w_s, jnp.zeros_like(w_s))
    mixed = jnp.einsum('hts,bcshd->bcthd', w, vh)
    mixed = mixed + jnp.transpose(b_s)[None, None, :, :, None]
    return u * mixed.reshape(bsz, s, A_WIDTH)


def multiscale_pool(z, w_pool, pool_scale):
    bsz, s, _ = z.shape
    zf = z.astype(jnp.float32).reshape(bsz, s, B_GROUPS, B_GROUP_DIM)
    csum = jnp.cumsum(zf, axis=1)
    cpad = jnp.concatenate([jnp.zeros_like(csum[:, :1]), csum], axis=1)
    pos = jnp.arange(s, dtype=jnp.int32)
    outs = []
    for g, win in enumerate(POOL_WINDOWS):
        c = cpad[:, :, g]
        lag = jnp.pad(c, ((0, 0), (win - 1, 0), (0, 0)))[:, :s]
        count = jnp.minimum(pos + 1, win).astype(jnp.float32)[None, :, None]
        outs.append((c[:, 1:] - lag) / count - zf[:, :, g])
    pooled = jnp.stack(outs, axis=2).astype(z.dtype)
    y = jnp.einsum('bsgc,gcd->bsgd', pooled, w_pool).reshape(bsz, s, B_WIDTH)
    return y * pool_scale


def _fwd_setup_inputs(seed: int = 0) -> dict:
    key = jax.random.key(seed)
    ks = jax.random.split(key, 16)
    f32 = jnp.float32
    x = jax.random.normal(ks[0], (BATCH, SEQ, D_MODEL), f32)
    g_mix = 1.0 + 0.05 * jax.random.normal(ks[1], (DEPTH, D_MODEL), f32)
    w_in = jax.random.normal(ks[2], (DEPTH, D_MODEL, IN_WIDTH), f32) * D_MODEL ** -0.5
    g_v = 1.0 + 0.05 * jax.random.normal(ks[3], (DEPTH, A_WIDTH), f32)
    w_s = jax.random.normal(ks[4], (DEPTH, A_HEADS, CHUNK, CHUNK), f32) * (0.5 * CHUNK ** -0.5)
    b_s = 1.0 + 0.1 * jax.random.normal(ks[5], (DEPTH, A_HEADS, CHUNK), f32)
    w_pool = jax.random.normal(ks[6], (DEPTH, B_GROUPS, B_GROUP_DIM, B_GROUP_DIM), f32) * B_GROUP_DIM ** -0.5
    pool_scale = 0.5 + 0.1 * jax.random.normal(ks[7], (DEPTH, B_WIDTH), f32)
    w_out = jax.random.normal(ks[8], (DEPTH, MIX_WIDTH, D_MODEL), f32) * MIX_WIDTH ** -0.5
    g_ffn = 1.0 + 0.05 * jax.random.normal(ks[9], (DEPTH, D_MODEL), f32)
    w_up = jax.random.normal(ks[10], (DEPTH, D_MODEL, D_FF), f32) * D_MODEL ** -0.5
    w_down = jax.random.normal(ks[11], (DEPTH, D_FF, D_MODEL), f32) * D_FF ** -0.5
    g_final = 1.0 + 0.05 * jax.random.normal(ks[12], (D_MODEL,), f32)
    return {"x": x, "g_mix": g_mix, "w_in": w_in, "g_v": g_v, "w_s": w_s,
            "b_s": b_s, "w_pool": w_pool, "pool_scale": pool_scale,
            "w_out": w_out, "g_ffn": g_ffn, "w_up": w_up, "w_down": w_down,
            "g_final": g_final}


def _fwd_reference(x, g_mix, w_in, g_v, w_s, b_s, w_pool, pool_scale, w_out,
              g_ffn, w_up, w_down, g_final):
    for layer in range(DEPTH):
        h = rmsnorm(x, g_mix[layer])
        proj = jnp.einsum('bsd,de->bse', h, w_in[layer])
        u = jax.nn.gelu(proj[..., :A_WIDTH])
        v = jax.nn.gelu(proj[..., A_WIDTH:2 * A_WIDTH])
        z = proj[..., 2 * A_WIDTH:]
        out_a = spatial_gating(u, v, w_s[layer], b_s[layer], g_v[layer])
        out_b = multiscale_pool(z, w_pool[layer], pool_scale[layer])
        mixed = jnp.concatenate([out_a, out_b], axis=-1)
        x = x + jnp.einsum('bse,ed->bsd', mixed, w_out[layer])
        h = rmsnorm(x, g_ffn[layer])
        act = jnp.square(jax.nn.relu(jnp.einsum('bsd,df->bsf', h, w_up[layer])))
        x = x + jnp.einsum('bsf,fd->bsd', act, w_down[layer])
    return rmsnorm(x, g_final)


import jax as _jax
import jax.numpy as _jnp

TWIN_FORMAT = 'train_step'
FWD_PARAMS = ['x', 'g_mix', 'w_in', 'g_v', 'w_s', 'b_s', 'w_pool', 'pool_scale', 'w_out', 'g_ffn', 'w_up', 'w_down', 'g_final']
TWIN_WEIGHTS = ['g_mix', 'w_in', 'g_v', 'w_s', 'b_s', 'w_pool', 'pool_scale', 'w_out', 'g_ffn', 'w_up', 'w_down', 'g_final']
TWIN_DIFF_INPUT = 'x'
TWIN_INPUTS = ['x', 'g_mix', 'w_in', 'g_v', 'w_s', 'b_s', 'w_pool', 'pool_scale', 'w_out', 'g_ffn', 'w_up', 'w_down', 'g_final', 'loss_target', 'm_g_mix', 'm_w_in', 'm_g_v', 'm_w_s', 'm_b_s', 'm_w_pool', 'm_pool_scale', 'm_w_out', 'm_g_ffn', 'm_w_up', 'm_w_down', 'm_g_final', 'v_g_mix', 'v_w_in', 'v_g_v', 'v_w_s', 'v_b_s', 'v_w_pool', 'v_pool_scale', 'v_w_out', 'v_g_ffn', 'v_w_up', 'v_w_down', 'v_g_final']
TWIN_OUTPUTS = ['loss', 'grad_x', 'grad_g_mix', 'grad_w_in', 'grad_g_v', 'grad_w_s', 'grad_b_s', 'grad_w_pool', 'grad_pool_scale', 'grad_w_out', 'grad_g_ffn', 'grad_w_up', 'grad_w_down', 'grad_g_final', 'delta_g_mix', 'delta_w_in', 'delta_g_v', 'delta_w_s', 'delta_b_s', 'delta_w_pool', 'delta_pool_scale', 'delta_w_out', 'delta_g_ffn', 'delta_w_up', 'delta_w_down', 'delta_g_final', 'new_m_g_mix', 'new_m_w_in', 'new_m_g_v', 'new_m_w_s', 'new_m_b_s', 'new_m_w_pool', 'new_m_pool_scale', 'new_m_w_out', 'new_m_g_ffn', 'new_m_w_up', 'new_m_w_down', 'new_m_g_final', 'new_v_g_mix', 'new_v_w_in', 'new_v_g_v', 'new_v_w_s', 'new_v_b_s', 'new_v_w_pool', 'new_v_pool_scale', 'new_v_w_out', 'new_v_g_ffn', 'new_v_w_up', 'new_v_w_down', 'new_v_g_final']
TWIN_LEAF_KINDS = {'loss': 'loss', 'grad_x': 'grad_x', 'grad_g_mix': 'grad_w', 'grad_w_in': 'grad_w', 'grad_g_v': 'grad_w', 'grad_w_s': 'grad_w', 'grad_b_s': 'grad_w', 'grad_w_pool': 'grad_w', 'grad_pool_scale': 'grad_w', 'grad_w_out': 'grad_w', 'grad_g_ffn': 'grad_w', 'grad_w_up': 'grad_w', 'grad_w_down': 'grad_w', 'grad_g_final': 'grad_w', 'delta_g_mix': 'delta_w', 'delta_w_in': 'delta_w', 'delta_g_v': 'delta_w', 'delta_w_s': 'delta_w', 'delta_b_s': 'delta_w', 'delta_w_pool': 'delta_w', 'delta_pool_scale': 'delta_w', 'delta_w_out': 'delta_w', 'delta_g_ffn': 'delta_w', 'delta_w_up': 'delta_w', 'delta_w_down': 'delta_w', 'delta_g_final': 'delta_w', 'new_m_g_mix': 'new_m', 'new_m_w_in': 'new_m', 'new_m_g_v': 'new_m', 'new_m_w_s': 'new_m', 'new_m_b_s': 'new_m', 'new_m_w_pool': 'new_m', 'new_m_pool_scale': 'new_m', 'new_m_w_out': 'new_m', 'new_m_g_ffn': 'new_m', 'new_m_w_up': 'new_m', 'new_m_w_down': 'new_m', 'new_m_g_final': 'new_m', 'new_v_g_mix': 'new_v', 'new_v_w_in': 'new_v', 'new_v_g_v': 'new_v', 'new_v_w_s': 'new_v', 'new_v_b_s': 'new_v', 'new_v_w_pool': 'new_v', 'new_v_pool_scale': 'new_v', 'new_v_w_out': 'new_v', 'new_v_g_ffn': 'new_v', 'new_v_w_up': 'new_v', 'new_v_w_down': 'new_v', 'new_v_g_final': 'new_v'}


def _forward(args):
    return _fwd_reference(*[args[k] for k in FWD_PARAMS])


def _output_shape():
    def fwd():
        inp = _fwd_setup_inputs(0)
        return _fwd_reference(*[inp[k] for k in FWD_PARAMS])
    out = _jax.eval_shape(fwd)
    return out.shape, out.dtype

N_MICROBATCH = 1
ADAM_LR = 0.001
ADAM_B1 = 0.9
ADAM_B2 = 0.999
ADAM_EPS = 1e-08
ADAM_WD = 0.01
ADAM_STEP = 10
PER_EXAMPLE_BATCH_AXIS = {'x': 0, 'loss_target': 0}
SHARED_INPUTS = []
_WEIGHT_DTYPES = {'g_mix': _jnp.float32, 'w_in': _jnp.float32, 'g_v': _jnp.float32, 'w_s': _jnp.float32, 'b_s': _jnp.float32, 'w_pool': _jnp.float32, 'pool_scale': _jnp.float32, 'w_out': _jnp.float32, 'g_ffn': _jnp.float32, 'w_up': _jnp.float32, 'w_down': _jnp.float32, 'g_final': _jnp.float32}
MOMENT_SCALE = {'g_mix': 5.244860e-02, 'w_in': 4.283241e-02, 'g_v': 2.059339e-02, 'w_s': 3.860794e-02, 'b_s': 5.555564e-02, 'w_pool': 3.747354e-02, 'pool_scale': 7.878723e-02, 'w_out': 7.448395e-02, 'g_ffn': 8.012767e-02, 'w_up': 4.078914e-02, 'w_down': 1.422639e-01, 'g_final': 1.620207e+01}


def _to_microbatches(a, axis):
    t = _jnp.moveaxis(a, axis, 0)
    t = t.reshape((N_MICROBATCH, t.shape[0] // N_MICROBATCH) + t.shape[1:])
    return _jnp.moveaxis(t, 1, axis + 1)


def setup_inputs(seed: int = 0) -> dict:
    inp = _fwd_setup_inputs(seed)
    key = _jax.random.fold_in(_jax.random.key(seed), 7919)
    shape, _ = _output_shape()
    out = dict(inp)
    out["loss_target"] = _jax.random.normal(_jax.random.fold_in(key, 0), shape, _jnp.float32)
    for i, name in enumerate(TWIN_WEIGHTS):
        w = inp[name].astype(_jnp.float32)
        if MOMENT_SCALE is None:
            s = _jnp.sqrt(_jnp.mean(_jnp.square(w)) + 1e-30)
        else:
            s = MOMENT_SCALE[name]
        km, kv = _jax.random.split(_jax.random.fold_in(key, i + 1))
        out[name] = w
        out["m_" + name] = s * _jax.random.normal(km, w.shape, _jnp.float32)
        out["v_" + name] = (s * s) * _jax.random.uniform(kv, w.shape, _jnp.float32, 0.5, 1.5)
    if N_MICROBATCH > 1:
        for name, axis in PER_EXAMPLE_BATCH_AXIS.items():
            out[name] = _to_microbatches(out[name], axis)
    return {'x': out['x'], 'g_mix': out['g_mix'], 'w_in': out['w_in'], 'g_v': out['g_v'], 'w_s': out['w_s'], 'b_s': out['b_s'], 'w_pool': out['w_pool'], 'pool_scale': out['pool_scale'], 'w_out': out['w_out'], 'g_ffn': out['g_ffn'], 'w_up': out['w_up'], 'w_down': out['w_down'], 'g_final': out['g_final'], 'loss_target': out['loss_target'], 'm_g_mix': out['m_g_mix'], 'm_w_in': out['m_w_in'], 'm_g_v': out['m_g_v'], 'm_w_s': out['m_w_s'], 'm_b_s': out['m_b_s'], 'm_w_pool': out['m_w_pool'], 'm_pool_scale': out['m_pool_scale'], 'm_w_out': out['m_w_out'], 'm_g_ffn': out['m_g_ffn'], 'm_w_up': out['m_w_up'], 'm_w_down': out['m_w_down'], 'm_g_final': out['m_g_final'], 'v_g_mix': out['v_g_mix'], 'v_w_in': out['v_w_in'], 'v_g_v': out['v_g_v'], 'v_w_s': out['v_w_s'], 'v_b_s': out['v_b_s'], 'v_w_pool': out['v_w_pool'], 'v_pool_scale': out['v_pool_scale'], 'v_w_out': out['v_w_out'], 'v_g_ffn': out['v_g_ffn'], 'v_w_up': out['v_w_up'], 'v_w_down': out['v_w_down'], 'v_g_final': out['v_g_final']}


def _loss(weights, diff, rest, loss_target):
    with _jax.named_scope("forward"):
        args = {**rest, TWIN_DIFF_INPUT: diff, **{k: w.astype(_WEIGHT_DTYPES[k]) for k, w in weights.items()}}
        y = _forward(args)
    with _jax.named_scope("loss_head"):
        err = _jnp.square(y.astype(_jnp.float32) - loss_target)
        return 0.5 * _jnp.sum(_jnp.mean(err, axis=-1)) if err.ndim else 0.5 * err


def _adamw(w, g, m, v):
    m = ADAM_B1 * m + (1.0 - ADAM_B1) * g
    v = ADAM_B2 * v + (1.0 - ADAM_B2) * _jnp.square(g)
    m_hat = m / (1.0 - ADAM_B1 ** ADAM_STEP)
    v_hat = v / (1.0 - ADAM_B2 ** ADAM_STEP)
    delta = -ADAM_LR * (m_hat / (_jnp.sqrt(v_hat) + ADAM_EPS) + ADAM_WD * w)
    return delta, m, v


def reference(x, g_mix, w_in, g_v, w_s, b_s, w_pool, pool_scale, w_out, g_ffn, w_up, w_down, g_final, loss_target, m_g_mix, m_w_in, m_g_v, m_w_s, m_b_s, m_w_pool, m_pool_scale, m_w_out, m_g_ffn, m_w_up, m_w_down, m_g_final, v_g_mix, v_w_in, v_g_v, v_w_s, v_b_s, v_w_pool, v_pool_scale, v_w_out, v_g_ffn, v_w_up, v_w_down, v_g_final):
    given = dict(x=x, g_mix=g_mix, w_in=w_in, g_v=g_v, w_s=w_s, b_s=b_s, w_pool=w_pool, pool_scale=pool_scale, w_out=w_out, g_ffn=g_ffn, w_up=w_up, w_down=w_down, g_final=g_final, loss_target=loss_target, m_g_mix=m_g_mix, m_w_in=m_w_in, m_g_v=m_g_v, m_w_s=m_w_s, m_b_s=m_b_s, m_w_pool=m_w_pool, m_pool_scale=m_pool_scale, m_w_out=m_w_out, m_g_ffn=m_g_ffn, m_w_up=m_w_up, m_w_down=m_w_down, m_g_final=m_g_final, v_g_mix=v_g_mix, v_w_in=v_w_in, v_g_v=v_g_v, v_w_s=v_w_s, v_b_s=v_b_s, v_w_pool=v_w_pool, v_pool_scale=v_pool_scale, v_w_out=v_w_out, v_g_ffn=v_g_ffn, v_w_up=v_w_up, v_w_down=v_w_down, v_g_final=v_g_final)
    weights = {n: given[n] for n in TWIN_WEIGHTS}
    shared = {n: given[n] for n in SHARED_INPUTS}
    per_example = {n: given[n] for n in ['x']}
    grad_fn = _jax.value_and_grad(_loss, argnums=(0, 1))

    def one_microbatch(ex, loss_target):
        ex = dict(ex)
        diff = ex.pop(TWIN_DIFF_INPUT)
        return grad_fn(weights, diff, {**shared, **ex}, loss_target)

    if N_MICROBATCH == 1:
        loss, (grad_w, grad_x) = one_microbatch(per_example, given["loss_target"])
    else:
        def body(carry, xs):
            loss_sum, grad_sum = carry
            l_k, (gw_k, gx_k) = one_microbatch(xs[0], xs[1])
            with _jax.named_scope("update"):
                return (loss_sum + l_k, _jax.tree.map(_jnp.add, grad_sum, gw_k)), gx_k

        init = (_jnp.zeros((), _jnp.float32), _jax.tree.map(_jnp.zeros_like, weights))
        (loss, grad_w), grad_x = _jax.lax.scan(body, init, (per_example, given["loss_target"]))
    with _jax.named_scope("update"):
        delta_w, new_m, new_v = {}, {}, {}
        for n in TWIN_WEIGHTS:
            delta_w[n], new_m[n], new_v[n] = _adamw(weights[n], grad_w[n], given["m_" + n], given["v_" + n])
    return (loss, grad_x, *[grad_w[n] for n in TWIN_WEIGHTS], *[delta_w[n] for n in TWIN_WEIGHTS],
            *[new_m[n] for n in TWIN_WEIGHTS], *[new_v[n] for n in TWIN_WEIGHTS])
```

```python
import functools

import jax
import jax.numpy as jnp
from jax import lax
from jax.experimental import pallas as pl
from jax.experimental.pallas import tpu as pltpu

F32 = jnp.float32
BF16 = jnp.bfloat16
EPS = 1e-6
D = 2048
A_W = 1024
HEADS = 8
HD = 128
POOL_WINDOWS = (2, 4, 8, 16)
GD = 256
IN_W = 3072
D_FF = 8192
N_CHIPS = 4
HALO = 16
LANES = 128
MIB = 2 ** 20

ADAM_LR, ADAM_B1, ADAM_B2, ADAM_EPS, ADAM_WD, ADAM_STEP = 0.001, 0.9, 0.999, 1e-08, 0.01, 10

ANY = pl.BlockSpec(memory_space=pl.ANY)
MESH = pl.DeviceIdType.MESH

NN = ((1,), (0,))
NT = ((1,), (1,))
TN = ((0,), (0,))


def _params(n_axes, vmem_mb):
    return pltpu.CompilerParams(dimension_semantics=("arbitrary",) * n_axes,
                                vmem_limit_bytes=vmem_mb * MIB)


def _place():
    x, y, c = lax.axis_index("x"), lax.axis_index("y"), lax.axis_index("c")
    return x, y, c


def _other_chips(x, y):
    return [(1 - x, y), (x, 1 - y), (1 - x, 1 - y)]


def _matmul(name, grid, kaxis, ins, in_specs, out_shapes, out_specs, dims, epi, acc_shape=None, vmem_mb=48):
    n_in, n_out = len(ins), len(out_shapes)
    nk = grid[kaxis] if kaxis is not None else 1

    def body(*refs):
        in_refs, out_refs, scratch = refs[:n_in], refs[n_in:n_in + n_out], refs[n_in + n_out:]
        ids = tuple(pl.program_id(a) for a in range(len(grid)))
        p = lax.dot_general(in_refs[0][...], in_refs[1][...], (dims, ((), ())), preferred_element_type=F32)
        if kaxis is None:
            epi(p, ids, in_refs[2:], out_refs)
        else:
            acc = scratch[0]
            k = ids[kaxis]

            @pl.when(k == 0)
            def _():
                acc[...] = p

            @pl.when(k > 0)
            def _():
                acc[...] += p

            @pl.when(k == nk - 1)
            def _():
                epi(acc, ids, in_refs[2:], out_refs)

    return pl.pallas_call(
        body, name=name, grid=grid, in_specs=in_specs, out_specs=out_specs, out_shape=out_shapes,
        scratch_shapes=[pltpu.VMEM(acc_shape, F32)] if kaxis is not None else [],
        compiler_params=_params(len(grid), vmem_mb),
    )(*ins)


def _row_rsqrt(xf):
    return lax.rsqrt(jnp.mean(xf * xf, axis=-1, keepdims=True) + EPS)


def _norm_bwd(dh, xf, g, resid):
    r = _row_rsqrt(xf)
    xh = xf * r
    dg = jnp.sum(dh * xh, axis=0, keepdims=True)
    dxh = dh * g
    dx = resid + r * (dxh - xh * jnp.mean(dxh * xh, axis=-1, keepdims=True))
    return dx, dg


def _for_rows(n, fn, sub=128):
    def step(q, carry):
        fn(pl.ds(pl.multiple_of(q * sub, sub), sub))
        return carry

    lax.fori_loop(0, n // sub, step, 0)


def _zero_when(first, *refs):
    @pl.when(first)
    def _():
        for ref in refs:
            ref[...] = jnp.zeros_like(ref)


_GELU_K = 0.7978845608028654
_GELU_C = 0.044715


def _gelu(x):
    t = jnp.tanh(_GELU_K * (x + _GELU_C * x * x * x))
    return 0.5 * x * (1.0 + t)


def _gelu_and_grad(x):
    x2 = x * x
    t = jnp.tanh(_GELU_K * (x + _GELU_C * x * x2))
    g = 0.5 * x * (1.0 + t)
    dg = 0.5 * (1.0 + t) + 0.5 * x * (1.0 - t * t) * (_GELU_K * (1.0 + 3.0 * _GELU_C * x2))
    return g, dg


def _window_sum(ext, w, causal):
    n = ext.shape[0]
    s, d = ext, 1
    while d < w:
        s = s + pltpu.roll(s, d if causal else n - d, 0)
        d *= 2
    return s


def _inv_count(t, w):
    return 1.0 / jnp.minimum(t + 1, w).astype(F32)


def _pooled(z_ref, zh_ref, g, w, i, tm):
    cols = slice(GD * g, GD * (g + 1))
    zb = z_ref[:, cols]
    zh = jnp.where(i > 0, zh_ref[:, cols], 0.0)
    ext = jnp.concatenate([zh, zb], axis=0)
    s = _window_sum(ext, w, True)[HALO:, :]
    t = i * tm + lax.broadcasted_iota(jnp.int32, (tm, 1), 0)
    return s * _inv_count(t, w) - zb


def _norm1(x, g, tm):
    s = x.shape[0]

    def body(x_ref, g_ref, h_ref):
        xf = x_ref[...]
        h_ref[...] = (xf * _row_rsqrt(xf) * g_ref[...]).astype(BF16)

    return pl.pallas_call(
        body, name="norm1", grid=(s // tm,),
        in_specs=[pl.BlockSpec((tm, D), lambda i: (i, 0)), pl.BlockSpec((1, D), lambda i: (0, 0))],
        out_specs=pl.BlockSpec((tm, D), lambda i: (i, 0)),
        out_shape=jax.ShapeDtypeStruct((s, D), BF16),
        compiler_params=_params(1, 32),
    )(x, g)


def _mixer_fwd(proj, wt, bst, gv, wpool, scale, tm):
    s = proj.shape[0]
    nq = tm // HD

    def body(up_ref, vp_ref, z_ref, zh_ref, wt_ref, bst_ref, gv_ref, wp_ref, sc_ref, out_ref):
        i = pl.program_id(0)
        for h in range(HEADS):
            cols = slice(HD * h, HD * (h + 1))
            gvh = gv_ref[:, cols]
            bcol = bst_ref[h]
            wth = wt_ref[h]

            def chunk(q, carry):
                rows = pl.ds(pl.multiple_of(q * HD, HD), HD)
                u = _gelu(up_ref[rows, cols])
                v = _gelu(vp_ref[rows, cols])
                vh = (v * _row_rsqrt(v) * gvh).astype(BF16)
                mixed = jnp.dot(wth, vh, preferred_element_type=F32) + bcol
                out_ref[rows, cols] = (u * mixed).astype(BF16)
                return carry

            lax.fori_loop(0, nq, chunk, 0)
        for g, w in enumerate(POOL_WINDOWS):
            cols = slice(GD * g, GD * (g + 1))
            pooled = _pooled(z_ref, zh_ref, g, w, i, tm)
            yv = jnp.dot(pooled.astype(BF16), wp_ref[g], preferred_element_type=F32)
            out_ref[:, A_W + GD * g:A_W + GD * (g + 1)] = (yv * sc_ref[:, cols]).astype(BF16)

    hb = tm // HALO
    full = lambda shape: pl.BlockSpec(shape, lambda i: (0,) * len(shape))
    return pl.pallas_call(
        body, name="mixer_fwd", grid=(s // tm,),
        in_specs=[pl.BlockSpec((tm, A_W), lambda i: (i, 0)),
                  pl.BlockSpec((tm, A_W), lambda i: (i, 1)),
                  pl.BlockSpec((tm, A_W), lambda i: (i, 2)),
                  pl.BlockSpec((HALO, A_W), lambda i: (jnp.maximum(i * hb - 1, 0), 2)),
                  full((HEADS, HD, HD)), full((HEADS, HD, HD)), full((1, A_W)),
                  full((4, GD, GD)), full((1, A_W))],
        out_specs=pl.BlockSpec((tm, D), lambda i: (i, 0)),
        out_shape=jax.ShapeDtypeStruct((s, D), BF16),
        compiler_params=_params(1, 40),
    )(proj, proj, proj, proj, wt, bst, gv, wpool, scale)


def _mixer_bwd(proj, dmix, wt, wtt, bst, gv, wpool, scale, tm):
    s = proj.shape[0]
    nb = s // tm
    nq = tm // HD
    hb = tm // HALO

    def body(up_ref, vp_ref, z_ref, zh_ref, doa_ref, dob_ref, dobh_ref, wt_ref, wtt_ref, bst_ref, gv_ref, wp_ref,
             sc_ref, dproj_ref, dws_ref, dbs_ref, dgv_ref, dwp_ref, dsc_ref, dbfull):
        i = pl.program_id(0)
        first = i == 0
        last = i == nb - 1

        @pl.when(first)
        def _():
            dws_ref[...] = jnp.zeros_like(dws_ref)
            dbfull[...] = jnp.zeros_like(dbfull)
            dgv_ref[...] = jnp.zeros_like(dgv_ref)
            dwp_ref[...] = jnp.zeros_like(dwp_ref)
            dsc_ref[...] = jnp.zeros_like(dsc_ref)

        for h in range(HEADS):
            cols = slice(HD * h, HD * (h + 1))
            gvh = gv_ref[:, cols]
            bcol = bst_ref[h]
            wth = wt_ref[h]
            wtth = wtt_ref[h]

            def chunk(q, carry):
                rows = pl.ds(pl.multiple_of(q * HD, HD), HD)
                u, du_dup = _gelu_and_grad(up_ref[rows, cols])
                v, dv_dvp = _gelu_and_grad(vp_ref[rows, cols])
                rv = _row_rsqrt(v)
                vn = v * rv
                vh = (vn * gvh).astype(BF16)
                mixed = jnp.dot(wth, vh, preferred_element_type=F32) + bcol
                doa = doa_ref[rows, cols].astype(F32)
                dmx = doa * u
                dmxb = dmx.astype(BF16)
                dbfull[h] += dmx
                dws_ref[h] += lax.dot_general(dmxb, vh, (NT, ((), ())), preferred_element_type=F32)
                dvh = jnp.dot(wtth, dmxb, preferred_element_type=F32)
                dgv_ref[:, cols] += jnp.sum(dvh * vn, axis=0, keepdims=True)
                dvn = dvh * gvh
                dv = rv * (dvn - vn * jnp.mean(dvn * vn, axis=-1, keepdims=True))
                dproj_ref[rows, cols] = (doa * mixed * du_dup).astype(BF16)
                dproj_ref[rows, A_W + HD * h:A_W + HD * (h + 1)] = (dv * dv_dvp).astype(BF16)
                return carry

            lax.fori_loop(0, nq, chunk, 0)

        t = i * tm + lax.broadcasted_iota(jnp.int32, (tm, 1), 0)
        th = (i + 1) * tm + lax.broadcasted_iota(jnp.int32, (HALO, 1), 0)
        for g, w in enumerate(POOL_WINDOWS):
            cols = slice(GD * g, GD * (g + 1))
            wpg = wp_ref[g]
            scg = sc_ref[:, cols]
            pb = _pooled(z_ref, zh_ref, g, w, i, tm).astype(BF16)
            ypre = jnp.dot(pb, wpg, preferred_element_type=F32)
            dob = dob_ref[:, cols].astype(F32)
            dsc_ref[:, cols] += jnp.sum(dob * ypre, axis=0, keepdims=True)
            dyb = (dob * scg).astype(BF16)
            dwp_ref[g] += lax.dot_general(pb, dyb, (TN, ((), ())), preferred_element_type=F32)
            dpo = lax.dot_general(dyb, wpg, (NT, ((), ())), preferred_element_type=F32)
            dyh = (dobh_ref[:, cols].astype(F32) * scg).astype(BF16)
            dpoh = lax.dot_general(dyh, wpg, (NT, ((), ())), preferred_element_type=F32)
            dpoh = jnp.where(last, 0.0, dpoh * _inv_count(th, w))
            ext = jnp.concatenate([dpo * _inv_count(t, w), dpoh], axis=0)
            dz = _window_sum(ext, w, False)[:tm, :] - dpo
            dproj_ref[:, 2 * A_W + GD * g:2 * A_W + GD * (g + 1)] = dz.astype(BF16)

        @pl.when(last)
        def _():
            r = lax.broadcasted_iota(jnp.int32, (HD, HD), 0)
            c = lax.broadcasted_iota(jnp.int32, (HD, HD), 1)
            for h in range(HEADS):
                dws_ref[h] = jnp.where(r >= c, dws_ref[h], 0.0)
                dbs_ref[h] = jnp.sum(dbfull[h], axis=-1, keepdims=True)

    full = lambda shape: pl.BlockSpec(shape, lambda i: (0,) * len(shape))
    return pl.pallas_call(
        body, name="mixer_bwd", grid=(nb,),
        in_specs=[pl.BlockSpec((tm, A_W), lambda i: (i, 0)),
                  pl.BlockSpec((tm, A_W), lambda i: (i, 1)),
                  pl.BlockSpec((tm, A_W), lambda i: (i, 2)),
                  pl.BlockSpec((HALO, A_W), lambda i: (jnp.maximum(i * hb - 1, 0), 2)),
                  pl.BlockSpec((tm, A_W), lambda i: (i, 0)),
                  pl.BlockSpec((tm, A_W), lambda i: (i, 1)),
                  pl.BlockSpec((HALO, A_W), lambda i: (jnp.minimum((i + 1) * hb, s // HALO - 1), 1)),
                  full((HEADS, HD, HD)), full((HEADS, HD, HD)), full((HEADS, HD, HD)), full((1, A_W)),
                  full((4, GD, GD)), full((1, A_W))],
        out_specs=[pl.BlockSpec((tm, IN_W), lambda i: (i, 0)),
                   full((HEADS, HD, HD)), full((HEADS, HD, 1)), full((1, A_W)), full((4, GD, GD)), full((1, A_W))],
        out_shape=[jax.ShapeDtypeStruct((s, IN_W), BF16),
                   jax.ShapeDtypeStruct((HEADS, HD, HD), F32),
                   jax.ShapeDtypeStruct((HEADS, HD, 1), F32),
                   jax.ShapeDtypeStruct((1, A_W), F32),
                   jax.ShapeDtypeStruct((4, GD, GD), F32),
                   jax.ShapeDtypeStruct((1, A_W), F32)],
        scratch_shapes=[pltpu.VMEM((HEADS, HD, HD), F32)],
        compiler_params=_params(1, 48),
    )(proj, proj, proj, proj, dmix, dmix, dmix, wt, wtt, bst, gv, wpool, scale)


def _inproj(h1, win_g, tm):
    s = h1.shape[0]
    cw = IN_W // N_CHIPS

    def epi(p, ids, extra, outs):
        outs[0][...] = p

    return _matmul(
        "inproj", (N_CHIPS, s // tm), None, [h1, win_g],
        [pl.BlockSpec((tm, D), lambda j, i: (i, 0)), pl.BlockSpec((None, D, cw), lambda j, i: (j, 0, 0))],
        [jax.ShapeDtypeStruct((s, IN_W), F32)], [pl.BlockSpec((tm, cw), lambda j, i: (i, j))], NN, epi, vmem_mb=32)[0]


def _outproj(mixed, wout, x, g_ffn, tm):
    s = x.shape[0]

    def epi(p, ids, extra, outs):
        x_ref, g_ref = extra
        x2 = x_ref[...] + p
        outs[0][...] = x2
        outs[1][...] = (x2 * _row_rsqrt(x2) * g_ref[...]).astype(BF16)

    row = pl.BlockSpec((tm, D), lambda i: (i, 0))
    return _matmul(
        "outproj", (s // tm,), None, [mixed, wout, x, g_ffn],
        [row, pl.BlockSpec((D, D), lambda i: (0, 0)), row, pl.BlockSpec((1, D), lambda i: (0, 0))],
        [jax.ShapeDtypeStruct((s, D), F32), jax.ShapeDtypeStruct((s, D), BF16)], [row, row], NN, epi, vmem_mb=48)


def _up(h2, wup_g, tm, tn):
    s = h2.shape[0]
    per = D // tn

    def epi(p, ids, extra, outs):
        a = jnp.maximum(p, 0.0)
        outs[0][...] = (a * a).astype(BF16)

    return _matmul(
        "up", (D_FF // tn, s // tm), None, [h2, wup_g],
        [pl.BlockSpec((tm, D), lambda j, i: (i, 0)), pl.BlockSpec((None, D, tn), lambda j, i: (j // per, 0, j % per))],
        [jax.ShapeDtypeStruct((s, D_FF), BF16)], [pl.BlockSpec((tm, tn), lambda j, i: (i, j))], NN, epi, vmem_mb=40)[0]


def _down(act, wdown, x2, tgt, g_final, tm, tk):
    s = x2.shape[0]

    def epi(acc, ids, extra, outs):
        x2_ref, t_ref, g_ref = extra
        dx_ref, dxb_ref, dgf_ref, loss_ref = outs
        g = g_ref[...]
        _zero_when(ids[0] == 0, dgf_ref, loss_ref)

        def block(rows):
            x3 = x2_ref[rows, :] + acc[rows, :]
            r = _row_rsqrt(x3)
            xh = x3 * r
            diff = xh * g - t_ref[rows, :]
            dy = diff * (1.0 / D)
            dxh = dy * g
            dx = r * (dxh - xh * jnp.mean(dxh * xh, axis=-1, keepdims=True))
            dx_ref[rows, :] = dx
            dxb_ref[rows, :] = dx.astype(BF16)
            dgf_ref[...] += jnp.sum(dy * xh, axis=0, keepdims=True)
            loss_ref[...] += jnp.sum(diff * diff, axis=0, keepdims=True)

        _for_rows(tm, block)

    row = pl.BlockSpec((tm, D), lambda i, k: (i, 0))
    vec = pl.BlockSpec((1, D), lambda i, k: (0, 0))
    return _matmul(
        "down", (s // tm, D_FF // tk), 1, [act, wdown, x2, tgt, g_final],
        [pl.BlockSpec((tm, tk), lambda i, k: (i, k)), pl.BlockSpec((tk, D), lambda i, k: (k, 0)), row, row, vec],
        [jax.ShapeDtypeStruct((s, D), F32), jax.ShapeDtypeStruct((s, D), BF16),
         jax.ShapeDtypeStruct((1, D), F32), jax.ShapeDtypeStruct((1, D), F32)],
        [row, row, vec, vec], NN, epi, acc_shape=(tm, D), vmem_mb=58)


def _dact(dx3b, wdown, act, tm, tn):
    s = dx3b.shape[0]

    def epi(p, ids, extra, outs):
        outs[0][...] = (p * (2.0 * jnp.sqrt(extra[0][...].astype(F32)))).astype(BF16)

    tile = pl.BlockSpec((tm, tn), lambda j, i: (i, j))
    return _matmul(
        "dact", (D_FF // tn, s // tm), None, [dx3b, wdown, act],
        [pl.BlockSpec((tm, D), lambda j, i: (i, 0)), pl.BlockSpec((tn, D), lambda j, i: (j, 0)), tile],
        [jax.ShapeDtypeStruct((s, D_FF), BF16)], [tile], NT, epi, vmem_mb=40)[0]


def _dweight(name, lhs, rhs, n_shards, rows, cols, tm, tk):
    s = lhs.shape[0]

    def epi(acc, ids, extra, outs):
        outs[0][...] = acc[...].astype(BF16)

    return _matmul(
        name, (n_shards, rows // tm, s // tk), 2, [lhs, rhs],
        [pl.BlockSpec((tk, tm), lambda j, i, k: (k, i)), pl.BlockSpec((tk, cols), lambda j, i, k: (k, j))],
        [jax.ShapeDtypeStruct((n_shards, rows, cols), BF16)],
        [pl.BlockSpec((None, tm, cols), lambda j, i, k: (j, i, 0))], TN, epi, acc_shape=(tm, cols), vmem_mb=40)[0]


def _dh2(da, wup_g, x2, dx3, g_ffn, tm, tk):
    s = x2.shape[0]
    per = D // tk

    def epi(acc, ids, extra, outs):
        x2_ref, dx3_ref, g_ref = extra
        g = g_ref[...]
        _zero_when(ids[0] == 0, outs[2])

        def block(rows):
            dx, dg = _norm_bwd(acc[rows, :], x2_ref[rows, :], g, dx3_ref[rows, :])
            outs[0][rows, :] = dx
            outs[1][rows, :] = dx.astype(BF16)
            outs[2][...] += dg

        _for_rows(tm, block)

    row = pl.BlockSpec((tm, D), lambda i, k: (i, 0))
    vec = pl.BlockSpec((1, D), lambda i, k: (0, 0))
    return _matmul(
        "dh2", (s // tm, D_FF // tk), 1, [da, wup_g, x2, dx3, g_ffn],
        [pl.BlockSpec((tm, tk), lambda i, k: (i, k)),
         pl.BlockSpec((None, D, tk), lambda i, k: (k // per, 0, k % per)), row, row, vec],
        [jax.ShapeDtypeStruct((s, D), F32), jax.ShapeDtypeStruct((s, D), BF16), jax.ShapeDtypeStruct((1, D), F32)],
        [row, row, vec], NT, epi, acc_shape=(tm, D), vmem_mb=58)


def _dmixed(dx2b, wout, tm):
    s = dx2b.shape[0]

    def epi(p, ids, extra, outs):
        outs[0][...] = p.astype(BF16)

    row = pl.BlockSpec((tm, D), lambda i: (i, 0))
    return _matmul(
        "dmixed", (s // tm,), None, [dx2b, wout], [row, pl.BlockSpec((D, D), lambda i: (0, 0))],
        [jax.ShapeDtypeStruct((s, D), BF16)], [row], NT, epi, vmem_mb=40)[0]


def _dh1(dproj, win_g, x, dx2, g_mix, tm):
    s = x.shape[0]
    cw = IN_W // N_CHIPS

    def epi(acc, ids, extra, outs):
        x_ref, dx2_ref, g_ref = extra
        g = g_ref[...]
        _zero_when(ids[0] == 0, outs[1])

        def block(rows):
            dx, dg = _norm_bwd(acc[rows, :], x_ref[rows, :], g, dx2_ref[rows, :])
            outs[0][rows, :] = dx
            outs[1][...] += dg

        _for_rows(tm, block)

    row = pl.BlockSpec((tm, D), lambda i, j: (i, 0))
    vec = pl.BlockSpec((1, D), lambda i, j: (0, 0))
    return _matmul(
        "dh1", (s // tm, N_CHIPS), 1, [dproj, win_g, x, dx2, g_mix],
        [pl.BlockSpec((tm, cw), lambda i, j: (i, j)), pl.BlockSpec((None, D, cw), lambda i, j: (j, 0, 0)),
         row, row, vec],
        [jax.ShapeDtypeStruct((s, D), F32), jax.ShapeDtypeStruct((1, D), F32)],
        [row, vec], NT, epi, acc_shape=(tm, D), vmem_mb=52)


def _remote(src, dst, send_sem, recv_sem, dev):
    return pltpu.make_async_remote_copy(src_ref=src, dst_ref=dst, send_sem=send_sem, recv_sem=recv_sem,
                                        device_id=dev, device_id_type=MESH)


def _allgather_weights(shards):
    n = len(shards)

    def body(*refs):
        ins, outs = refs[:n], refs[n:2 * n]
        send_sems, recv_sems, local_sems = refs[2 * n:]
        x, y, c = _place()
        k = 2 * x + y
        chips = _other_chips(x, y)
        sib = (x, y, 1 - c)
        local = []
        for w in range(n):
            cp = pltpu.make_async_copy(ins[w], outs[w].at[k], local_sems.at[w])
            cp.start()
            local.append(cp)

        def half(w, who):
            rh = shards[w].shape[0] // 2
            return pl.ds(pl.multiple_of(who * rh, 16), rh)

        sends = []
        for w in range(n):
            for j, (cx, cy) in enumerate(chips):
                cp = _remote(ins[w].at[half(w, c)], outs[w].at[k, half(w, c)],
                             send_sems.at[3 * w + j], recv_sems.at[3 * w + j], (cx, cy, c))
                cp.start()
                sends.append(cp)
        for w in range(n):
            for j, (cx, cy) in enumerate(chips):
                piece = outs[w].at[2 * cx + cy, half(w, c)]
                _remote(piece, piece, send_sems.at[3 * w + j], recv_sems.at[3 * w + j], (cx, cy, c)).wait_recv()
                cp = _remote(piece, piece, send_sems.at[3 * n + 3 * w + j], recv_sems.at[3 * n + 3 * w + j], sib)
                cp.start()
                sends.append(cp)
        for w in range(n):
            for j, (cx, cy) in enumerate(chips):
                piece = outs[w].at[2 * cx + cy, half(w, 1 - c)]
                _remote(piece, piece, send_sems.at[3 * n + 3 * w + j], recv_sems.at[3 * n + 3 * w + j],
                        sib).wait_recv()
        for cp in sends:
            cp.wait_send()
        for cp in local:
            cp.wait()

    return pl.pallas_call(
        body, name="allgather_weights",
        in_specs=[ANY] * n, out_specs=[ANY] * n,
        out_shape=[jax.ShapeDtypeStruct((N_CHIPS,) + a.shape, a.dtype) for a in shards],
        scratch_shapes=[pltpu.SemaphoreType.DMA((6 * n,)), pltpu.SemaphoreType.DMA((6 * n,)),
                        pltpu.SemaphoreType.DMA((n,))],
    )(*shards)


def _small_allreduce(part):
    rows = part.shape[0]
    rh = rows // 2

    def body(p_ref, o_ref, sib_ref, slots, send_sems, recv_sems):
        x, y, c = _place()
        k = 2 * x + y
        sib = (x, y, 1 - c)
        half = pl.ds(pl.multiple_of(c * rh, 8), rh)
        cp = _remote(p_ref, sib_ref, send_sems.at[0], recv_sems.at[0], sib)
        cp.start()
        cp.wait()
        slots[k] = p_ref[half, :] + sib_ref[half, :]
        cps = []
        for j, (cx, cy) in enumerate(_other_chips(x, y)):
            cp = _remote(slots.at[k], slots.at[k], send_sems.at[1 + j], recv_sems.at[1 + j], (cx, cy, c))
            cp.start()
            cps.append(cp)
        for cp in cps:
            cp.wait()
        o_ref[half, :] = ((slots[0] + slots[1]) + slots[2]) + slots[3]
        cp = _remote(o_ref.at[half], o_ref.at[half], send_sems.at[4], recv_sems.at[4], sib)
        cp.start()
        cp.wait()

    vm = pl.BlockSpec(memory_space=pltpu.VMEM)
    return pl.pallas_call(
        body, name="small_allreduce", in_specs=[vm], out_specs=vm,
        out_shape=jax.ShapeDtypeStruct(part.shape, F32),
        scratch_shapes=[pltpu.VMEM(part.shape, F32), pltpu.VMEM((N_CHIPS, rh, LANES), F32),
                        pltpu.SemaphoreType.DMA((5,)), pltpu.SemaphoreType.DMA((5,))],
        compiler_params=pltpu.CompilerParams(vmem_limit_bytes=32 * MIB),
    )(part)


def _rs_sibling(dws):
    n = len(dws)

    def body(*refs):
        ins, outs = refs[:n], refs[n:2 * n]
        send_sems, recv_sems = refs[2 * n:]
        x, y, c = _place()
        cps = []
        for w in range(n):
            for j in range(N_CHIPS):
                cp = _remote(ins[w].at[j, 1 - c], outs[w].at[j], send_sems.at[4 * w + j], recv_sems.at[4 * w + j],
                             (x, y, 1 - c))
                cp.start()
                cps.append(cp)
        for cp in cps:
            cp.wait()

    return pl.pallas_call(
        body, name="rs_sibling", in_specs=[ANY] * n, out_specs=[ANY] * n,
        out_shape=[jax.ShapeDtypeStruct((N_CHIPS,) + a.shape[2:], a.dtype) for a in dws],
        scratch_shapes=[pltpu.SemaphoreType.DMA((4 * n,)), pltpu.SemaphoreType.DMA((4 * n,))],
    )(*dws)


def _rs_chips(sums):
    n = len(sums)

    def body(*refs):
        ins, outs = refs[:n], refs[n:2 * n]
        send_sems, recv_sems = refs[2 * n:]
        x, y, c = _place()
        cps = []
        for w in range(n):
            for j, (cx, cy) in enumerate(_other_chips(x, y)):
                cp = _remote(ins[w].at[2 * cx + cy], outs[w].at[j], send_sems.at[3 * w + j],
                             recv_sems.at[3 * w + j], (cx, cy, c))
                cp.start()
                cps.append(cp)
        for cp in cps:
            cp.wait()

    return pl.pallas_call(
        body, name="rs_chips", in_specs=[ANY] * n, out_specs=[ANY] * n,
        out_shape=[jax.ShapeDtypeStruct((3,) + a.shape[1:], a.dtype) for a in sums],
        scratch_shapes=[pltpu.SemaphoreType.DMA((3 * n,)), pltpu.SemaphoreType.DMA((3 * n,))],
    )(*sums)


def _share_halves(grads):
    n = len(grads)

    def body(*refs):
        outs = refs[n:2 * n]
        send_sems, recv_sems = refs[2 * n:]
        x, y, c = _place()
        cps = []
        for w in range(n):
            cp = _remote(outs[w].at[c], outs[w].at[c], send_sems.at[w], recv_sems.at[w], (x, y, 1 - c))
            cp.start()
            cps.append(cp)
        for cp in cps:
            cp.wait()

    return pl.pallas_call(
        body, name="share_halves", in_specs=[ANY] * n, out_specs=[ANY] * n,
        out_shape=[jax.ShapeDtypeStruct(a.shape, a.dtype) for a in grads],
        input_output_aliases={w: w for w in range(n)},
        scratch_shapes=[pltpu.SemaphoreType.DMA((n,)), pltpu.SemaphoreType.DMA((n,))],
    )(*grads)


def _add_sibling(place, dw, got, tr):
    _, _, rh, cols = dw.shape

    def body(pref, a_ref, b_ref, o_ref):
        o_ref[...] = (a_ref[...].astype(F32) + b_ref[...].astype(F32)).astype(BF16)

    return pl.pallas_call(
        body, name="add_sibling",
        grid_spec=pltpu.PrefetchScalarGridSpec(
            num_scalar_prefetch=1, grid=(N_CHIPS, rh // tr),
            in_specs=[pl.BlockSpec((None, None, tr, cols), lambda j, r, p: (j, p[1], r, 0)),
                      pl.BlockSpec((None, tr, cols), lambda j, r, p: (j, r, 0))],
            out_specs=pl.BlockSpec((None, tr, cols), lambda j, r, p: (j, r, 0))),
        out_shape=jax.ShapeDtypeStruct((N_CHIPS, rh, cols), BF16),
        compiler_params=_params(2, 32),
    )(place, dw, got)


def _add_chips(place, sums, got, tr):
    _, rh, cols = sums.shape

    def body(pref, a_ref, b_ref, o_ref):
        b = b_ref[...].astype(F32)
        o_ref[...] = ((a_ref[...].astype(F32) + b[0]) + b[1]) + b[2]

    return pl.pallas_call(
        body, name="add_chips",
        grid_spec=pltpu.PrefetchScalarGridSpec(
            num_scalar_prefetch=1, grid=(rh // tr,),
            in_specs=[pl.BlockSpec((None, tr, cols), lambda r, p: (p[0], r, 0)),
                      pl.BlockSpec((3, tr, cols), lambda r, p: (0, r, 0))],
            out_specs=pl.BlockSpec((None, tr, cols), lambda r, p: (p[1], r, 0))),
        out_shape=jax.ShapeDtypeStruct((2, rh, cols), F32),
        compiler_params=_params(1, 32),
    )(place, sums, got)


def _adamw_math(w, g, m, v):
    m = ADAM_B1 * m + (1.0 - ADAM_B1) * g
    v = ADAM_B2 * v + (1.0 - ADAM_B2) * (g * g)
    m_hat = m / (1.0 - ADAM_B1 ** ADAM_STEP)
    v_hat = v / (1.0 - ADAM_B2 ** ADAM_STEP)
    delta = -ADAM_LR * (m_hat / (jnp.sqrt(v_hat) + ADAM_EPS) + ADAM_WD * w)
    return delta, m, v


def _adamw(name, w, g, m, v, tr):
    rows, cols = w.shape

    def body(w_ref, g_ref, m_ref, v_ref, d_ref, nm_ref, nv_ref):
        d_ref[...], nm_ref[...], nv_ref[...] = _adamw_math(w_ref[...], g_ref[...], m_ref[...], v_ref[...])

    blk = pl.BlockSpec((tr, cols), lambda r: (r, 0))
    return pl.pallas_call(
        body, name=name, grid=(rows // tr,), in_specs=[blk] * 4, out_specs=[blk] * 3,
        out_shape=[jax.ShapeDtypeStruct(w.shape, F32)] * 3,
        compiler_params=_params(1, 32),
    )(w, g, m, v)


def _rows(a):
    return a.reshape(-1, LANES)


def kernel(x, g_mix, w_in, g_v, w_s, b_s, w_pool, pool_scale, w_out, g_ffn, w_up, w_down, g_final, loss_target, m_g_mix, m_w_in, m_g_v, m_w_s, m_b_s, m_w_pool, m_pool_scale, m_w_out, m_g_ffn, m_w_up, m_w_down, m_g_final, v_g_mix, v_w_in, v_g_v, v_w_s, v_b_s, v_w_pool, v_pool_scale, v_w_out, v_g_ffn, v_w_up, v_w_down, v_g_final):
    s = x.shape[1]
    tm = 512
    xs = x[0]
    tgt = loss_target[0]
    cx, cy, cc = _place()
    chip = 2 * cx + cy
    place = jnp.stack([chip, cc]).astype(jnp.int32)

    win_g, wout_g, wup_g, wdown_g, wpool_g = _allgather_weights([
        w_in[0].astype(BF16), w_out[0].astype(BF16), w_up[0].astype(BF16), w_down[0].astype(BF16),
        w_pool[0].reshape(4 * 64, GD).astype(BF16)])
    wout_f = wout_g.reshape(D, D)
    wdown_f = wdown_g.reshape(D_FF, D)
    wpool_f = wpool_g.reshape(N_CHIPS, 4, 64, GD).transpose(1, 0, 2, 3).reshape(4, GD, GD)
    tril = jnp.tril(jnp.ones((HD, HD), dtype=bool))
    wt = jnp.where(tril[None], w_s[0], 0.0).astype(BF16)
    wtt = wt.transpose(0, 2, 1)
    bst = jnp.broadcast_to(b_s[0][:, :, None], (HEADS, HD, HD))
    gfin = g_final.reshape(1, D)

    h1 = _norm1(xs, g_mix, tm)
    proj = _inproj(h1, win_g, tm)
    mixed = _mixer_fwd(proj, wt, bst, g_v, wpool_f, pool_scale, tm)
    x2, h2 = _outproj(mixed, wout_f, xs, g_ffn, 256)
    act = _up(h2, wup_g, tm, 1024)
    dx3, dx3b, dgf, lossv = _down(act, wdown_f, x2, tgt, gfin, tm, 1024)

    da = _dact(dx3b, wdown_f, act, tm, 1024)
    dwdown = _dweight("dw_down", act, dx3b, 1, D_FF, D, 512, 512)
    dx2, dx2b, dgffn = _dh2(da, wup_g, x2, dx3, g_ffn, tm, 1024)
    dwup = _dweight("dw_up", h2, da, N_CHIPS, D, D, 512, 512)
    dmix = _dmixed(dx2b, wout_f, 256)
    dwout = _dweight("dw_out", mixed, dx2b, 1, D, D, 512, 512)
    dproj, dws, dbs, dgv, dwp, dsc = _mixer_bwd(proj, dmix, wt, wtt, bst, g_v, wpool_f, pool_scale, tm)
    grad_x, dgmix = _dh1(dproj, win_g, xs, dx2, g_mix, tm)
    dwin = _dweight("dw_in", h1, dproj, N_CHIPS, D, IN_W // N_CHIPS, 512, 512)

    big = [dwin.reshape(N_CHIPS, 2, D // 2, IN_W // N_CHIPS), dwout.reshape(N_CHIPS, 2, D // 8, D),
           dwup.reshape(N_CHIPS, 2, D // 2, D), dwdown.reshape(N_CHIPS, 2, D // 2, D)]
    from_sib = _rs_sibling(big)
    chip_sums = [_add_sibling(place, a, b, 256) for a, b in zip(big, from_sib)]
    from_chips = _rs_chips(chip_sums)
    halves = [_add_chips(place, a, b, 256) for a, b in zip(chip_sums, from_chips)]
    g_in, g_out, g_up, g_down = _share_halves(halves)
    g_in = g_in.reshape(D, IN_W // N_CHIPS)
    g_out = g_out.reshape(D // N_CHIPS, D)
    g_up = g_up.reshape(D, D)
    g_down = g_down.reshape(D, D)

    pieces = [dgmix, dgv, dws, dbs, dwp, dsc, dgffn, dgf, lossv, jnp.zeros((8 * LANES,), F32)]
    sizes = [p.size // LANES for p in pieces]
    tot = _small_allreduce(jnp.concatenate([_rows(p) for p in pieces], axis=0))
    offs = [sum(sizes[:i]) for i in range(len(sizes))]
    take = lambda i: tot[offs[i]:offs[i] + sizes[i]]
    s_gmix, s_gv, s_ws, s_bs, s_wp, s_sc, s_gffn, s_gf = [take(i) for i in range(8)]
    loss = (0.5 / D) * jnp.sum(take(8))
    s_wp_mine = lax.dynamic_slice_in_dim(s_wp.reshape(4, GD, GD), chip * 64, 64, axis=1)

    d_in, nm_in, nv_in = _adamw("adamw_in", w_in[0], g_in, m_w_in[0], v_w_in[0], 128)
    d_out, nm_out, nv_out = _adamw("adamw_out", w_out[0], g_out, m_w_out[0], v_w_out[0], 128)
    d_up, nm_up, nv_up = _adamw("adamw_up", w_up[0], g_up, m_w_up[0], v_w_up[0], 128)
    d_down, nm_down, nv_down = _adamw("adamw_down", w_down[0], g_down, m_w_down[0], v_w_down[0], 128)
    small_g = [s_gmix, s_gv, s_ws, s_bs, _rows(s_wp_mine), s_sc, s_gffn, s_gf]
    small_w = [g_mix, g_v, w_s, b_s, w_pool, pool_scale, g_ffn, g_final]
    small_m = [m_g_mix, m_g_v, m_w_s, m_b_s, m_w_pool, m_pool_scale, m_g_ffn, m_g_final]
    small_v = [v_g_mix, v_g_v, v_w_s, v_b_s, v_w_pool, v_pool_scale, v_g_ffn, v_g_final]
    cat = lambda parts: jnp.concatenate([_rows(p) for p in parts], axis=0)
    sg = cat(small_g)
    sd, snm, snv = _adamw("adamw_small", cat(small_w), sg, cat(small_m), cat(small_v), sg.shape[0])
    ssz = [p.size // LANES for p in small_w]
    soff = [sum(ssz[:i]) for i in range(len(ssz))]
    split = lambda a: [a[soff[i]:soff[i] + ssz[i]].reshape(small_w[i].shape) for i in range(len(ssz))]
    gs, ds, nms, nvs = split(sg), split(sd), split(snm), split(snv)

    def ordered(small, w_in_, w_out_, w_up_, w_down_):
        return [small[0], w_in_[None], small[1], small[2], small[3], small[4], small[5], w_out_[None], small[6],
                w_up_[None], w_down_[None], small[7]]

    return (loss, grad_x[None],
            *ordered(gs, g_in, g_out, g_up, g_down),
            *ordered(ds, d_in, d_out, d_up, d_down),
            *ordered(nms, nm_in, nm_out, nm_up, nm_down),
            *ordered(nvs, nv_in, nv_out, nv_up, nv_down))
```

```python
import functools

import jax
import jax.numpy as jnp
from jax import lax
from jax.experimental import pallas as pl
from jax.experimental.pallas import tpu as pltpu

F32 = jnp.float32
BF16 = jnp.bfloat16
EPS = 1e-6
D = 2048
A_W = 1024
HEADS = 8
HD = 128
POOL_WINDOWS = (2, 4, 8, 16)
GD = 256
IN_W = 3072
D_FF = 8192
N_CHIPS = 4
HALO = 16
LANES = 128
MIB = 2 ** 20

ADAM_LR, ADAM_B1, ADAM_B2, ADAM_EPS, ADAM_WD, ADAM_STEP = 0.001, 0.9, 0.999, 1e-08, 0.01, 10

ANY = pl.BlockSpec(memory_space=pl.ANY)
MESH = pl.DeviceIdType.MESH

NN = ((1,), (0,))
NT = ((1,), (1,))
TN = ((0,), (0,))


def _place():
    return lax.axis_index("x"), lax.axis_index("y"), lax.axis_index("c")


def _other_chips(x, y):
    return [(1 - x, y), (x, 1 - y), (1 - x, 1 - y)]


def _remote(src, dst, send_sem, recv_sem, dev):
    return pltpu.make_async_remote_copy(src_ref=src, dst_ref=dst, send_sem=send_sem, recv_sem=recv_sem,
                                        device_id=dev, device_id_type=MESH)


class _SemView:
    def __init__(self, sems, base):
        self.sems, self.base = sems, base

    @property
    def at(self):
        return self

    def __getitem__(self, i):
        return self.sems.at[self.base + i]


class _Carry:
    def __init__(self, srcs, lands, n_sems, start, finish):
        self.srcs, self.lands, self.n_sems, self.start, self.finish = list(srcs), list(lands), n_sems, start, finish


def _join(*carries):
    def run(which):
        def go(srcs, lands, ssem, rsem):
            so = lo = qo = 0
            for c in carries:
                getattr(c, which)(srcs[so:so + len(c.srcs)], lands[lo:lo + len(c.lands)],
                                  _SemView(ssem, qo), _SemView(rsem, qo))
                so, lo, qo = so + len(c.srcs), lo + len(c.lands), qo + c.n_sems
        return go

    return _Carry([s for c in carries for s in c.srcs], [l for c in carries for l in c.lands],
                  sum(c.n_sems for c in carries), run("start"), run("finish"))


def _half_rows(shape, who):
    rh = shape[0] // 2
    return pl.ds(pl.multiple_of(who * rh, 16), rh)


def _gather_carry(shards, gathered, peers):
    n, p = len(shards), len(peers)

    def copies(srcs, lands, ssem, rsem):
        x, y, c = _place()
        k = 2 * x + y
        chips = _other_chips(x, y)
        sib = (x, y, 1 - c)
        ici, fwd, got = [], [], []
        for w in range(n):
            shape = shards[w].shape
            for pi, j in enumerate(peers):
                cx, cy = chips[j]
                s = w * p + pi
                ici.append(_remote(srcs[w].at[_half_rows(shape, c)], lands[w].at[k, _half_rows(shape, c)],
                                   ssem.at[s], rsem.at[s], (cx, cy, c)))
                mine = lands[w].at[2 * cx + cy, _half_rows(shape, c)]
                fwd.append(_remote(mine, mine, ssem.at[n * p + s], rsem.at[n * p + s], sib))
                theirs = lands[w].at[2 * cx + cy, _half_rows(shape, 1 - c)]
                got.append(_remote(theirs, theirs, ssem.at[n * p + s], rsem.at[n * p + s], sib))
        return ici, fwd, got

    def start(srcs, lands, ssem, rsem):
        for cp in copies(srcs, lands, ssem, rsem)[0]:
            cp.start()

    def finish(srcs, lands, ssem, rsem):
        ici, fwd, got = copies(srcs, lands, ssem, rsem)
        for a, f in zip(ici, fwd):
            a.wait_recv()
            f.start()
        for g in got:
            g.wait_recv()
        for cp in ici + fwd:
            cp.wait_send()

    return _Carry(shards, gathered, 2 * n * p, start, finish)


def _sibling_carry(dw):
    def copies(srcs, lands, ssem, rsem):
        x, y, c = _place()
        return [_remote(srcs[0].at[j, 1 - c], lands[0].at[j], ssem.at[j], rsem.at[j], (x, y, 1 - c))
                for j in range(N_CHIPS)]

    def start(*a):
        for cp in copies(*a):
            cp.start()

    def finish(*a):
        for cp in copies(*a):
            cp.wait()

    return _Carry([dw], [jax.ShapeDtypeStruct((N_CHIPS,) + dw.shape[2:], dw.dtype)], N_CHIPS, start, finish)


def _chips_carry(sums, land, peers):
    def copies(srcs, lands, ssem, rsem):
        x, y, c = _place()
        chips = _other_chips(x, y)
        out = []
        for pi, j in enumerate(peers):
            cx, cy = chips[j]
            out.append(_remote(srcs[0].at[2 * cx + cy], lands[0].at[j], ssem.at[pi], rsem.at[pi], (cx, cy, c)))
        return out

    def start(*a):
        for cp in copies(*a):
            cp.start()

    def finish(*a):
        for cp in copies(*a):
            cp.wait()

    if land is None:
        land = jax.ShapeDtypeStruct((3,) + sums.shape[1:], sums.dtype)
    return _Carry([sums], [land], len(peers), start, finish)


def _call(name, grid, body, ins, in_specs, out_shapes, out_specs, scratch=(), vmem_mb=48, carry=None):
    n_in, n_out, n_sc = len(ins), len(out_shapes), len(scratch)
    ins, in_specs = list(ins), list(in_specs)
    out_shapes, out_specs, scratch = list(out_shapes), list(out_specs), list(scratch)
    aliases = {}
    if carry is not None:
        ins += carry.srcs
        in_specs += [ANY] * len(carry.srcs)
        for land in carry.lands:
            if not isinstance(land, jax.ShapeDtypeStruct):
                aliases[len(ins)] = len(out_shapes)
                ins.append(land)
                in_specs.append(ANY)
                land = jax.ShapeDtypeStruct(land.shape, land.dtype)
            out_shapes.append(land)
            out_specs.append(ANY)
        scratch += [pltpu.SemaphoreType.DMA((carry.n_sems,)), pltpu.SemaphoreType.DMA((carry.n_sems,))]
    n_in_all, n_out_all = len(ins), len(out_shapes)

    def kbody(*refs):
        in_refs, out_refs, sc = refs[:n_in_all], refs[n_in_all:n_in_all + n_out_all], refs[n_in_all + n_out_all:]
        ids = tuple(pl.program_id(a) for a in range(len(grid)))
        if carry is not None:
            first = functools.reduce(jnp.logical_and, [i == 0 for i in ids])
            last = functools.reduce(jnp.logical_and, [i == g - 1 for i, g in zip(ids, grid)])
            comm = (in_refs[n_in:n_in + len(carry.srcs)], out_refs[n_out:], sc[n_sc], sc[n_sc + 1])

            @pl.when(first)
            def _():
                carry.start(*comm)

        body(in_refs[:n_in], out_refs[:n_out], sc[:n_sc], ids)
        if carry is not None:
            @pl.when(last)
            def _():
                carry.finish(*comm)

    return pl.pallas_call(
        kbody, name=name, grid=grid, in_specs=in_specs, out_specs=out_specs, out_shape=out_shapes,
        scratch_shapes=scratch, input_output_aliases=aliases,
        compiler_params=pltpu.CompilerParams(dimension_semantics=("arbitrary",) * len(grid),
                                             vmem_limit_bytes=vmem_mb * MIB),
    )(*ins)


def _matmul(name, grid, kaxis, ins, in_specs, out_shapes, out_specs, dims, epi, tail=None, acc_shape=None, nc=512,
            vmem_mb=48, carry=None):
    nk = grid[kaxis] if kaxis is not None else 1

    def body(in_refs, out_refs, scratch, ids):
        a = in_refs[0][...]
        b_ref = in_refs[1]
        n = b_ref.shape[0] if dims == NT else b_ref.shape[1]

        def prod(c0):
            b = b_ref[c0:c0 + nc, :] if dims == NT else b_ref[:, c0:c0 + nc]
            return lax.dot_general(a, b, (dims, ((), ())), preferred_element_type=F32)

        if kaxis is None:
            for c0 in range(0, n, nc):
                epi(prod(c0), slice(c0, c0 + nc), ids, in_refs[2:], out_refs)
            if tail is not None:
                tail(ids, in_refs[2:], out_refs)
        else:
            acc = scratch[0]
            _zero_when(ids[kaxis] == 0, acc)
            for c0 in range(0, n, nc):
                acc[:, c0:c0 + nc] += prod(c0)

            @pl.when(ids[kaxis] == nk - 1)
            def _():
                epi(acc, ids, in_refs[2:], out_refs)

    return _call(name, grid, body, ins, in_specs, out_shapes, out_specs,
                 [pltpu.VMEM(acc_shape, F32)] if kaxis is not None else [], vmem_mb, carry)


def _row_rsqrt(xf):
    return lax.rsqrt(jnp.mean(xf * xf, axis=-1, keepdims=True) + EPS)


def _norm_bwd(dh, xf, g, resid):
    r = _row_rsqrt(xf)
    xh = xf * r
    dg = jnp.sum(dh * xh, axis=0, keepdims=True)
    dxh = dh * g
    dx = resid + r * (dxh - xh * jnp.mean(dxh * xh, axis=-1, keepdims=True))
    return dx, dg


def _for_rows(n, fn, sub=128):
    def step(q, carry):
        fn(pl.ds(pl.multiple_of(q * sub, sub), sub))
        return carry

    lax.fori_loop(0, n // sub, step, 0)


def _zero_when(first, *refs):
    @pl.when(first)
    def _():
        for ref in refs:
            ref[...] = jnp.zeros_like(ref)


_GELU_K = 0.7978845608028654
_GELU_C = 0.044715


def _gelu(x):
    t = jnp.tanh(_GELU_K * (x + _GELU_C * x * x * x))
    return 0.5 * x * (1.0 + t)


def _gelu_and_grad(x):
    x2 = x * x
    t = jnp.tanh(_GELU_K * (x + _GELU_C * x * x2))
    g = 0.5 * x * (1.0 + t)
    dg = 0.5 * (1.0 + t) + 0.5 * x * (1.0 - t * t) * (_GELU_K * (1.0 + 3.0 * _GELU_C * x2))
    return g, dg


def _window_sum(ext, w, causal):
    n = ext.shape[0]
    s, d = ext, 1
    while d < w:
        s = s + pltpu.roll(s, d if causal else n - d, 0)
        d *= 2
    return s


def _inv_count(t, w):
    return 1.0 / jnp.minimum(t + 1, w).astype(F32)


def _pooled(z_ref, zh_ref, g, w, i, tm):
    cols = slice(GD * g, GD * (g + 1))
    zb = z_ref[:, cols]
    zh = jnp.where(i > 0, zh_ref[:, cols], 0.0)
    ext = jnp.concatenate([zh, zb], axis=0)
    s = _window_sum(ext, w, True)[HALO:, :]
    t = i * tm + lax.broadcasted_iota(jnp.int32, (tm, 1), 0)
    return s * _inv_count(t, w) - zb


def _full(shape, n_axes=1):
    return pl.BlockSpec(shape, lambda *ids: (0,) * len(shape))


def _norm1(x, g, tm):
    s = x.shape[0]

    def body(ins, outs, scratch, ids):
        xf = ins[0][...]
        outs[0][...] = (xf * _row_rsqrt(xf) * ins[1][...]).astype(BF16)

    row = pl.BlockSpec((tm, D), lambda i: (i, 0))
    return _call("norm1", (s // tm,), body, [x, g], [row, _full((1, D))],
                 [jax.ShapeDtypeStruct((s, D), BF16)], [row], vmem_mb=32)[0]


def _mixer_fwd(proj, wt, bst, gv, wpool, scale, tm, carry):
    s = proj.shape[0]
    nq = tm // HD

    def body(ins, outs, scratch, ids):
        up_ref, vp_ref, z_ref, zh_ref, wt_ref, bst_ref, gv_ref, wp_ref, sc_ref = ins
        out_ref = outs[0]
        i = ids[0]
        for h in range(HEADS):
            cols = slice(HD * h, HD * (h + 1))
            gvh = gv_ref[:, cols]
            bcol = bst_ref[h]
            wth = wt_ref[h]

            def chunk(q, c_):
                rows = pl.ds(pl.multiple_of(q * HD, HD), HD)
                u = _gelu(up_ref[rows, cols])
                v = _gelu(vp_ref[rows, cols])
                vh = (v * _row_rsqrt(v) * gvh).astype(BF16)
                mixed = jnp.dot(wth, vh, preferred_element_type=F32) + bcol
                out_ref[rows, cols] = (u * mixed).astype(BF16)
                return c_

            lax.fori_loop(0, nq, chunk, 0)
        for g, w in enumerate(POOL_WINDOWS):
            cols = slice(GD * g, GD * (g + 1))
            pooled = _pooled(z_ref, zh_ref, g, w, i, tm)
            yv = jnp.dot(pooled.astype(BF16), wp_ref[g], preferred_element_type=F32)
            out_ref[:, A_W + GD * g:A_W + GD * (g + 1)] = (yv * sc_ref[:, cols]).astype(BF16)

    hb = tm // HALO
    return _call(
        "mixer_fwd", (s // tm,), body, [proj, proj, proj, proj, wt, bst, gv, wpool, scale],
        [pl.BlockSpec((tm, A_W), lambda i: (i, 0)),
         pl.BlockSpec((tm, A_W), lambda i: (i, 1)),
         pl.BlockSpec((tm, A_W), lambda i: (i, 2)),
         pl.BlockSpec((HALO, A_W), lambda i: (jnp.maximum(i * hb - 1, 0), 2)),
         _full((HEADS, HD, HD)), _full((HEADS, HD, HD)), _full((1, A_W)), _full((4, GD, GD)), _full((1, A_W))],
        [jax.ShapeDtypeStruct((s, D), BF16)], [pl.BlockSpec((tm, D), lambda i: (i, 0))], vmem_mb=40, carry=carry)


def _mixer_bwd(proj, dmix, wt, wtt, bst, gv, wpool, scale, tm, carry):
    s = proj.shape[0]
    nb = s // tm
    nq = tm // HD
    hb = tm // HALO

    def body(ins, outs, scratch, ids):
        (up_ref, vp_ref, z_ref, zh_ref, doa_ref, dob_ref, dobh_ref, wt_ref, wtt_ref, bst_ref, gv_ref, wp_ref,
         sc_ref) = ins
        dproj_ref, dws_ref, dbs_ref, dgv_ref, dwp_ref, dsc_ref = outs
        dbfull = scratch[0]
        i = ids[0]
        last = i == nb - 1
        _zero_when(i == 0, dws_ref, dbfull, dgv_ref, dwp_ref, dsc_ref)

        for h in range(HEADS):
            cols = slice(HD * h, HD * (h + 1))
            gvh = gv_ref[:, cols]
            bcol = bst_ref[h]
            wth = wt_ref[h]
            wtth = wtt_ref[h]

            def chunk(q, c_):
                rows = pl.ds(pl.multiple_of(q * HD, HD), HD)
                u, du_dup = _gelu_and_grad(up_ref[rows, cols])
                v, dv_dvp = _gelu_and_grad(vp_ref[rows, cols])
                rv = _row_rsqrt(v)
                vn = v * rv
                vh = (vn * gvh).astype(BF16)
                mixed = jnp.dot(wth, vh, preferred_element_type=F32) + bcol
                doa = doa_ref[rows, cols].astype(F32)
                dmx = doa * u
                dmxb = dmx.astype(BF16)
                dbfull[h] += dmx
                dws_ref[h] += lax.dot_general(dmxb, vh, (NT, ((), ())), preferred_element_type=F32)
                dvh = jnp.dot(wtth, dmxb, preferred_element_type=F32)
                dgv_ref[:, cols] += jnp.sum(dvh * vn, axis=0, keepdims=True)
                dvn = dvh * gvh
                dv = rv * (dvn - vn * jnp.mean(dvn * vn, axis=-1, keepdims=True))
                dproj_ref[rows, cols] = (doa * mixed * du_dup).astype(BF16)
                dproj_ref[rows, A_W + HD * h:A_W + HD * (h + 1)] = (dv * dv_dvp).astype(BF16)
                return c_

            lax.fori_loop(0, nq, chunk, 0)

        t = i * tm + lax.broadcasted_iota(jnp.int32, (tm, 1), 0)
        th = (i + 1) * tm + lax.broadcasted_iota(jnp.int32, (HALO, 1), 0)
        for g, w in enumerate(POOL_WINDOWS):
            cols = slice(GD * g, GD * (g + 1))
            wpg = wp_ref[g]
            scg = sc_ref[:, cols]
            pb = _pooled(z_ref, zh_ref, g, w, i, tm).astype(BF16)
            ypre = jnp.dot(pb, wpg, preferred_element_type=F32)
            dob = dob_ref[:, cols].astype(F32)
            dsc_ref[:, cols] += jnp.sum(dob * ypre, axis=0, keepdims=True)
            dyb = (dob * scg).astype(BF16)
            dwp_ref[g] += lax.dot_general(pb, dyb, (TN, ((), ())), preferred_element_type=F32)
            dpo = lax.dot_general(dyb, wpg, (NT, ((), ())), preferred_element_type=F32)
            dyh = (dobh_ref[:, cols].astype(F32) * scg).astype(BF16)
            dpoh = lax.dot_general(dyh, wpg, (NT, ((), ())), preferred_element_type=F32)
            dpoh = jnp.where(last, 0.0, dpoh * _inv_count(th, w))
            ext = jnp.concatenate([dpo * _inv_count(t, w), dpoh], axis=0)
            dz = _window_sum(ext, w, False)[:tm, :] - dpo
            dproj_ref[:, 2 * A_W + GD * g:2 * A_W + GD * (g + 1)] = dz.astype(BF16)

        @pl.when(last)
        def _():
            r = lax.broadcasted_iota(jnp.int32, (HD, HD), 0)
            c = lax.broadcasted_iota(jnp.int32, (HD, HD), 1)
            for h in range(HEADS):
                dws_ref[h] = jnp.where(r >= c, dws_ref[h], 0.0)
                dbs_ref[h] = jnp.sum(dbfull[h], axis=-1, keepdims=True)

    return _call(
        "mixer_bwd", (nb,), body, [proj, proj, proj, proj, dmix, dmix, dmix, wt, wtt, bst, gv, wpool, scale],
        [pl.BlockSpec((tm, A_W), lambda i: (i, 0)),
         pl.BlockSpec((tm, A_W), lambda i: (i, 1)),
         pl.BlockSpec((tm, A_W), lambda i: (i, 2)),
         pl.BlockSpec((HALO, A_W), lambda i: (jnp.maximum(i * hb - 1, 0), 2)),
         pl.BlockSpec((tm, A_W), lambda i: (i, 0)),
         pl.BlockSpec((tm, A_W), lambda i: (i, 1)),
         pl.BlockSpec((HALO, A_W), lambda i: (jnp.minimum((i + 1) * hb, s // HALO - 1), 1)),
         _full((HEADS, HD, HD)), _full((HEADS, HD, HD)), _full((HEADS, HD, HD)), _full((1, A_W)),
         _full((4, GD, GD)), _full((1, A_W))],
        [jax.ShapeDtypeStruct((s, IN_W), BF16),
         jax.ShapeDtypeStruct((HEADS, HD, HD), F32),
         jax.ShapeDtypeStruct((HEADS, HD, 1), F32),
         jax.ShapeDtypeStruct((1, A_W), F32),
         jax.ShapeDtypeStruct((4, GD, GD), F32),
         jax.ShapeDtypeStruct((1, A_W), F32)],
        [pl.BlockSpec((tm, IN_W), lambda i: (i, 0)),
         _full((HEADS, HD, HD)), _full((HEADS, HD, 1)), _full((1, A_W)), _full((4, GD, GD)), _full((1, A_W))],
        [pltpu.VMEM((HEADS, HD, HD), F32)], vmem_mb=48, carry=carry)


def _inproj(h1, win_g, tm, carry):
    s = h1.shape[0]
    cw = IN_W // N_CHIPS

    def epi(p, cols, ids, extra, outs):
        outs[0][:, cols] = p

    return _matmul(
        "inproj", (N_CHIPS, s // tm), None, [h1, win_g],
        [pl.BlockSpec((tm, D), lambda j, i: (i, 0)), pl.BlockSpec((None, D, cw), lambda j, i: (j, 0, 0))],
        [jax.ShapeDtypeStruct((s, IN_W), F32)], [pl.BlockSpec((tm, cw), lambda j, i: (i, j))], NN, epi,
        nc=256, vmem_mb=32, carry=carry)


def _outproj(mixed, wout, x, g_ffn, tm, carry):
    s = x.shape[0]

    def epi(p, cols, ids, extra, outs):
        outs[0][:, cols] = extra[0][:, cols] + p

    def tail(ids, extra, outs):
        x2 = outs[0][...]
        outs[1][...] = (x2 * _row_rsqrt(x2) * extra[1][...]).astype(BF16)

    row = pl.BlockSpec((tm, D), lambda i: (i, 0))
    return _matmul(
        "outproj", (s // tm,), None, [mixed, wout, x, g_ffn], [row, _full((D, D)), row, _full((1, D))],
        [jax.ShapeDtypeStruct((s, D), F32), jax.ShapeDtypeStruct((s, D), BF16)], [row, row], NN, epi, tail,
        vmem_mb=48, carry=carry)


def _up(h2, wup_g, tm, tn, carry):
    s = h2.shape[0]
    per = D // tn

    def epi(p, cols, ids, extra, outs):
        a = jnp.maximum(p, 0.0)
        outs[0][:, cols] = (a * a).astype(BF16)

    return _matmul(
        "up", (D_FF // tn, s // tm), None, [h2, wup_g],
        [pl.BlockSpec((tm, D), lambda j, i: (i, 0)), pl.BlockSpec((None, D, tn), lambda j, i: (j // per, 0, j % per))],
        [jax.ShapeDtypeStruct((s, D_FF), BF16)], [pl.BlockSpec((tm, tn), lambda j, i: (i, j))], NN, epi,
        vmem_mb=40, carry=carry)


def _down(act, wdown, x2, tgt, g_final, tm, tk):
    s = x2.shape[0]

    def epi(acc, ids, extra, outs):
        x2_ref, t_ref, g_ref = extra
        dx_ref, dxb_ref, dgf_ref, loss_ref = outs
        g = g_ref[...]
        _zero_when(ids[0] == 0, dgf_ref, loss_ref)

        def block(rows):
            x3 = x2_ref[rows, :] + acc[rows, :]
            r = _row_rsqrt(x3)
            xh = x3 * r
            diff = xh * g - t_ref[rows, :]
            dy = diff * (1.0 / D)
            dxh = dy * g
            dx = r * (dxh - xh * jnp.mean(dxh * xh, axis=-1, keepdims=True))
            dx_ref[rows, :] = dx
            dxb_ref[rows, :] = dx.astype(BF16)
            dgf_ref[...] += jnp.sum(dy * xh, axis=0, keepdims=True)
            loss_ref[...] += jnp.sum(diff * diff, axis=0, keepdims=True)

        _for_rows(tm, block)

    row = pl.BlockSpec((tm, D), lambda i, k: (i, 0))
    vec = pl.BlockSpec((1, D), lambda i, k: (0, 0))
    return _matmul(
        "down", (s // tm, D_FF // tk), 1, [act, wdown, x2, tgt, g_final],
        [pl.BlockSpec((tm, tk), lambda i, k: (i, k)), pl.BlockSpec((tk, D), lambda i, k: (k, 0)), row, row, vec],
        [jax.ShapeDtypeStruct((s, D), F32), jax.ShapeDtypeStruct((s, D), BF16),
         jax.ShapeDtypeStruct((1, D), F32), jax.ShapeDtypeStruct((1, D), F32)],
        [row, row, vec, vec], NN, epi, acc_shape=(tm, D), vmem_mb=58)


def _dact(dx3b, wdown, act, tm, tn, carry):
    s = dx3b.shape[0]

    def epi(p, cols, ids, extra, outs):
        outs[0][:, cols] = (p * (2.0 * jnp.sqrt(extra[0][:, cols].astype(F32)))).astype(BF16)

    tile = pl.BlockSpec((tm, tn), lambda j, i: (i, j))
    return _matmul(
        "dact", (D_FF // tn, s // tm), None, [dx3b, wdown, act],
        [pl.BlockSpec((tm, D), lambda j, i: (i, 0)), pl.BlockSpec((tn, D), lambda j, i: (j, 0)), tile],
        [jax.ShapeDtypeStruct((s, D_FF), BF16)], [tile], NT, epi, vmem_mb=40, carry=carry)


def _dweight(name, lhs, rhs, n_shards, rows, cols, tm, tk, nc=512, carry=None):
    s = lhs.shape[0]

    def epi(acc, ids, extra, outs):
        outs[0][...] = acc[...].astype(BF16)

    return _matmul(
        name, (n_shards, rows // tm, s // tk), 2, [lhs, rhs],
        [pl.BlockSpec((tk, tm), lambda j, i, k: (k, i)), pl.BlockSpec((tk, cols), lambda j, i, k: (k, j))],
        [jax.ShapeDtypeStruct((n_shards, rows, cols), BF16)],
        [pl.BlockSpec((None, tm, cols), lambda j, i, k: (j, i, 0))], TN, epi, acc_shape=(tm, cols), nc=nc,
        vmem_mb=40, carry=carry)


def _dh2(da, wup_g, x2, dx3, g_ffn, tm, tk, carry):
    s = x2.shape[0]
    per = D // tk

    def epi(acc, ids, extra, outs):
        x2_ref, dx3_ref, g_ref = extra
        g = g_ref[...]
        _zero_when(ids[0] == 0, outs[2])

        def block(rows):
            dx, dg = _norm_bwd(acc[rows, :], x2_ref[rows, :], g, dx3_ref[rows, :])
            outs[0][rows, :] = dx
            outs[1][rows, :] = dx.astype(BF16)
            outs[2][...] += dg

        _for_rows(tm, block)

    row = pl.BlockSpec((tm, D), lambda i, k: (i, 0))
    vec = pl.BlockSpec((1, D), lambda i, k: (0, 0))
    return _matmul(
        "dh2", (s // tm, D_FF // tk), 1, [da, wup_g, x2, dx3, g_ffn],
        [pl.BlockSpec((tm, tk), lambda i, k: (i, k)),
         pl.BlockSpec((None, D, tk), lambda i, k: (k // per, 0, k % per)), row, row, vec],
        [jax.ShapeDtypeStruct((s, D), F32), jax.ShapeDtypeStruct((s, D), BF16), jax.ShapeDtypeStruct((1, D), F32)],
        [row, row, vec], NT, epi, acc_shape=(tm, D), vmem_mb=58, carry=carry)


def _dmixed(dx2b, wout, tm, carry):
    s = dx2b.shape[0]

    def epi(p, cols, ids, extra, outs):
        outs[0][:, cols] = p.astype(BF16)

    row = pl.BlockSpec((tm, D), lambda i: (i, 0))
    return _matmul(
        "dmixed", (s // tm,), None, [dx2b, wout], [row, _full((D, D))],
        [jax.ShapeDtypeStruct((s, D), BF16)], [row], NT, epi, vmem_mb=40, carry=carry)


def _dh1(dproj, win_g, x, dx2, g_mix, tm, carry):
    s = x.shape[0]
    cw = IN_W // N_CHIPS

    def epi(acc, ids, extra, outs):
        x_ref, dx2_ref, g_ref = extra
        g = g_ref[...]
        _zero_when(ids[0] == 0, outs[1])

        def block(rows):
            dx, dg = _norm_bwd(acc[rows, :], x_ref[rows, :], g, dx2_ref[rows, :])
            outs[0][rows, :] = dx
            outs[1][...] += dg

        _for_rows(tm, block)

    row = pl.BlockSpec((tm, D), lambda i, j: (i, 0))
    vec = pl.BlockSpec((1, D), lambda i, j: (0, 0))
    return _matmul(
        "dh1", (s // tm, N_CHIPS), 1, [dproj, win_g, x, dx2, g_mix],
        [pl.BlockSpec((tm, cw), lambda i, j: (i, j)), pl.BlockSpec((None, D, cw), lambda i, j: (j, 0, 0)),
         row, row, vec],
        [jax.ShapeDtypeStruct((s, D), F32), jax.ShapeDtypeStruct((1, D), F32)],
        [row, vec], NT, epi, acc_shape=(tm, D), vmem_mb=52, carry=carry)


def _allgather_first(shards, n_full):
    n = len(shards)
    gather = _gather_carry(shards[:n_full], [None] * n_full, (0, 1, 2))

    def body(*refs):
        ins, outs = refs[:n], refs[n:2 * n]
        send_sems, recv_sems, local_sems = refs[2 * n:]
        x, y, _ = _place()
        local = []
        for w in range(n):
            cp = pltpu.make_async_copy(ins[w], outs[w].at[2 * x + y], local_sems.at[w])
            cp.start()
            local.append(cp)
        gather.start(ins[:n_full], outs[:n_full], send_sems, recv_sems)
        gather.finish(ins[:n_full], outs[:n_full], send_sems, recv_sems)
        for cp in local:
            cp.wait()

    return pl.pallas_call(
        body, name="allgather_first",
        in_specs=[ANY] * n, out_specs=[ANY] * n,
        out_shape=[jax.ShapeDtypeStruct((N_CHIPS,) + a.shape, a.dtype) for a in shards],
        scratch_shapes=[pltpu.SemaphoreType.DMA((gather.n_sems,)), pltpu.SemaphoreType.DMA((gather.n_sems,)),
                        pltpu.SemaphoreType.DMA((n,))],
    )(*shards)


def _small_allreduce(part):
    rows = part.shape[0]
    rh = rows // 2

    def body(p_ref, o_ref, sib_ref, slots, send_sems, recv_sems):
        x, y, c = _place()
        k = 2 * x + y
        sib = (x, y, 1 - c)
        half = pl.ds(pl.multiple_of(c * rh, 8), rh)
        cp = _remote(p_ref, sib_ref, send_sems.at[0], recv_sems.at[0], sib)
        cp.start()
        cp.wait()
        slots[k] = p_ref[half, :] + sib_ref[half, :]
        cps = []
        for j, (cx, cy) in enumerate(_other_chips(x, y)):
            cp = _remote(slots.at[k], slots.at[k], send_sems.at[1 + j], recv_sems.at[1 + j], (cx, cy, c))
            cp.start()
            cps.append(cp)
        for cp in cps:
            cp.wait()
        o_ref[half, :] = ((slots[0] + slots[1]) + slots[2]) + slots[3]
        cp = _remote(o_ref.at[half], o_ref.at[half], send_sems.at[4], recv_sems.at[4], sib)
        cp.start()
        cp.wait()

    vm = pl.BlockSpec(memory_space=pltpu.VMEM)
    return pl.pallas_call(
        body, name="small_allreduce", in_specs=[vm], out_specs=vm,
        out_shape=jax.ShapeDtypeStruct(part.shape, F32),
        scratch_shapes=[pltpu.VMEM(part.shape, F32), pltpu.VMEM((N_CHIPS, rh, LANES), F32),
                        pltpu.SemaphoreType.DMA((5,)), pltpu.SemaphoreType.DMA((5,))],
        compiler_params=pltpu.CompilerParams(vmem_limit_bytes=32 * MIB),
    )(part)


def _comm_only(name, carry):
    ns, nl = len(carry.srcs), len(carry.lands)
    lands_in = [l for l in carry.lands if not isinstance(l, jax.ShapeDtypeStruct)]
    assert len(lands_in) in (0, nl)

    def body(*refs):
        srcs = refs[:ns]
        lands = refs[ns + len(lands_in):ns + len(lands_in) + nl]
        ssem, rsem = refs[ns + len(lands_in) + nl:]
        carry.start(srcs, lands, ssem, rsem)
        carry.finish(srcs, lands, ssem, rsem)

    return pl.pallas_call(
        body, name=name, in_specs=[ANY] * (ns + len(lands_in)), out_specs=[ANY] * nl,
        out_shape=[jax.ShapeDtypeStruct(l.shape, l.dtype) for l in carry.lands],
        input_output_aliases={ns + i: i for i in range(len(lands_in))},
        scratch_shapes=[pltpu.SemaphoreType.DMA((carry.n_sems,)), pltpu.SemaphoreType.DMA((carry.n_sems,))],
    )(*carry.srcs, *lands_in)


def _share_carry(grads):
    n = len(grads)

    def copies(srcs, lands, ssem, rsem):
        x, y, c = _place()
        return [_remote(lands[w].at[c], lands[w].at[c], ssem.at[w], rsem.at[w], (x, y, 1 - c)) for w in range(n)]

    def start(*a):
        for cp in copies(*a):
            cp.start()

    def finish(*a):
        for cp in copies(*a):
            cp.wait()

    return _Carry([], grads, n, start, finish)


def _add_sibling(place, dw, got, tr):
    _, _, rh, cols = dw.shape

    def body(pref, a_ref, b_ref, o_ref):
        o_ref[...] = (a_ref[...].astype(F32) + b_ref[...].astype(F32)).astype(BF16)

    return pl.pallas_call(
        body, name="add_sibling",
        grid_spec=pltpu.PrefetchScalarGridSpec(
            num_scalar_prefetch=1, grid=(N_CHIPS, rh // tr),
            in_specs=[pl.BlockSpec((None, None, tr, cols), lambda j, r, p: (j, p[1], r, 0)),
                      pl.BlockSpec((None, tr, cols), lambda j, r, p: (j, r, 0))],
            out_specs=pl.BlockSpec((None, tr, cols), lambda j, r, p: (j, r, 0))),
        out_shape=jax.ShapeDtypeStruct((N_CHIPS, rh, cols), BF16),
        compiler_params=pltpu.CompilerParams(dimension_semantics=("arbitrary",) * 2, vmem_limit_bytes=32 * MIB),
    )(place, dw, got)


def _add_chips(place, sums, got, tr):
    _, rh, cols = sums.shape

    def body(pref, a_ref, b_ref, o_ref):
        b = b_ref[...].astype(F32)
        o_ref[...] = ((a_ref[...].astype(F32) + b[0]) + b[1]) + b[2]

    return pl.pallas_call(
        body, name="add_chips",
        grid_spec=pltpu.PrefetchScalarGridSpec(
            num_scalar_prefetch=1, grid=(rh // tr,),
            in_specs=[pl.BlockSpec((None, tr, cols), lambda r, p: (p[0], r, 0)),
                      pl.BlockSpec((3, tr, cols), lambda r, p: (0, r, 0))],
            out_specs=pl.BlockSpec((None, tr, cols), lambda r, p: (p[1], r, 0))),
        out_shape=jax.ShapeDtypeStruct((2, rh, cols), F32),
        compiler_params=pltpu.CompilerParams(dimension_semantics=("arbitrary",), vmem_limit_bytes=32 * MIB),
    )(place, sums, got)


def _adamw_math(w, g, m, v):
    m = ADAM_B1 * m + (1.0 - ADAM_B1) * g
    v = ADAM_B2 * v + (1.0 - ADAM_B2) * (g * g)
    m_hat = m / (1.0 - ADAM_B1 ** ADAM_STEP)
    v_hat = v / (1.0 - ADAM_B2 ** ADAM_STEP)
    delta = -ADAM_LR * (m_hat / (jnp.sqrt(v_hat) + ADAM_EPS) + ADAM_WD * w)
    return delta, m, v


def _adamw(name, w, g, m, v, tr, carry=None):
    rows, cols = w.shape

    def body(ins, outs, scratch, ids):
        outs[0][...], outs[1][...], outs[2][...] = _adamw_math(ins[0][...], ins[1][...], ins[2][...], ins[3][...])

    blk = pl.BlockSpec((tr, cols), lambda r: (r, 0))
    return _call(name, (rows // tr,), body, [w, g, m, v], [blk] * 4,
                 [jax.ShapeDtypeStruct(w.shape, F32)] * 3, [blk] * 3, vmem_mb=32, carry=carry)


def _rows(a):
    return a.reshape(-1, LANES)


def kernel(x, g_mix, w_in, g_v, w_s, b_s, w_pool, pool_scale, w_out, g_ffn, w_up, w_down, g_final, loss_target, m_g_mix, m_w_in, m_g_v, m_w_s, m_b_s, m_w_pool, m_pool_scale, m_w_out, m_g_ffn, m_w_up, m_w_down, m_g_final, v_g_mix, v_w_in, v_g_v, v_w_s, v_b_s, v_w_pool, v_pool_scale, v_w_out, v_g_ffn, v_w_up, v_w_down, v_g_final):
    tm = 512
    xs = x[0]
    tgt = loss_target[0]
    cx, cy, cc = _place()
    chip = 2 * cx + cy
    place = jnp.stack([chip, cc]).astype(jnp.int32)
    everyone, near, far = (0, 1, 2), (0, 1), (2,)

    win_s, wpool_s = w_in[0].astype(BF16), w_pool[0].reshape(4 * 64, GD).astype(BF16)
    wout_s, wup_s, wdown_s = w_out[0].astype(BF16), w_up[0].astype(BF16), w_down[0].astype(BF16)
    win_g, wpool_g, wout_g, wup_g, wdown_g = _allgather_first([win_s, wpool_s, wout_s, wup_s, wdown_s], 2)
    wpool_f = wpool_g.reshape(N_CHIPS, 4, 64, GD).transpose(1, 0, 2, 3).reshape(4, GD, GD)
    tril = jnp.tril(jnp.ones((HD, HD), dtype=bool))
    wt = jnp.where(tril[None], w_s[0], 0.0).astype(BF16)
    wtt = wt.transpose(0, 2, 1)
    bst = jnp.broadcast_to(b_s[0][:, :, None], (HEADS, HD, HD))
    gfin = g_final.reshape(1, D)

    h1 = _norm1(xs, g_mix, tm)
    proj, wout_g = _inproj(h1, win_g, tm, _gather_carry([wout_s], [wout_g], everyone))
    mixed, wup_g = _mixer_fwd(proj, wt, bst, g_v, wpool_f, pool_scale, tm, _gather_carry([wup_s], [wup_g], near))
    wout_f = wout_g.reshape(D, D)
    x2, h2, wup_g = _outproj(mixed, wout_f, xs, g_ffn, 256, _gather_carry([wup_s], [wup_g], far))
    act, wdown_g = _up(h2, wup_g, tm, 1024, _gather_carry([wdown_s], [wdown_g], everyone))
    wdown_f = wdown_g.reshape(D_FF, D)
    dx3, dx3b, dgf, lossv = _down(act, wdown_f, x2, tgt, gfin, tm, 1024)

    halves = lambda dw, rows, cols: dw.reshape(N_CHIPS, 2, rows // (2 * N_CHIPS), cols)
    dwdown = halves(_dweight("dw_down", act, dx3b, 1, D_FF, D, 512, 1024)[0], D_FF, D)
    da, sib_down = _dact(dx3b, wdown_f, act, tm, 1024, _sibling_carry(dwdown))
    sum_down = _add_sibling(place, dwdown, sib_down, 256)
    dx2, dx2b, dgffn, got_down = _dh2(da, wup_g, x2, dx3, g_ffn, tm, 1024, _chips_carry(sum_down, None, everyone))
    dwup = halves(_dweight("dw_up", h2, da, N_CHIPS, D, D, 512, 1024)[0], D_FF, D)
    dmix, sib_up = _dmixed(dx2b, wout_f, 256, _sibling_carry(dwup))
    sum_up = _add_sibling(place, dwup, sib_up, 256)
    dproj, dws, dbs, dgv, dwp, dsc, got_up = _mixer_bwd(proj, dmix, wt, wtt, bst, g_v, wpool_f, pool_scale, tm,
                                                        _chips_carry(sum_up, None, near))
    dwout = halves(_dweight("dw_out", mixed, dx2b, 1, D, D, 512, 1024)[0], D, D)
    dwin, got_up, sib_out = _dweight("dw_in", h1, dproj, N_CHIPS, D, IN_W // N_CHIPS, 512, 1024, nc=256,
                                     carry=_join(_chips_carry(sum_up, got_up, far), _sibling_carry(dwout)))
    dwin = halves(dwin, N_CHIPS * D, IN_W // N_CHIPS)
    sum_out = _add_sibling(place, dwout, sib_out, 256)
    grad_x, dgmix, got_out, sib_in = _dh1(dproj, win_g, xs, dx2, g_mix, tm,
                                          _join(_chips_carry(sum_out, None, everyone), _sibling_carry(dwin)))
    sum_in = _add_sibling(place, dwin, sib_in, 256)
    half_down = _add_chips(place, sum_down, got_down, 256)
    half_up = _add_chips(place, sum_up, got_up, 256)
    half_out = _add_chips(place, sum_out, got_out, 256)
    g_down, g_up, g_out = _comm_only("share_halves", _share_carry([half_down, half_up, half_out]))
    g_down, g_up, g_out = g_down.reshape(D, D), g_up.reshape(D, D), g_out.reshape(D // N_CHIPS, D)

    d_down, nm_down, nv_down, got_in = _adamw("adamw_down", w_down[0], g_down, m_w_down[0], v_w_down[0], 128,
                                              _chips_carry(sum_in, None, everyone))
    d_up, nm_up, nv_up = _adamw("adamw_up", w_up[0], g_up, m_w_up[0], v_w_up[0], 128)
    d_out, nm_out, nv_out = _adamw("adamw_out", w_out[0], g_out, m_w_out[0], v_w_out[0], 128)
    half_in = _add_chips(place, sum_in, got_in, 256)
    g_in = _comm_only("share_half_in", _share_carry([half_in]))[0].reshape(D, IN_W // N_CHIPS)
    d_in, nm_in, nv_in = _adamw("adamw_in", w_in[0], g_in, m_w_in[0], v_w_in[0], 128)

    pieces = [dgmix, dgv, dws, dbs, dwp, dsc, dgffn, dgf, lossv, jnp.zeros((8 * LANES,), F32)]
    sizes = [p.size // LANES for p in pieces]
    tot = _small_allreduce(jnp.concatenate([_rows(p) for p in pieces], axis=0))
    offs = [sum(sizes[:i]) for i in range(len(sizes))]
    take = lambda i: tot[offs[i]:offs[i] + sizes[i]]
    s_gmix, s_gv, s_ws, s_bs, s_wp, s_sc, s_gffn, s_gf = [take(i) for i in range(8)]
    loss = (0.5 / D) * jnp.sum(take(8))
    s_wp_mine = lax.dynamic_slice_in_dim(s_wp.reshape(4, GD, GD), chip * 64, 64, axis=1)
    small_g = [s_gmix, s_gv, s_ws, s_bs, _rows(s_wp_mine), s_sc, s_gffn, s_gf]
    small_w = [g_mix, g_v, w_s, b_s, w_pool, pool_scale, g_ffn, g_final]
    small_m = [m_g_mix, m_g_v, m_w_s, m_b_s, m_w_pool, m_pool_scale, m_g_ffn, m_g_final]
    small_v = [v_g_mix, v_g_v, v_w_s, v_b_s, v_w_pool, v_pool_scale, v_g_ffn, v_g_final]
    cat = lambda parts: jnp.concatenate([_rows(p) for p in parts], axis=0)
    sg = cat(small_g)
    sd, snm, snv = _adamw("adamw_small", cat(small_w), sg, cat(small_m), cat(small_v), sg.shape[0])
    ssz = [p.size // LANES for p in small_w]
    soff = [sum(ssz[:i]) for i in range(len(ssz))]
    split = lambda a: [a[soff[i]:soff[i] + ssz[i]].reshape(small_w[i].shape) for i in range(len(ssz))]
    gs, ds, nms, nvs = split(sg), split(sd), split(snm), split(snv)

    def ordered(small, w_in_, w_out_, w_up_, w_down_):
        return [small[0], w_in_[None], small[1], small[2], small[3], small[4], small[5], w_out_[None], small[6],
                w_up_[None], w_down_[None], small[7]]

    return (loss, grad_x[None],
            *ordered(gs, g_in, g_out, g_up, g_down),
            *ordered(ds, d_in, d_out, d_up, d_down),
            *ordered(nms, nm_in, nm_out, nm_up, nm_down),
            *ordered(nvs, nv_in, nv_out, nv_up, nv_down))
```

```python
import functools

import jax
import jax.numpy as jnp
from jax import lax
from jax.experimental import pallas as pl
from jax.experimental.pallas import tpu as pltpu

F32 = jnp.float32
BF16 = jnp.bfloat16
EPS = 1e-6
D = 2048
A_W = 1024
HEADS = 8
HD = 128
POOL_WINDOWS = (2, 4, 8, 16)
GD = 256
IN_W = 3072
D_FF = 8192
N_CHIPS = 4
HALO = 16
LANES = 128
MIB = 2 ** 20

ADAM_LR, ADAM_B1, ADAM_B2, ADAM_EPS, ADAM_WD, ADAM_STEP = 0.001, 0.9, 0.999, 1e-08, 0.01, 10

ANY = pl.BlockSpec(memory_space=pl.ANY)
MESH = pl.DeviceIdType.MESH

NN = ((1,), (0,))
NT = ((1,), (1,))
TN = ((0,), (0,))


def _place():
    return lax.axis_index("x"), lax.axis_index("y"), lax.axis_index("c")


def _other_chips(x, y):
    return [(1 - x, y), (x, 1 - y), (1 - x, 1 - y)]


def _remote(src, dst, send_sem, recv_sem, dev):
    return pltpu.make_async_remote_copy(src_ref=src, dst_ref=dst, send_sem=send_sem, recv_sem=recv_sem,
                                        device_id=dev, device_id_type=MESH)


class _SemView:
    def __init__(self, sems, base):
        self.sems, self.base = sems, base

    @property
    def at(self):
        return self

    def __getitem__(self, i):
        return self.sems.at[self.base + i]


class _Carry:
    def __init__(self, srcs, lands, n_sems, start, finish, middle=None):
        self.srcs, self.lands, self.n_sems, self.start, self.finish = list(srcs), list(lands), n_sems, start, finish
        self.middle = middle


def _join(*carries):
    def run(which):
        def go(srcs, lands, ssem, rsem):
            so = lo = qo = 0
            for c in carries:
                if getattr(c, which) is not None:
                    getattr(c, which)(srcs[so:so + len(c.srcs)], lands[lo:lo + len(c.lands)],
                                      _SemView(ssem, qo), _SemView(rsem, qo))
                so, lo, qo = so + len(c.srcs), lo + len(c.lands), qo + c.n_sems
        return go

    middle = run("middle") if any(c.middle is not None for c in carries) else None
    return _Carry([s for c in carries for s in c.srcs], [l for c in carries for l in c.lands],
                  sum(c.n_sems for c in carries), run("start"), run("finish"), middle)


GATHER_SEMS = 7


def _gather_copies(lands, ssem, rsem):
    x, y, c = _place()
    k, kx, ky, kd = 2 * x + y, 2 * (1 - x) + y, 2 * x + (1 - y), 2 * (1 - x) + (1 - y)
    to_x, to_y, sib = (1 - x, y, c), (x, 1 - y, c), (x, y, 1 - c)
    out = []
    for w, land in enumerate(lands):
        rh = land.shape[1] // 2
        rq = rh // 2
        half = pl.ds(pl.multiple_of(c * rh, 16), rh)
        quarters = [pl.ds(pl.multiple_of(c * rh + q * rq, 16), rq) for q in range(2)]

        def cp(i, piece, dev, w=w):
            return _remote(piece, piece, ssem.at[GATHER_SEMS * w + i], rsem.at[GATHER_SEMS * w + i], dev)

        out.append(dict(
            ax=cp(0, land.at[k, half], to_x), ay=cp(1, land.at[k, half], to_y),
            rx=cp(2, land.at[kx, quarters[0]], to_y), ry=cp(3, land.at[ky, quarters[1]], to_x),
            fx=cp(4, land.at[kx, half], sib), fy=cp(5, land.at[ky, half], sib), fd=cp(6, land.at[kd, half], sib)))
    return out


def _gather_whole(gathered):
    def start(srcs, lands, ssem, rsem):
        for d in _gather_copies(lands, ssem, rsem):
            d["ax"].start()
            d["ay"].start()

    def middle(srcs, lands, ssem, rsem):
        for d in _gather_copies(lands, ssem, rsem):
            d["ax"].wait_recv()
            d["rx"].start()
            d["fx"].start()
            d["ay"].wait_recv()
            d["ry"].start()
            d["fy"].start()

    def finish(srcs, lands, ssem, rsem):
        for d in _gather_copies(lands, ssem, rsem):
            d["rx"].wait_recv()
            d["ry"].wait_recv()
            d["fd"].start()
            for name in ("fx", "fy", "fd"):
                d[name].wait_recv()
            for cp in d.values():
                cp.wait_send()

    return _Carry([], gathered, GATHER_SEMS * len(gathered), start, finish, middle)


def _gather_near(gathered):
    def start(srcs, lands, ssem, rsem):
        for d in _gather_copies(lands, ssem, rsem):
            d["ax"].start()
            d["ay"].start()

    def finish(srcs, lands, ssem, rsem):
        for d in _gather_copies(lands, ssem, rsem):
            d["ax"].wait_recv()
            d["fx"].start()
            d["ay"].wait_recv()
            d["fy"].start()
            for name in ("fx", "fy"):
                d[name].wait_recv()
            for name in ("ax", "ay", "fx", "fy"):
                d[name].wait_send()

    return _Carry([], gathered, GATHER_SEMS * len(gathered), start, finish)


def _gather_far(gathered):
    def start(srcs, lands, ssem, rsem):
        for d in _gather_copies(lands, ssem, rsem):
            d["rx"].start()
            d["ry"].start()

    def finish(srcs, lands, ssem, rsem):
        for d in _gather_copies(lands, ssem, rsem):
            d["rx"].wait_recv()
            d["ry"].wait_recv()
            d["fd"].start()
            d["fd"].wait_recv()
            for name in ("rx", "ry", "fd"):
                d[name].wait_send()

    return _Carry([], gathered, GATHER_SEMS * len(gathered), start, finish)


def _sibling_carry(dw):
    def copies(srcs, lands, ssem, rsem):
        x, y, c = _place()
        return [_remote(srcs[0].at[j, 1 - c], lands[0].at[j], ssem.at[j], rsem.at[j], (x, y, 1 - c))
                for j in range(N_CHIPS)]

    def start(*a):
        for cp in copies(*a):
            cp.start()

    def finish(*a):
        for cp in copies(*a):
            cp.wait()

    return _Carry([dw], [jax.ShapeDtypeStruct((N_CHIPS,) + dw.shape[2:], dw.dtype)], N_CHIPS, start, finish)


def _chips_carry(sums, land, peers):
    def copies(srcs, lands, ssem, rsem):
        x, y, c = _place()
        chips = _other_chips(x, y)
        out = []
        for pi, j in enumerate(peers):
            cx, cy = chips[j]
            out.append(_remote(srcs[0].at[2 * cx + cy], lands[0].at[j], ssem.at[pi], rsem.at[pi], (cx, cy, c)))
        return out

    def start(*a):
        for cp in copies(*a):
            cp.start()

    def finish(*a):
        for cp in copies(*a):
            cp.wait()

    if land is None:
        land = jax.ShapeDtypeStruct((3,) + sums.shape[1:], sums.dtype)
    return _Carry([sums], [land], len(peers), start, finish)


def _call(name, grid, body, ins, in_specs, out_shapes, out_specs, scratch=(), vmem_mb=48, carry=None):
    n_in, n_out, n_sc = len(ins), len(out_shapes), len(scratch)
    ins, in_specs = list(ins), list(in_specs)
    out_shapes, out_specs, scratch = list(out_shapes), list(out_specs), list(scratch)
    aliases = {}
    if carry is not None:
        ins += carry.srcs
        in_specs += [ANY] * len(carry.srcs)
        for land in carry.lands:
            if not isinstance(land, jax.ShapeDtypeStruct):
                aliases[len(ins)] = len(out_shapes)
                ins.append(land)
                in_specs.append(ANY)
                land = jax.ShapeDtypeStruct(land.shape, land.dtype)
            out_shapes.append(land)
            out_specs.append(ANY)
        scratch += [pltpu.SemaphoreType.DMA((carry.n_sems,)), pltpu.SemaphoreType.DMA((carry.n_sems,))]
    n_in_all, n_out_all = len(ins), len(out_shapes)

    def kbody(*refs):
        in_refs, out_refs, sc = refs[:n_in_all], refs[n_in_all:n_in_all + n_out_all], refs[n_in_all + n_out_all:]
        ids = tuple(pl.program_id(a) for a in range(len(grid)))
        if carry is not None:
            first = functools.reduce(jnp.logical_and, [i == 0 for i in ids])
            last = functools.reduce(jnp.logical_and, [i == g - 1 for i, g in zip(ids, grid)])
            comm = (in_refs[n_in:n_in + len(carry.srcs)], out_refs[n_out:], sc[n_sc], sc[n_sc + 1])

            @pl.when(first)
            def _():
                carry.start(*comm)

            if carry.middle is not None:
                step, total = ids[0], grid[0]
                for i, g in zip(ids[1:], grid[1:]):
                    step, total = step * g + i, total * g

                @pl.when(step == total // 2)
                def _():
                    carry.middle(*comm)

        body(in_refs[:n_in], out_refs[:n_out], sc[:n_sc], ids)
        if carry is not None:
            @pl.when(last)
            def _():
                carry.finish(*comm)

    return pl.pallas_call(
        kbody, name=name, grid=grid, in_specs=in_specs, out_specs=out_specs, out_shape=out_shapes,
        scratch_shapes=scratch, input_output_aliases=aliases,
        compiler_params=pltpu.CompilerParams(dimension_semantics=("arbitrary",) * len(grid),
                                             vmem_limit_bytes=vmem_mb * MIB),
    )(*ins)


def _matmul(name, grid, kaxis, ins, in_specs, out_shapes, out_specs, dims, epi, tail=None, acc_shape=None, nc=512,
            vmem_mb=48, carry=None):
    nk = grid[kaxis] if kaxis is not None else 1

    def body(in_refs, out_refs, scratch, ids):
        a = in_refs[0][...]
        b_ref = in_refs[1]
        n = b_ref.shape[0] if dims == NT else b_ref.shape[1]

        def prod(c0):
            b = b_ref[c0:c0 + nc, :] if dims == NT else b_ref[:, c0:c0 + nc]
            return lax.dot_general(a, b, (dims, ((), ())), preferred_element_type=F32)

        if kaxis is None:
            for c0 in range(0, n, nc):
                epi(prod(c0), slice(c0, c0 + nc), ids, in_refs[2:], out_refs)
            if tail is not None:
                tail(ids, in_refs[2:], out_refs)
        else:
            acc = scratch[0]
            _zero_when(ids[kaxis] == 0, acc)
            for c0 in range(0, n, nc):
                acc[:, c0:c0 + nc] += prod(c0)

            @pl.when(ids[kaxis] == nk - 1)
            def _():
                epi(acc, ids, in_refs[2:], out_refs)

    return _call(name, grid, body, ins, in_specs, out_shapes, out_specs,
                 [pltpu.VMEM(acc_shape, F32)] if kaxis is not None else [], vmem_mb, carry)


def _row_rsqrt(xf):
    return lax.rsqrt(jnp.mean(xf * xf, axis=-1, keepdims=True) + EPS)


def _norm_bwd(dh, xf, g, resid):
    r = _row_rsqrt(xf)
    xh = xf * r
    dg = jnp.sum(dh * xh, axis=0, keepdims=True)
    dxh = dh * g
    dx = resid + r * (dxh - xh * jnp.mean(dxh * xh, axis=-1, keepdims=True))
    return dx, dg


def _for_rows(n, fn, sub=128):
    def step(q, carry):
        fn(pl.ds(pl.multiple_of(q * sub, sub), sub))
        return carry

    lax.fori_loop(0, n // sub, step, 0)


def _zero_when(first, *refs):
    @pl.when(first)
    def _():
        for ref in refs:
            ref[...] = jnp.zeros_like(ref)


_GELU_K = 0.7978845608028654
_GELU_C = 0.044715


def _gelu(x):
    t = jnp.tanh(_GELU_K * (x + _GELU_C * x * x * x))
    return 0.5 * x * (1.0 + t)


def _gelu_and_grad(x):
    x2 = x * x
    t = jnp.tanh(_GELU_K * (x + _GELU_C * x * x2))
    g = 0.5 * x * (1.0 + t)
    dg = 0.5 * (1.0 + t) + 0.5 * x * (1.0 - t * t) * (_GELU_K * (1.0 + 3.0 * _GELU_C * x2))
    return g, dg


def _window_sum(ext, w, causal):
    n = ext.shape[0]
    s, d = ext, 1
    while d < w:
        s = s + pltpu.roll(s, d if causal else n - d, 0)
        d *= 2
    return s


def _inv_count(t, w):
    return 1.0 / jnp.minimum(t + 1, w).astype(F32)


def _pooled(z_ref, zh_ref, g, w, i, tm):
    cols = slice(GD * g, GD * (g + 1))
    zb = z_ref[:, cols]
    zh = jnp.where(i > 0, zh_ref[:, cols], 0.0)
    ext = jnp.concatenate([zh, zb], axis=0)
    s = _window_sum(ext, w, True)[HALO:, :]
    t = i * tm + lax.broadcasted_iota(jnp.int32, (tm, 1), 0)
    return s * _inv_count(t, w) - zb


def _full(shape, n_axes=1):
    return pl.BlockSpec(shape, lambda *ids: (0,) * len(shape))


def _norm1(x, g, tm):
    s = x.shape[0]

    def body(ins, outs, scratch, ids):
        xf = ins[0][...]
        outs[0][...] = (xf * _row_rsqrt(xf) * ins[1][...]).astype(BF16)

    row = pl.BlockSpec((tm, D), lambda i: (i, 0))
    return _call("norm1", (s // tm,), body, [x, g], [row, _full((1, D))],
                 [jax.ShapeDtypeStruct((s, D), BF16)], [row], vmem_mb=32)[0]


def _mixer_fwd(proj, wt, bst, gv, wpool, scale, tm, carry):
    s = proj.shape[0]
    nq = tm // HD

    def body(ins, outs, scratch, ids):
        up_ref, vp_ref, z_ref, zh_ref, wt_ref, bst_ref, gv_ref, wp_ref, sc_ref = ins
        out_ref = outs[0]
        i = ids[0]
        for h in range(HEADS):
            cols = slice(HD * h, HD * (h + 1))
            gvh = gv_ref[:, cols]
            bcol = bst_ref[h]
            wth = wt_ref[h]

            def chunk(q, c_):
                rows = pl.ds(pl.multiple_of(q * HD, HD), HD)
                u = _gelu(up_ref[rows, cols])
                v = _gelu(vp_ref[rows, cols])
                vh = (v * _row_rsqrt(v) * gvh).astype(BF16)
                mixed = jnp.dot(wth, vh, preferred_element_type=F32) + bcol
                out_ref[rows, cols] = (u * mixed).astype(BF16)
                return c_

            lax.fori_loop(0, nq, chunk, 0)
        for g, w in enumerate(POOL_WINDOWS):
            cols = slice(GD * g, GD * (g + 1))
            pooled = _pooled(z_ref, zh_ref, g, w, i, tm)
            yv = jnp.dot(pooled.astype(BF16), wp_ref[g], preferred_element_type=F32)
            out_ref[:, A_W + GD * g:A_W + GD * (g + 1)] = (yv * sc_ref[:, cols]).astype(BF16)

    hb = tm // HALO
    return _call(
        "mixer_fwd", (s // tm,), body, [proj, proj, proj, proj, wt, bst, gv, wpool, scale],
        [pl.BlockSpec((tm, A_W), lambda i: (i, 0)),
         pl.BlockSpec((tm, A_W), lambda i: (i, 1)),
         pl.BlockSpec((tm, A_W), lambda i: (i, 2)),
         pl.BlockSpec((HALO, A_W), lambda i: (jnp.maximum(i * hb - 1, 0), 2)),
         _full((HEADS, HD, HD)), _full((HEADS, HD, HD)), _full((1, A_W)), _full((4, GD, GD)), _full((1, A_W))],
        [jax.ShapeDtypeStruct((s, D), BF16)], [pl.BlockSpec((tm, D), lambda i: (i, 0))], vmem_mb=40, carry=carry)


def _mixer_bwd(proj, dmix, wt, wtt, bst, gv, wpool, scale, tm, carry):
    s = proj.shape[0]
    nb = s // tm
    nq = tm // HD
    hb = tm // HALO

    def body(ins, outs, scratch, ids):
        (up_ref, vp_ref, z_ref, zh_ref, doa_ref, dob_ref, dobh_ref, wt_ref, wtt_ref, bst_ref, gv_ref, wp_ref,
         sc_ref) = ins
        dproj_ref, dws_ref, dbs_ref, dgv_ref, dwp_ref, dsc_ref = outs
        dbfull = scratch[0]
        i = ids[0]
        last = i == nb - 1
        _zero_when(i == 0, dws_ref, dbfull, dgv_ref, dwp_ref, dsc_ref)

        for h in range(HEADS):
            cols = slice(HD * h, HD * (h + 1))
            gvh = gv_ref[:, cols]
            bcol = bst_ref[h]
            wth = wt_ref[h]
            wtth = wtt_ref[h]

            def chunk(q, c_):
                rows = pl.ds(pl.multiple_of(q * HD, HD), HD)
                u, du_dup = _gelu_and_grad(up_ref[rows, cols])
                v, dv_dvp = _gelu_and_grad(vp_ref[rows, cols])
                rv = _row_rsqrt(v)
                vn = v * rv
                vh = (vn * gvh).astype(BF16)
                mixed = jnp.dot(wth, vh, preferred_element_type=F32) + bcol
                doa = doa_ref[rows, cols].astype(F32)
                dmx = doa * u
                dmxb = dmx.astype(BF16)
                dbfull[h] += dmx
                dws_ref[h] += lax.dot_general(dmxb, vh, (NT, ((), ())), preferred_element_type=F32)
                dvh = jnp.dot(wtth, dmxb, preferred_element_type=F32)
                dgv_ref[:, cols] += jnp.sum(dvh * vn, axis=0, keepdims=True)
                dvn = dvh * gvh
                dv = rv * (dvn - vn * jnp.mean(dvn * vn, axis=-1, keepdims=True))
                dproj_ref[rows, cols] = (doa * mixed * du_dup).astype(BF16)
                dproj_ref[rows, A_W + HD * h:A_W + HD * (h + 1)] = (dv * dv_dvp).astype(BF16)
                return c_

            lax.fori_loop(0, nq, chunk, 0)

        t = i * tm + lax.broadcasted_iota(jnp.int32, (tm, 1), 0)
        th = (i + 1) * tm + lax.broadcasted_iota(jnp.int32, (HALO, 1), 0)
        for g, w in enumerate(POOL_WINDOWS):
            cols = slice(GD * g, GD * (g + 1))
            wpg = wp_ref[g]
            scg = sc_ref[:, cols]
            pb = _pooled(z_ref, zh_ref, g, w, i, tm).astype(BF16)
            ypre = jnp.dot(pb, wpg, preferred_element_type=F32)
            dob = dob_ref[:, cols].astype(F32)
            dsc_ref[:, cols] += jnp.sum(dob * ypre, axis=0, keepdims=True)
            dyb = (dob * scg).astype(BF16)
            dwp_ref[g] += lax.dot_general(pb, dyb, (TN, ((), ())), preferred_element_type=F32)
            dpo = lax.dot_general(dyb, wpg, (NT, ((), ())), preferred_element_type=F32)
            dyh = (dobh_ref[:, cols].astype(F32) * scg).astype(BF16)
            dpoh = lax.dot_general(dyh, wpg, (NT, ((), ())), preferred_element_type=F32)
            dpoh = jnp.where(last, 0.0, dpoh * _inv_count(th, w))
            ext = jnp.concatenate([dpo * _inv_count(t, w), dpoh], axis=0)
            dz = _window_sum(ext, w, False)[:tm, :] - dpo
            dproj_ref[:, 2 * A_W + GD * g:2 * A_W + GD * (g + 1)] = dz.astype(BF16)

        @pl.when(last)
        def _():
            r = lax.broadcasted_iota(jnp.int32, (HD, HD), 0)
            c = lax.broadcasted_iota(jnp.int32, (HD, HD), 1)
            for h in range(HEADS):
                dws_ref[h] = jnp.where(r >= c, dws_ref[h], 0.0)
                dbs_ref[h] = jnp.sum(dbfull[h], axis=-1, keepdims=True)

    return _call(
        "mixer_bwd", (nb,), body, [proj, proj, proj, proj, dmix, dmix, dmix, wt, wtt, bst, gv, wpool, scale],
        [pl.BlockSpec((tm, A_W), lambda i: (i, 0)),
         pl.BlockSpec((tm, A_W), lambda i: (i, 1)),
         pl.BlockSpec((tm, A_W), lambda i: (i, 2)),
         pl.BlockSpec((HALO, A_W), lambda i: (jnp.maximum(i * hb - 1, 0), 2)),
         pl.BlockSpec((tm, A_W), lambda i: (i, 0)),
         pl.BlockSpec((tm, A_W), lambda i: (i, 1)),
         pl.BlockSpec((HALO, A_W), lambda i: (jnp.minimum((i + 1) * hb, s // HALO - 1), 1)),
         _full((HEADS, HD, HD)), _full((HEADS, HD, HD)), _full((HEADS, HD, HD)), _full((1, A_W)),
         _full((4, GD, GD)), _full((1, A_W))],
        [jax.ShapeDtypeStruct((s, IN_W), BF16),
         jax.ShapeDtypeStruct((HEADS, HD, HD), F32),
         jax.ShapeDtypeStruct((HEADS, HD, 1), F32),
         jax.ShapeDtypeStruct((1, A_W), F32),
         jax.ShapeDtypeStruct((4, GD, GD), F32),
         jax.ShapeDtypeStruct((1, A_W), F32)],
        [pl.BlockSpec((tm, IN_W), lambda i: (i, 0)),
         _full((HEADS, HD, HD)), _full((HEADS, HD, 1)), _full((1, A_W)), _full((4, GD, GD)), _full((1, A_W))],
        [pltpu.VMEM((HEADS, HD, HD), F32)], vmem_mb=48, carry=carry)


def _inproj(h1, win_g, tm, carry):
    s = h1.shape[0]
    cw = IN_W // N_CHIPS

    def epi(p, cols, ids, extra, outs):
        outs[0][:, cols] = p

    return _matmul(
        "inproj", (N_CHIPS, s // tm), None, [h1, win_g],
        [pl.BlockSpec((tm, D), lambda j, i: (i, 0)), pl.BlockSpec((None, D, cw), lambda j, i: (j, 0, 0))],
        [jax.ShapeDtypeStruct((s, IN_W), F32)], [pl.BlockSpec((tm, cw), lambda j, i: (i, j))], NN, epi,
        nc=256, vmem_mb=32, carry=carry)


def _outproj(mixed, wout, x, g_ffn, tm, carry):
    s = x.shape[0]

    def epi(p, cols, ids, extra, outs):
        outs[0][:, cols] = extra[0][:, cols] + p

    def tail(ids, extra, outs):
        x2 = outs[0][...]
        outs[1][...] = (x2 * _row_rsqrt(x2) * extra[1][...]).astype(BF16)

    row = pl.BlockSpec((tm, D), lambda i: (i, 0))
    return _matmul(
        "outproj", (s // tm,), None, [mixed, wout, x, g_ffn], [row, _full((D, D)), row, _full((1, D))],
        [jax.ShapeDtypeStruct((s, D), F32), jax.ShapeDtypeStruct((s, D), BF16)], [row, row], NN, epi, tail,
        vmem_mb=48, carry=carry)


def _up(h2, wup_g, tm, tn, carry):
    s = h2.shape[0]
    per = D // tn

    def epi(p, cols, ids, extra, outs):
        a = jnp.maximum(p, 0.0)
        outs[0][:, cols] = (a * a).astype(BF16)

    return _matmul(
        "up", (D_FF // tn, s // tm), None, [h2, wup_g],
        [pl.BlockSpec((tm, D), lambda j, i: (i, 0)), pl.BlockSpec((None, D, tn), lambda j, i: (j // per, 0, j % per))],
        [jax.ShapeDtypeStruct((s, D_FF), BF16)], [pl.BlockSpec((tm, tn), lambda j, i: (i, j))], NN, epi,
        vmem_mb=40, carry=carry)


def _down(act, wdown, x2, tgt, g_final, tm, tk):
    s = x2.shape[0]

    def epi(acc, ids, extra, outs):
        x2_ref, t_ref, g_ref = extra
        dx_ref, dxb_ref, dgf_ref, loss_ref = outs
        g = g_ref[...]
        _zero_when(ids[0] == 0, dgf_ref, loss_ref)

        def block(rows):
            x3 = x2_ref[rows, :] + acc[rows, :]
            r = _row_rsqrt(x3)
            xh = x3 * r
            diff = xh * g - t_ref[rows, :]
            dy = diff * (1.0 / D)
            dxh = dy * g
            dx = r * (dxh - xh * jnp.mean(dxh * xh, axis=-1, keepdims=True))
            dx_ref[rows, :] = dx
            dxb_ref[rows, :] = dx.astype(BF16)
            dgf_ref[...] += jnp.sum(dy * xh, axis=0, keepdims=True)
            loss_ref[...] += jnp.sum(diff * diff, axis=0, keepdims=True)

        _for_rows(tm, block)

    row = pl.BlockSpec((tm, D), lambda i, k: (i, 0))
    vec = pl.BlockSpec((1, D), lambda i, k: (0, 0))
    return _matmul(
        "down", (s // tm, D_FF // tk), 1, [act, wdown, x2, tgt, g_final],
        [pl.BlockSpec((tm, tk), lambda i, k: (i, k)), pl.BlockSpec((tk, D), lambda i, k: (k, 0)), row, row, vec],
        [jax.ShapeDtypeStruct((s, D), F32), jax.ShapeDtypeStruct((s, D), BF16),
         jax.ShapeDtypeStruct((1, D), F32), jax.ShapeDtypeStruct((1, D), F32)],
        [row, row, vec, vec], NN, epi, acc_shape=(tm, D), vmem_mb=58)


def _dact(dx3b, wdown, act, tm, tn, carry):
    s = dx3b.shape[0]

    def epi(p, cols, ids, extra, outs):
        outs[0][:, cols] = (p * (2.0 * jnp.sqrt(extra[0][:, cols].astype(F32)))).astype(BF16)

    tile = pl.BlockSpec((tm, tn), lambda j, i: (i, j))
    return _matmul(
        "dact", (D_FF // tn, s // tm), None, [dx3b, wdown, act],
        [pl.BlockSpec((tm, D), lambda j, i: (i, 0)), pl.BlockSpec((tn, D), lambda j, i: (j, 0)), tile],
        [jax.ShapeDtypeStruct((s, D_FF), BF16)], [tile], NT, epi, vmem_mb=40, carry=carry)


def _dweight(name, lhs, rhs, n_shards, rows, cols, tm, nc=512, carry=None):
    s = lhs.shape[0]

    def epi(p, cs, ids, extra, outs):
        outs[0][:, cs] = p.astype(BF16)

    return _matmul(
        name, (n_shards, rows // tm), None, [lhs, rhs],
        [pl.BlockSpec((s, tm), lambda j, i: (0, i)), pl.BlockSpec((s, cols), lambda j, i: (0, j))],
        [jax.ShapeDtypeStruct((n_shards, rows, cols), BF16)],
        [pl.BlockSpec((None, tm, cols), lambda j, i: (j, i, 0))], TN, epi, nc=nc, vmem_mb=56, carry=carry)


def _dh2(da, wup_g, x2, dx3, g_ffn, tm, tk, carry):
    s = x2.shape[0]
    per = D // tk

    def epi(acc, ids, extra, outs):
        x2_ref, dx3_ref, g_ref = extra
        g = g_ref[...]
        _zero_when(ids[0] == 0, outs[2])

        def block(rows):
            dx, dg = _norm_bwd(acc[rows, :], x2_ref[rows, :], g, dx3_ref[rows, :])
            outs[0][rows, :] = dx
            outs[1][rows, :] = dx.astype(BF16)
            outs[2][...] += dg

        _for_rows(tm, block)

    row = pl.BlockSpec((tm, D), lambda i, k: (i, 0))
    vec = pl.BlockSpec((1, D), lambda i, k: (0, 0))
    return _matmul(
        "dh2", (s // tm, D_FF // tk), 1, [da, wup_g, x2, dx3, g_ffn],
        [pl.BlockSpec((tm, tk), lambda i, k: (i, k)),
         pl.BlockSpec((None, D, tk), lambda i, k: (k // per, 0, k % per)), row, row, vec],
        [jax.ShapeDtypeStruct((s, D), F32), jax.ShapeDtypeStruct((s, D), BF16), jax.ShapeDtypeStruct((1, D), F32)],
        [row, row, vec], NT, epi, acc_shape=(tm, D), vmem_mb=58, carry=carry)


def _dmixed(dx2b, wout, tm, carry):
    s = dx2b.shape[0]

    def epi(p, cols, ids, extra, outs):
        outs[0][:, cols] = p.astype(BF16)

    row = pl.BlockSpec((tm, D), lambda i: (i, 0))
    return _matmul(
        "dmixed", (s // tm,), None, [dx2b, wout], [row, _full((D, D))],
        [jax.ShapeDtypeStruct((s, D), BF16)], [row], NT, epi, vmem_mb=40, carry=carry)


def _dh1(dproj, win_g, x, dx2, g_mix, tm, carry):
    s = x.shape[0]
    cw = IN_W // N_CHIPS

    def epi(acc, ids, extra, outs):
        x_ref, dx2_ref, g_ref = extra
        g = g_ref[...]
        _zero_when(ids[0] == 0, outs[1])

        def block(rows):
            dx, dg = _norm_bwd(acc[rows, :], x_ref[rows, :], g, dx2_ref[rows, :])
            outs[0][rows, :] = dx
            outs[1][...] += dg

        _for_rows(tm, block)

    row = pl.BlockSpec((tm, D), lambda i, j: (i, 0))
    vec = pl.BlockSpec((1, D), lambda i, j: (0, 0))
    return _matmul(
        "dh1", (s // tm, N_CHIPS), 1, [dproj, win_g, x, dx2, g_mix],
        [pl.BlockSpec((tm, cw), lambda i, j: (i, j)), pl.BlockSpec((None, D, cw), lambda i, j: (j, 0, 0)),
         row, row, vec],
        [jax.ShapeDtypeStruct((s, D), F32), jax.ShapeDtypeStruct((1, D), F32)],
        [row, vec], NT, epi, acc_shape=(tm, D), vmem_mb=52, carry=carry)


def _cast_place(name, place, w, tr):
    rows, cols = w.shape

    def body(pref, w_ref, o_ref):
        o_ref[...] = w_ref[...].astype(BF16)

    return pl.pallas_call(
        body, name=name,
        grid_spec=pltpu.PrefetchScalarGridSpec(
            num_scalar_prefetch=1, grid=(rows // tr,),
            in_specs=[pl.BlockSpec((tr, cols), lambda r, p: (r, 0))],
            out_specs=pl.BlockSpec((None, tr, cols), lambda r, p: (p[0], r, 0))),
        out_shape=jax.ShapeDtypeStruct((N_CHIPS, rows, cols), BF16),
        compiler_params=pltpu.CompilerParams(dimension_semantics=("arbitrary",), vmem_limit_bytes=32 * MIB),
    )(place, w)


def _small_allreduce(part):
    rows = part.shape[0]
    rh = rows // 2

    def body(p_ref, o_ref, sib_ref, slots, send_sems, recv_sems):
        x, y, c = _place()
        k = 2 * x + y
        sib = (x, y, 1 - c)
        half = pl.ds(pl.multiple_of(c * rh, 8), rh)
        cp = _remote(p_ref, sib_ref, send_sems.at[0], recv_sems.at[0], sib)
        cp.start()
        cp.wait()
        slots[k] = p_ref[half, :] + sib_ref[half, :]
        cps = []
        for j, (cx, cy) in enumerate(_other_chips(x, y)):
            cp = _remote(slots.at[k], slots.at[k], send_sems.at[1 + j], recv_sems.at[1 + j], (cx, cy, c))
            cp.start()
            cps.append(cp)
        for cp in cps:
            cp.wait()
        o_ref[half, :] = ((slots[0] + slots[1]) + slots[2]) + slots[3]
        cp = _remote(o_ref.at[half], o_ref.at[half], send_sems.at[4], recv_sems.at[4], sib)
        cp.start()
        cp.wait()

    vm = pl.BlockSpec(memory_space=pltpu.VMEM)
    return pl.pallas_call(
        body, name="small_allreduce", in_specs=[vm], out_specs=vm,
        out_shape=jax.ShapeDtypeStruct(part.shape, F32),
        scratch_shapes=[pltpu.VMEM(part.shape, F32), pltpu.VMEM((N_CHIPS, rh, LANES), F32),
                        pltpu.SemaphoreType.DMA((5,)), pltpu.SemaphoreType.DMA((5,))],
        compiler_params=pltpu.CompilerParams(vmem_limit_bytes=32 * MIB),
    )(part)


def _comm_only(name, carry):
    ns, nl = len(carry.srcs), len(carry.lands)
    lands_in = [l for l in carry.lands if not isinstance(l, jax.ShapeDtypeStruct)]
    assert len(lands_in) in (0, nl)

    def body(*refs):
        srcs = refs[:ns]
        lands = refs[ns + len(lands_in):ns + len(lands_in) + nl]
        ssem, rsem = refs[ns + len(lands_in) + nl:]
        carry.start(srcs, lands, ssem, rsem)
        if carry.middle is not None:
            carry.middle(srcs, lands, ssem, rsem)
        carry.finish(srcs, lands, ssem, rsem)

    return pl.pallas_call(
        body, name=name, in_specs=[ANY] * (ns + len(lands_in)), out_specs=[ANY] * nl,
        out_shape=[jax.ShapeDtypeStruct(l.shape, l.dtype) for l in carry.lands],
        input_output_aliases={ns + i: i for i in range(len(lands_in))},
        scratch_shapes=[pltpu.SemaphoreType.DMA((carry.n_sems,)), pltpu.SemaphoreType.DMA((carry.n_sems,))],
    )(*carry.srcs, *lands_in)


def _share_carry(grads):
    n = len(grads)

    def copies(srcs, lands, ssem, rsem):
        x, y, c = _place()
        return [_remote(lands[w].at[c], lands[w].at[c], ssem.at[w], rsem.at[w], (x, y, 1 - c)) for w in range(n)]

    def start(*a):
        for cp in copies(*a):
            cp.start()

    def finish(*a):
        for cp in copies(*a):
            cp.wait()

    return _Carry([], grads, n, start, finish)


def _add_sibling(place, dw, got, tr):
    _, _, rh, cols = dw.shape

    def body(pref, a_ref, b_ref, o_ref):
        o_ref[...] = (a_ref[...].astype(F32) + b_ref[...].astype(F32)).astype(BF16)

    return pl.pallas_call(
        body, name="add_sibling",
        grid_spec=pltpu.PrefetchScalarGridSpec(
            num_scalar_prefetch=1, grid=(N_CHIPS, rh // tr),
            in_specs=[pl.BlockSpec((None, None, tr, cols), lambda j, r, p: (j, p[1], r, 0)),
                      pl.BlockSpec((None, tr, cols), lambda j, r, p: (j, r, 0))],
            out_specs=pl.BlockSpec((None, tr, cols), lambda j, r, p: (j, r, 0))),
        out_shape=jax.ShapeDtypeStruct((N_CHIPS, rh, cols), BF16),
        compiler_params=pltpu.CompilerParams(dimension_semantics=("arbitrary",) * 2, vmem_limit_bytes=32 * MIB),
    )(place, dw, got)


def _add_chips(place, sums, got, tr):
    _, rh, cols = sums.shape

    def body(pref, a_ref, b_ref, o_ref):
        b = b_ref[...].astype(F32)
        o_ref[...] = ((a_ref[...].astype(F32) + b[0]) + b[1]) + b[2]

    return pl.pallas_call(
        body, name="add_chips",
        grid_spec=pltpu.PrefetchScalarGridSpec(
            num_scalar_prefetch=1, grid=(rh // tr,),
            in_specs=[pl.BlockSpec((None, tr, cols), lambda r, p: (p[0], r, 0)),
                      pl.BlockSpec((3, tr, cols), lambda r, p: (0, r, 0))],
            out_specs=pl.BlockSpec((None, tr, cols), lambda r, p: (p[1], r, 0))),
        out_shape=jax.ShapeDtypeStruct((2, rh, cols), F32),
        compiler_params=pltpu.CompilerParams(dimension_semantics=("arbitrary",), vmem_limit_bytes=32 * MIB),
    )(place, sums, got)


def _adamw_math(w, g, m, v):
    m = ADAM_B1 * m + (1.0 - ADAM_B1) * g
    v = ADAM_B2 * v + (1.0 - ADAM_B2) * (g * g)
    m_hat = m / (1.0 - ADAM_B1 ** ADAM_STEP)
    v_hat = v / (1.0 - ADAM_B2 ** ADAM_STEP)
    delta = -ADAM_LR * (m_hat / (jnp.sqrt(v_hat) + ADAM_EPS) + ADAM_WD * w)
    return delta, m, v


def _adamw(name, w, g, m, v, tr, carry=None):
    rows, cols = w.shape

    def body(ins, outs, scratch, ids):
        g_val = ins[1][...]
        outs[0][...] = g_val
        outs[1][...], outs[2][...], outs[3][...] = _adamw_math(ins[0][...], g_val, ins[2][...], ins[3][...])

    blk = pl.BlockSpec((tr, cols), lambda r: (r, 0))
    return _call(name, (rows // tr,), body, [w, g, m, v], [blk] * 4,
                 [jax.ShapeDtypeStruct(w.shape, F32)] * 4, [blk] * 4, vmem_mb=40, carry=carry)


def _rows(a):
    return a.reshape(-1, LANES)


def kernel(x, g_mix, w_in, g_v, w_s, b_s, w_pool, pool_scale, w_out, g_ffn, w_up, w_down, g_final, loss_target, m_g_mix, m_w_in, m_g_v, m_w_s, m_b_s, m_w_pool, m_pool_scale, m_w_out, m_g_ffn, m_w_up, m_w_down, m_g_final, v_g_mix, v_w_in, v_g_v, v_w_s, v_b_s, v_w_pool, v_pool_scale, v_w_out, v_g_ffn, v_w_up, v_w_down, v_g_final):
    tm = 512
    xs = x[0]
    tgt = loss_target[0]
    cx, cy, cc = _place()
    chip = 2 * cx + cy
    place = jnp.stack([chip, cc]).astype(jnp.int32)
    everyone, near, far = (0, 1, 2), (0, 1), (2,)

    win_g = _cast_place("cast_w_in", place, w_in[0], 256)
    wpool_g = _cast_place("cast_w_pool", place, w_pool[0].reshape(4 * 64, GD), 128)
    wout_g = _cast_place("cast_w_out", place, w_out[0], 128)
    wup_g = _cast_place("cast_w_up", place, w_up[0], 256)
    wdown_g = _cast_place("cast_w_down", place, w_down[0], 256)
    win_g, wpool_g = _comm_only("allgather_first", _gather_whole([win_g, wpool_g]))
    wpool_f = wpool_g.reshape(N_CHIPS, 4, 64, GD).transpose(1, 0, 2, 3).reshape(4, GD, GD)
    tril = jnp.tril(jnp.ones((HD, HD), dtype=bool))
    wt = jnp.where(tril[None], w_s[0], 0.0).astype(BF16)
    wtt = wt.transpose(0, 2, 1)
    bst = jnp.broadcast_to(b_s[0][:, :, None], (HEADS, HD, HD))
    gfin = g_final.reshape(1, D)

    h1 = _norm1(xs, g_mix, tm)
    proj, wout_g = _inproj(h1, win_g, tm, _gather_whole([wout_g]))
    mixed, wup_g = _mixer_fwd(proj, wt, bst, g_v, wpool_f, pool_scale, tm, _gather_near([wup_g]))
    wout_f = wout_g.reshape(D, D)
    x2, h2, wup_g = _outproj(mixed, wout_f, xs, g_ffn, 256, _gather_far([wup_g]))
    act, wdown_g = _up(h2, wup_g, tm, 1024, _gather_whole([wdown_g]))
    wdown_f = wdown_g.reshape(D_FF, D)
    dx3, dx3b, dgf, lossv = _down(act, wdown_f, x2, tgt, gfin, tm, 1024)

    halves = lambda dw, rows, cols: dw.reshape(N_CHIPS, 2, rows // (2 * N_CHIPS), cols)
    cw = IN_W // N_CHIPS
    dwdown = halves(_dweight("dw_down", act, dx3b, 1, D_FF, D, 512)[0], D_FF, D)
    da, sib_down = _dact(dx3b, wdown_f, act, tm, 1024, _sibling_carry(dwdown))
    sum_down = _add_sibling(place, dwdown, sib_down, 256)
    dx2, dx2b, dgffn, got_down = _dh2(da, wup_g, x2, dx3, g_ffn, tm, 1024, _chips_carry(sum_down, None, everyone))
    half_down = _add_chips(place, sum_down, got_down, 256)
    dwup = halves(_dweight("dw_up", h2, da, N_CHIPS, D, D, 512)[0], D_FF, D)
    dmix, sib_up, half_down = _dmixed(dx2b, wout_f, 256, _join(_sibling_carry(dwup), _share_carry([half_down])))
    sum_up = _add_sibling(place, dwup, sib_up, 256)
    g_down, d_down, nm_down, nv_down = _adamw("adamw_down", w_down[0], half_down.reshape(D, D), m_w_down[0],
                                              v_w_down[0], 128)
    dproj, dws, dbs, dgv, dwp, dsc, got_up = _mixer_bwd(proj, dmix, wt, wtt, bst, g_v, wpool_f, pool_scale, tm,
                                                        _chips_carry(sum_up, None, near))
    dwout = halves(_dweight("dw_out", mixed, dx2b, 1, D, D, 512)[0], D, D)
    dwin, got_up, sib_out = _dweight("dw_in", h1, dproj, N_CHIPS, D, cw, 512, nc=256,
                                     carry=_join(_chips_carry(sum_up, got_up, far), _sibling_carry(dwout)))
    dwin = halves(dwin, N_CHIPS * D, cw)
    sum_out = _add_sibling(place, dwout, sib_out, 256)
    half_up = _add_chips(place, sum_up, got_up, 256)
    sib_in = _comm_only("sibling_in", _sibling_carry(dwin))[0]
    sum_in = _add_sibling(place, dwin, sib_in, 256)
    grad_x, dgmix, got_out, got_in, half_up = _dh1(
        dproj, win_g, xs, dx2, g_mix, tm,
        _join(_chips_carry(sum_out, None, everyone), _chips_carry(sum_in, None, everyone), _share_carry([half_up])))
    g_up, d_up, nm_up, nv_up = _adamw("adamw_up", w_up[0], half_up.reshape(D, D), m_w_up[0], v_w_up[0], 128)
    half_out = _add_chips(place, sum_out, got_out, 256)
    half_in = _add_chips(place, sum_in, got_in, 256)
    half_out, half_in = _comm_only("share_halves", _share_carry([half_out, half_in]))
    g_out, d_out, nm_out, nv_out = _adamw("adamw_out", w_out[0], half_out.reshape(D // N_CHIPS, D), m_w_out[0],
                                          v_w_out[0], 128)
    g_in, d_in, nm_in, nv_in = _adamw("adamw_in", w_in[0], half_in.reshape(D, cw), m_w_in[0], v_w_in[0], 128)

    pieces = [dgmix, dgv, dws, dbs, dwp, dsc, dgffn, dgf, lossv, jnp.zeros((8 * LANES,), F32)]
    sizes = [p.size // LANES for p in pieces]
    tot = _small_allreduce(jnp.concatenate([_rows(p) for p in pieces], axis=0))
    offs = [sum(sizes[:i]) for i in range(len(sizes))]
    take = lambda i: tot[offs[i]:offs[i] + sizes[i]]
    s_gmix, s_gv, s_ws, s_bs, s_wp, s_sc, s_gffn, s_gf = [take(i) for i in range(8)]
    loss = (0.5 / D) * jnp.sum(take(8))
    s_wp_mine = lax.dynamic_slice_in_dim(s_wp.reshape(4, GD, GD), chip * 64, 64, axis=1)
    small_g = [s_gmix, s_gv, s_ws, s_bs, _rows(s_wp_mine), s_sc, s_gffn, s_gf]
    small_w = [g_mix, g_v, w_s, b_s, w_pool, pool_scale, g_ffn, g_final]
    small_m = [m_g_mix, m_g_v, m_w_s, m_b_s, m_w_pool, m_pool_scale, m_g_ffn, m_g_final]
    small_v = [v_g_mix, v_g_v, v_w_s, v_b_s, v_w_pool, v_pool_scale, v_g_ffn, v_g_final]
    cat = lambda parts: jnp.concatenate([_rows(p) for p in parts], axis=0)
    sg = cat(small_g)
    sg, sd, snm, snv = _adamw("adamw_small", cat(small_w), sg, cat(small_m), cat(small_v), sg.shape[0])
    ssz = [p.size // LANES for p in small_w]
    soff = [sum(ssz[:i]) for i in range(len(ssz))]
    split = lambda a: [a[soff[i]:soff[i] + ssz[i]].reshape(small_w[i].shape) for i in range(len(ssz))]
    gs, ds, nms, nvs = split(sg), split(sd), split(snm), split(snv)

    def ordered(small, w_in_, w_out_, w_up_, w_down_):
        return [small[0], w_in_[None], small[1], small[2], small[3], small[4], small[5], w_out_[None], small[6],
                w_up_[None], w_down_[None], small[7]]

    return (loss, grad_x[None],
            *ordered(gs, g_in, g_out, g_up, g_down),
            *ordered(ds, d_in, d_out, d_up, d_down),
            *ordered(nms, nm_in, nm_out, nm_up, nm_down),
            *ordered(nvs, nv_in, nv_out, nv_up, nv_down))
```

```python
import functools

import jax
import jax.numpy as jnp
from jax import lax
from jax.experimental import pallas as pl
from jax.experimental.pallas import tpu as pltpu

F32 = jnp.float32
BF16 = jnp.bfloat16
EPS = 1e-6
D = 2048
A_W = 1024
HEADS = 8
HD = 128
POOL_WINDOWS = (2, 4, 8, 16)
GD = 256
IN_W = 3072
D_FF = 8192
N_CHIPS = 4
HALO = 16
LANES = 128
MIB = 2 ** 20

ADAM_LR, ADAM_B1, ADAM_B2, ADAM_EPS, ADAM_WD, ADAM_STEP = 0.001, 0.9, 0.999, 1e-08, 0.01, 10

ANY = pl.BlockSpec(memory_space=pl.ANY)
MESH = pl.DeviceIdType.MESH

NN = ((1,), (0,))
NT = ((1,), (1,))
TN = ((0,), (0,))


def _place():
    return lax.axis_index("x"), lax.axis_index("y"), lax.axis_index("c")


def _my_chip():
    return 2 * lax.axis_index("x") + lax.axis_index("y")


def _my_core():
    return lax.axis_index("c")


def _other_chips(x, y):
    return [(1 - x, y), (x, 1 - y), (1 - x, 1 - y)]


def _remote(src, dst, send_sem, recv_sem, dev):
    return pltpu.make_async_remote_copy(src_ref=src, dst_ref=dst, send_sem=send_sem, recv_sem=recv_sem,
                                        device_id=dev, device_id_type=MESH)


class _SemView:
    def __init__(self, sems, base):
        self.sems, self.base = sems, base

    @property
    def at(self):
        return self

    def __getitem__(self, i):
        return self.sems.at[self.base + i]


class _Carry:
    def __init__(self, srcs, lands, n_sems, start, finish, middle=None):
        self.srcs, self.lands, self.n_sems, self.start, self.finish = list(srcs), list(lands), n_sems, start, finish
        self.middle = middle


def _join(*carries):
    def run(which):
        def go(srcs, lands, ssem, rsem):
            so = lo = qo = 0
            for c in carries:
                if getattr(c, which) is not None:
                    getattr(c, which)(srcs[so:so + len(c.srcs)], lands[lo:lo + len(c.lands)],
                                      _SemView(ssem, qo), _SemView(rsem, qo))
                so, lo, qo = so + len(c.srcs), lo + len(c.lands), qo + c.n_sems
        return go

    middle = run("middle") if any(c.middle is not None for c in carries) else None
    return _Carry([s for c in carries for s in c.srcs], [l for c in carries for l in c.lands],
                  sum(c.n_sems for c in carries), run("start"), run("finish"), middle)


GATHER_SEMS = 7


def _gather_copies(lands, ssem, rsem):
    x, y, c = _place()
    k, kx, ky, kd = 2 * x + y, 2 * (1 - x) + y, 2 * x + (1 - y), 2 * (1 - x) + (1 - y)
    to_x, to_y, sib = (1 - x, y, c), (x, 1 - y, c), (x, y, 1 - c)
    out = []
    for w, land in enumerate(lands):
        rh = land.shape[1] // 2
        rq = rh // 2
        half = pl.ds(pl.multiple_of(c * rh, 16), rh)
        quarters = [pl.ds(pl.multiple_of(c * rh + q * rq, 16), rq) for q in range(2)]

        def cp(i, piece, dev, w=w):
            return _remote(piece, piece, ssem.at[GATHER_SEMS * w + i], rsem.at[GATHER_SEMS * w + i], dev)

        out.append(dict(
            ax=cp(0, land.at[k, half], to_x), ay=cp(1, land.at[k, half], to_y),
            rx=cp(2, land.at[kx, quarters[0]], to_y), ry=cp(3, land.at[ky, quarters[1]], to_x),
            fx=cp(4, land.at[kx, half], sib), fy=cp(5, land.at[ky, half], sib), fd=cp(6, land.at[kd, half], sib)))
    return out


def _gather_whole(gathered):
    def start(srcs, lands, ssem, rsem):
        for d in _gather_copies(lands, ssem, rsem):
            d["ax"].start()
            d["ay"].start()

    def middle(srcs, lands, ssem, rsem):
        for d in _gather_copies(lands, ssem, rsem):
            d["ax"].wait_recv()
            d["rx"].start()
            d["fx"].start()
            d["ay"].wait_recv()
            d["ry"].start()
            d["fy"].start()

    def finish(srcs, lands, ssem, rsem):
        for d in _gather_copies(lands, ssem, rsem):
            d["rx"].wait_recv()
            d["ry"].wait_recv()
            d["fd"].start()
            for name in ("fx", "fy", "fd"):
                d[name].wait_recv()
            for cp in d.values():
                cp.wait_send()

    return _Carry([], gathered, GATHER_SEMS * len(gathered), start, finish, middle)


def _gather_near(gathered):
    def start(srcs, lands, ssem, rsem):
        for d in _gather_copies(lands, ssem, rsem):
            d["ax"].start()
            d["ay"].start()

    def finish(srcs, lands, ssem, rsem):
        for d in _gather_copies(lands, ssem, rsem):
            for name in ("ax", "ay"):
                d[name].wait_recv()
                d[name].wait_send()

    return _Carry([], gathered, GATHER_SEMS * len(gathered), start, finish)


def _gather_far(gathered):
    def start(srcs, lands, ssem, rsem):
        for d in _gather_copies(lands, ssem, rsem):
            for name in ("rx", "ry", "fx", "fy"):
                d[name].start()

    def finish(srcs, lands, ssem, rsem):
        for d in _gather_copies(lands, ssem, rsem):
            d["rx"].wait_recv()
            d["ry"].wait_recv()
            d["fd"].start()
            for name in ("fx", "fy", "fd"):
                d[name].wait_recv()
            for name in ("rx", "ry", "fx", "fy", "fd"):
                d[name].wait_send()

    return _Carry([], gathered, GATHER_SEMS * len(gathered), start, finish)


def _sibling_carry(dw):
    def copies(srcs, lands, ssem, rsem):
        x, y, c = _place()
        return [_remote(srcs[0].at[j, 1 - c], lands[0].at[j], ssem.at[j], rsem.at[j], (x, y, 1 - c))
                for j in range(N_CHIPS)]

    def start(*a):
        for cp in copies(*a):
            cp.start()

    def finish(*a):
        for cp in copies(*a):
            cp.wait()

    return _Carry([dw], [jax.ShapeDtypeStruct((N_CHIPS,) + dw.shape[2:], dw.dtype)], N_CHIPS, start, finish)


def _chips_carry(sums, land, peers):
    def copies(srcs, lands, ssem, rsem):
        x, y, c = _place()
        chips = _other_chips(x, y)
        out = []
        for pi, j in enumerate(peers):
            cx, cy = chips[j]
            out.append(_remote(srcs[0].at[2 * cx + cy], lands[0].at[j], ssem.at[pi], rsem.at[pi], (cx, cy, c)))
        return out

    def start(*a):
        for cp in copies(*a):
            cp.start()

    def finish(*a):
        for cp in copies(*a):
            cp.wait()

    if land is None:
        land = jax.ShapeDtypeStruct((3,) + sums.shape[1:], sums.dtype)
    return _Carry([sums], [land], len(peers), start, finish)


def _call(name, grid, body, ins, in_specs, out_shapes, out_specs, scratch=(), vmem_mb=48, carry=None):
    n_in, n_out, n_sc = len(ins), len(out_shapes), len(scratch)
    ins, in_specs = list(ins), list(in_specs)
    out_shapes, out_specs, scratch = list(out_shapes), list(out_specs), list(scratch)
    aliases = {}
    if carry is not None:
        ins += carry.srcs
        in_specs += [ANY] * len(carry.srcs)
        for land in carry.lands:
            if not isinstance(land, jax.ShapeDtypeStruct):
                aliases[len(ins)] = len(out_shapes)
                ins.append(land)
                in_specs.append(ANY)
                land = jax.ShapeDtypeStruct(land.shape, land.dtype)
            out_shapes.append(land)
            out_specs.append(ANY)
        scratch += [pltpu.SemaphoreType.DMA((carry.n_sems,)), pltpu.SemaphoreType.DMA((carry.n_sems,))]
    n_in_all, n_out_all = len(ins), len(out_shapes)

    def kbody(*refs):
        in_refs, out_refs, sc = refs[:n_in_all], refs[n_in_all:n_in_all + n_out_all], refs[n_in_all + n_out_all:]
        ids = tuple(pl.program_id(a) for a in range(len(grid)))
        if carry is not None:
            first = functools.reduce(jnp.logical_and, [i == 0 for i in ids])
            last = functools.reduce(jnp.logical_and, [i == g - 1 for i, g in zip(ids, grid)])
            comm = (in_refs[n_in:n_in + len(carry.srcs)], out_refs[n_out:], sc[n_sc], sc[n_sc + 1])

            @pl.when(first)
            def _():
                carry.start(*comm)

            if carry.middle is not None:
                step, total = ids[0], grid[0]
                for i, g in zip(ids[1:], grid[1:]):
                    step, total = step * g + i, total * g

                @pl.when(step == total // 2)
                def _():
                    carry.middle(*comm)

        body(in_refs[:n_in], out_refs[:n_out], sc[:n_sc], ids)
        if carry is not None:
            @pl.when(last)
            def _():
                carry.finish(*comm)

    return pl.pallas_call(
        kbody, name=name, grid=grid, in_specs=in_specs, out_specs=out_specs, out_shape=out_shapes,
        scratch_shapes=scratch, input_output_aliases=aliases,
        compiler_params=pltpu.CompilerParams(dimension_semantics=("arbitrary",) * len(grid),
                                             vmem_limit_bytes=vmem_mb * MIB),
    )(*ins)


def _matmul(name, grid, kaxis, ins, in_specs, out_shapes, out_specs, dims, epi, tail=None, acc_shape=None, nc=512,
            vmem_mb=48, carry=None):
    nk = grid[kaxis] if kaxis is not None else 1

    def body(in_refs, out_refs, scratch, ids):
        a = in_refs[0][...]
        b_ref = in_refs[1]
        n = b_ref.shape[0] if dims == NT else b_ref.shape[1]

        def prod(c0):
            b = b_ref[c0:c0 + nc, :] if dims == NT else b_ref[:, c0:c0 + nc]
            return lax.dot_general(a, b, (dims, ((), ())), preferred_element_type=F32)

        if kaxis is None:
            for c0 in range(0, n, nc):
                epi(prod(c0), slice(c0, c0 + nc), ids, in_refs[2:], out_refs)
            if tail is not None:
                tail(ids, in_refs[2:], out_refs)
        else:
            acc = scratch[0]
            _zero_when(ids[kaxis] == 0, acc)
            for c0 in range(0, n, nc):
                acc[:, c0:c0 + nc] += prod(c0)

            @pl.when(ids[kaxis] == nk - 1)
            def _():
                epi(acc, ids, in_refs[2:], out_refs)

    return _call(name, grid, body, ins, in_specs, out_shapes, out_specs,
                 [pltpu.VMEM(acc_shape, F32)] if kaxis is not None else [], vmem_mb, carry)


def _row_rsqrt(xf):
    return lax.rsqrt(jnp.mean(xf * xf, axis=-1, keepdims=True) + EPS)


def _norm_bwd(dh, xf, g, resid):
    r = _row_rsqrt(xf)
    xh = xf * r
    dg = jnp.sum(dh * xh, axis=0, keepdims=True)
    dxh = dh * g
    dx = resid + r * (dxh - xh * jnp.mean(dxh * xh, axis=-1, keepdims=True))
    return dx, dg


def _for_rows(n, fn, sub=128):
    def step(q, carry):
        fn(pl.ds(pl.multiple_of(q * sub, sub), sub))
        return carry

    lax.fori_loop(0, n // sub, step, 0)


def _zero_when(first, *refs):
    @pl.when(first)
    def _():
        for ref in refs:
            ref[...] = jnp.zeros_like(ref)


_GELU_K = 0.7978845608028654
_GELU_C = 0.044715


def _gelu(x):
    t = jnp.tanh(_GELU_K * (x + _GELU_C * x * x * x))
    return 0.5 * x * (1.0 + t)


def _gelu_and_grad(x):
    x2 = x * x
    t = jnp.tanh(_GELU_K * (x + _GELU_C * x * x2))
    g = 0.5 * x * (1.0 + t)
    dg = 0.5 * (1.0 + t) + 0.5 * x * (1.0 - t * t) * (_GELU_K * (1.0 + 3.0 * _GELU_C * x2))
    return g, dg


def _window_sum(ext, w, causal):
    n = ext.shape[0]
    s, d = ext, 1
    while d < w:
        s = s + pltpu.roll(s, d if causal else n - d, 0)
        d *= 2
    return s


def _inv_count(t, w):
    return 1.0 / jnp.minimum(t + 1, w).astype(F32)


def _pooled(z_ref, zh_ref, g, w, i, tm):
    cols = slice(GD * g, GD * (g + 1))
    zb = z_ref[:, cols]
    zh = jnp.where(i > 0, zh_ref[:, cols], 0.0)
    ext = jnp.concatenate([zh, zb], axis=0)
    s = _window_sum(ext, w, True)[HALO:, :]
    t = i * tm + lax.broadcasted_iota(jnp.int32, (tm, 1), 0)
    return s * _inv_count(t, w) - zb


def _full(shape, n_axes=1):
    return pl.BlockSpec(shape, lambda *ids: (0,) * len(shape))


def _norm1(x, g, tm):
    s = x.shape[0]

    def body(ins, outs, scratch, ids):
        xf = ins[0][...]
        outs[0][...] = (xf * _row_rsqrt(xf) * ins[1][...]).astype(BF16)

    row = pl.BlockSpec((tm, D), lambda i: (i, 0))
    return _call("norm1", (s // tm,), body, [x, g], [row, _full((1, D))],
                 [jax.ShapeDtypeStruct((s, D), BF16)], [row], vmem_mb=32)[0]


def _mixer_fwd(proj, wt, bst, gv, wpool, scale, tm, carry):
    s = proj.shape[0]
    nq = tm // HD

    def body(ins, outs, scratch, ids):
        up_ref, vp_ref, z_ref, zh_ref, wt_ref, bst_ref, gv_ref, wp_ref, sc_ref = ins
        out_ref = outs[0]
        i = ids[0]
        for h in range(HEADS):
            cols = slice(HD * h, HD * (h + 1))
            gvh = gv_ref[:, cols]
            bcol = bst_ref[h]
            wth = wt_ref[h]

            def chunk(q, c_):
                rows = pl.ds(pl.multiple_of(q * HD, HD), HD)
                u = _gelu(up_ref[rows, cols])
                v = _gelu(vp_ref[rows, cols])
                vh = (v * _row_rsqrt(v) * gvh).astype(BF16)
                mixed = jnp.dot(wth, vh, preferred_element_type=F32) + bcol
                out_ref[rows, cols] = (u * mixed).astype(BF16)
                return c_

            lax.fori_loop(0, nq, chunk, 0)
        for g, w in enumerate(POOL_WINDOWS):
            cols = slice(GD * g, GD * (g + 1))
            pooled = _pooled(z_ref, zh_ref, g, w, i, tm)
            yv = jnp.dot(pooled.astype(BF16), wp_ref[g], preferred_element_type=F32)
            out_ref[:, A_W + GD * g:A_W + GD * (g + 1)] = (yv * sc_ref[:, cols]).astype(BF16)

    hb = tm // HALO
    return _call(
        "mixer_fwd", (s // tm,), body, [proj, proj, proj, proj, wt, bst, gv, wpool, scale],
        [pl.BlockSpec((tm, A_W), lambda i: (i, 0)),
         pl.BlockSpec((tm, A_W), lambda i: (i, 1)),
         pl.BlockSpec((tm, A_W), lambda i: (i, 2)),
         pl.BlockSpec((HALO, A_W), lambda i: (jnp.maximum(i * hb - 1, 0), 2)),
         _full((HEADS, HD, HD)), _full((HEADS, HD, HD)), _full((1, A_W)), _full((4, GD, GD)), _full((1, A_W))],
        [jax.ShapeDtypeStruct((s, D), BF16)], [pl.BlockSpec((tm, D), lambda i: (i, 0))], vmem_mb=40, carry=carry)


def _mixer_bwd(proj, dmix, wt, wtt, bst, gv, wpool, scale, tm, carry):
    s = proj.shape[0]
    nb = s // tm
    nq = tm // HD
    hb = tm // HALO

    def body(ins, outs, scratch, ids):
        (up_ref, vp_ref, z_ref, zh_ref, doa_ref, dob_ref, dobh_ref, wt_ref, wtt_ref, bst_ref, gv_ref, wp_ref,
         sc_ref) = ins
        dproj_ref, dws_ref, dbs_ref, dgv_ref, dwp_ref, dsc_ref = outs
        dbfull = scratch[0]
        i = ids[0]
        last = i == nb - 1
        _zero_when(i == 0, dws_ref, dbfull, dgv_ref, dwp_ref, dsc_ref)

        for h in range(HEADS):
            cols = slice(HD * h, HD * (h + 1))
            gvh = gv_ref[:, cols]
            bcol = bst_ref[h]
            wth = wt_ref[h]
            wtth = wtt_ref[h]

            def chunk(q, c_):
                rows = pl.ds(pl.multiple_of(q * HD, HD), HD)
                u, du_dup = _gelu_and_grad(up_ref[rows, cols])
                v, dv_dvp = _gelu_and_grad(vp_ref[rows, cols])
                rv = _row_rsqrt(v)
                vn = v * rv
                vh = (vn * gvh).astype(BF16)
                mixed = jnp.dot(wth, vh, preferred_element_type=F32) + bcol
                doa = doa_ref[rows, cols].astype(F32)
                dmx = doa * u
                dmxb = dmx.astype(BF16)
                dbfull[h] += dmx
                dws_ref[h] += lax.dot_general(dmxb, vh, (NT, ((), ())), preferred_element_type=F32)
                dvh = jnp.dot(wtth, dmxb, preferred_element_type=F32)
                dgv_ref[:, cols] += jnp.sum(dvh * vn, axis=0, keepdims=True)
                dvn = dvh * gvh
                dv = rv * (dvn - vn * jnp.mean(dvn * vn, axis=-1, keepdims=True))
                dproj_ref[rows, cols] = (doa * mixed * du_dup).astype(BF16)
                dproj_ref[rows, A_W + HD * h:A_W + HD * (h + 1)] = (dv * dv_dvp).astype(BF16)
                return c_

            lax.fori_loop(0, nq, chunk, 0)

        t = i * tm + lax.broadcasted_iota(jnp.int32, (tm, 1), 0)
        th = (i + 1) * tm + lax.broadcasted_iota(jnp.int32, (HALO, 1), 0)
        for g, w in enumerate(POOL_WINDOWS):
            cols = slice(GD * g, GD * (g + 1))
            wpg = wp_ref[g]
            scg = sc_ref[:, cols]
            pb = _pooled(z_ref, zh_ref, g, w, i, tm).astype(BF16)
            ypre = jnp.dot(pb, wpg, preferred_element_type=F32)
            dob = dob_ref[:, cols].astype(F32)
            dsc_ref[:, cols] += jnp.sum(dob * ypre, axis=0, keepdims=True)
            dyb = (dob * scg).astype(BF16)
            dwp_ref[g] += lax.dot_general(pb, dyb, (TN, ((), ())), preferred_element_type=F32)
            dpo = lax.dot_general(dyb, wpg, (NT, ((), ())), preferred_element_type=F32)
            dyh = (dobh_ref[:, cols].astype(F32) * scg).astype(BF16)
            dpoh = lax.dot_general(dyh, wpg, (NT, ((), ())), preferred_element_type=F32)
            dpoh = jnp.where(last, 0.0, dpoh * _inv_count(th, w))
            ext = jnp.concatenate([dpo * _inv_count(t, w), dpoh], axis=0)
            dz = _window_sum(ext, w, False)[:tm, :] - dpo
            dproj_ref[:, 2 * A_W + GD * g:2 * A_W + GD * (g + 1)] = dz.astype(BF16)

        @pl.when(last)
        def _():
            r = lax.broadcasted_iota(jnp.int32, (HD, HD), 0)
            c = lax.broadcasted_iota(jnp.int32, (HD, HD), 1)
            for h in range(HEADS):
                dws_ref[h] = jnp.where(r >= c, dws_ref[h], 0.0)
                dbs_ref[h] = jnp.sum(dbfull[h], axis=-1, keepdims=True)

    return _call(
        "mixer_bwd", (nb,), body, [proj, proj, proj, proj, dmix, dmix, dmix, wt, wtt, bst, gv, wpool, scale],
        [pl.BlockSpec((tm, A_W), lambda i: (i, 0)),
         pl.BlockSpec((tm, A_W), lambda i: (i, 1)),
         pl.BlockSpec((tm, A_W), lambda i: (i, 2)),
         pl.BlockSpec((HALO, A_W), lambda i: (jnp.maximum(i * hb - 1, 0), 2)),
         pl.BlockSpec((tm, A_W), lambda i: (i, 0)),
         pl.BlockSpec((tm, A_W), lambda i: (i, 1)),
         pl.BlockSpec((HALO, A_W), lambda i: (jnp.minimum((i + 1) * hb, s // HALO - 1), 1)),
         _full((HEADS, HD, HD)), _full((HEADS, HD, HD)), _full((HEADS, HD, HD)), _full((1, A_W)),
         _full((4, GD, GD)), _full((1, A_W))],
        [jax.ShapeDtypeStruct((s, IN_W), BF16),
         jax.ShapeDtypeStruct((HEADS, HD, HD), F32),
         jax.ShapeDtypeStruct((HEADS, HD, 1), F32),
         jax.ShapeDtypeStruct((1, A_W), F32),
         jax.ShapeDtypeStruct((4, GD, GD), F32),
         jax.ShapeDtypeStruct((1, A_W), F32)],
        [pl.BlockSpec((tm, IN_W), lambda i: (i, 0)),
         _full((HEADS, HD, HD)), _full((HEADS, HD, 1)), _full((1, A_W)), _full((4, GD, GD)), _full((1, A_W))],
        [pltpu.VMEM((HEADS, HD, HD), F32)], vmem_mb=48, carry=carry)


def _inproj(h1, win_g, tm, carry):
    s = h1.shape[0]
    cw = IN_W // N_CHIPS

    def epi(p, cols, ids, extra, outs):
        outs[0][:, cols] = p

    return _matmul(
        "inproj", (N_CHIPS, s // tm), None, [h1, win_g],
        [pl.BlockSpec((tm, D), lambda j, i: (i, 0)), pl.BlockSpec((None, D, cw), lambda j, i: (j, 0, 0))],
        [jax.ShapeDtypeStruct((s, IN_W), F32)], [pl.BlockSpec((tm, cw), lambda j, i: (i, j))], NN, epi,
        nc=256, vmem_mb=32, carry=carry)


def _outproj(mixed, wout, x, g_ffn, tm, carry):
    s = x.shape[0]

    def epi(p, cols, ids, extra, outs):
        outs[0][:, cols] = extra[0][:, cols] + p

    def tail(ids, extra, outs):
        x2 = outs[0][...]
        outs[1][...] = (x2 * _row_rsqrt(x2) * extra[1][...]).astype(BF16)

    row = pl.BlockSpec((tm, D), lambda i: (i, 0))
    return _matmul(
        "outproj", (s // tm,), None, [mixed, wout, x, g_ffn], [row, _full((D, D)), row, _full((1, D))],
        [jax.ShapeDtypeStruct((s, D), F32), jax.ShapeDtypeStruct((s, D), BF16)], [row, row], NN, epi, tail,
        vmem_mb=48, carry=carry)


def _up(h2, wup_g, tm, tn, carry):
    s = h2.shape[0]
    per = D // tn

    def epi(p, cols, ids, extra, outs):
        a = jnp.maximum(p, 0.0)
        outs[0][:, cols] = (a * a).astype(BF16)

    return _matmul(
        "up", (D_FF // tn, s // tm), None, [h2, wup_g],
        [pl.BlockSpec((tm, D), lambda j, i: (i, 0)), pl.BlockSpec((None, D, tn), lambda j, i: (j // per, 0, j % per))],
        [jax.ShapeDtypeStruct((s, D_FF), BF16)], [pl.BlockSpec((tm, tn), lambda j, i: (i, j))], NN, epi,
        vmem_mb=40, carry=carry)


def _down(act, wdown, x2, tgt, g_final, tm, tk):
    s = x2.shape[0]

    def epi(acc, ids, extra, outs):
        x2_ref, t_ref, g_ref = extra
        dxb_ref, dgf_ref, loss_ref = outs
        g = g_ref[...]
        _zero_when(ids[0] == 0, dgf_ref, loss_ref)

        def block(rows):
            x3 = x2_ref[rows, :] + acc[rows, :]
            r = _row_rsqrt(x3)
            xh = x3 * r
            diff = xh * g - t_ref[rows, :]
            dy = diff * (1.0 / D)
            dxh = dy * g
            dx = r * (dxh - xh * jnp.mean(dxh * xh, axis=-1, keepdims=True))
            dxb_ref[rows, :] = dx.astype(BF16)
            dgf_ref[...] += jnp.sum(dy * xh, axis=0, keepdims=True)
            loss_ref[...] += jnp.sum(diff * diff, axis=0, keepdims=True)

        _for_rows(tm, block)

    row = pl.BlockSpec((tm, D), lambda i, k: (i, 0))
    vec = pl.BlockSpec((1, D), lambda i, k: (0, 0))
    return _matmul(
        "down", (s // tm, D_FF // tk), 1, [act, wdown, x2, tgt, g_final],
        [pl.BlockSpec((tm, tk), lambda i, k: (i, k)), pl.BlockSpec((tk, D), lambda i, k: (k, 0)), row, row, vec],
        [jax.ShapeDtypeStruct((s, D), BF16), jax.ShapeDtypeStruct((1, D), F32), jax.ShapeDtypeStruct((1, D), F32)],
        [row, vec, vec], NN, epi, acc_shape=(tm, D), vmem_mb=56)


def _dact(dx3b, wdown, act, tm, tn, carry):
    s = dx3b.shape[0]

    def epi(p, cols, ids, extra, outs):
        outs[0][:, cols] = (p * (2.0 * jnp.sqrt(extra[0][:, cols].astype(F32)))).astype(BF16)

    tile = pl.BlockSpec((tm, tn), lambda j, i: (i, j))
    return _matmul(
        "dact", (D_FF // tn, s // tm), None, [dx3b, wdown, act],
        [pl.BlockSpec((tm, D), lambda j, i: (i, 0)), pl.BlockSpec((tn, D), lambda j, i: (j, 0)), tile],
        [jax.ShapeDtypeStruct((s, D_FF), BF16)], [tile], NT, epi, vmem_mb=40, carry=carry)


def _dweight(name, lhs, rhs, n_shards, rows, cols, tm, nc=512, carry=None):
    s = lhs.shape[0]

    def epi(p, cs, ids, extra, outs):
        outs[0][:, cs] = p.astype(BF16)

    return _matmul(
        name, (n_shards, rows // tm), None, [lhs, rhs],
        [pl.BlockSpec((s, tm), lambda j, i: (0, i)), pl.BlockSpec((s, cols), lambda j, i: (0, j))],
        [jax.ShapeDtypeStruct((n_shards, rows, cols), BF16)],
        [pl.BlockSpec((None, tm, cols), lambda j, i: (j, i, 0))], TN, epi, nc=nc, vmem_mb=56, carry=carry)


def _dh2(da, wup_g, x2, dx3, g_ffn, tm, tk, carry):
    s = x2.shape[0]
    per = D // tk

    def epi(acc, ids, extra, outs):
        x2_ref, dx3_ref, g_ref = extra
        g = g_ref[...]
        _zero_when(ids[0] == 0, outs[1])

        def block(rows):
            dx, dg = _norm_bwd(acc[rows, :], x2_ref[rows, :], g, dx3_ref[rows, :].astype(F32))
            outs[0][rows, :] = dx.astype(BF16)
            outs[1][...] += dg

        _for_rows(tm, block)

    row = pl.BlockSpec((tm, D), lambda i, k: (i, 0))
    vec = pl.BlockSpec((1, D), lambda i, k: (0, 0))
    return _matmul(
        "dh2", (s // tm, D_FF // tk), 1, [da, wup_g, x2, dx3, g_ffn],
        [pl.BlockSpec((tm, tk), lambda i, k: (i, k)),
         pl.BlockSpec((None, D, tk), lambda i, k: (k // per, 0, k % per)), row, row, vec],
        [jax.ShapeDtypeStruct((s, D), BF16), jax.ShapeDtypeStruct((1, D), F32)],
        [row, vec], NT, epi, acc_shape=(tm, D), vmem_mb=56, carry=carry)


def _dmixed(dx2b, wout, tm, carry):
    s = dx2b.shape[0]

    def epi(p, cols, ids, extra, outs):
        outs[0][:, cols] = p.astype(BF16)

    row = pl.BlockSpec((tm, D), lambda i: (i, 0))
    return _matmul(
        "dmixed", (s // tm,), None, [dx2b, wout], [row, _full((D, D))],
        [jax.ShapeDtypeStruct((s, D), BF16)], [row], NT, epi, vmem_mb=40, carry=carry)


def _dh1(dproj, win_g, x, dx2, g_mix, tm, carry):
    s = x.shape[0]
    cw = IN_W // N_CHIPS

    def epi(acc, ids, extra, outs):
        x_ref, dx2_ref, g_ref = extra
        g = g_ref[...]
        _zero_when(ids[0] == 0, outs[1])

        def block(rows):
            dx, dg = _norm_bwd(acc[rows, :], x_ref[rows, :], g, dx2_ref[rows, :].astype(F32))
            outs[0][rows, :] = dx
            outs[1][...] += dg

        _for_rows(tm, block)

    row = pl.BlockSpec((tm, D), lambda i, j: (i, 0))
    vec = pl.BlockSpec((1, D), lambda i, j: (0, 0))
    return _matmul(
        "dh1", (s // tm, N_CHIPS), 1, [dproj, win_g, x, dx2, g_mix],
        [pl.BlockSpec((tm, cw), lambda i, j: (i, j)), pl.BlockSpec((None, D, cw), lambda i, j: (j, 0, 0)),
         row, row, vec],
        [jax.ShapeDtypeStruct((s, D), F32), jax.ShapeDtypeStruct((1, D), F32)],
        [row, vec], NT, epi, acc_shape=(tm, D), vmem_mb=52, carry=carry)


def _cast_place(name, w, tr):
    rows, cols = w.shape

    def body(ins, outs, scratch, ids):
        outs[0][...] = ins[0][...].astype(BF16)

    return _call(name, (rows // tr,), body, [w], [pl.BlockSpec((tr, cols), lambda r: (r, 0))],
                 [jax.ShapeDtypeStruct((N_CHIPS, rows, cols), BF16)],
                 [pl.BlockSpec((None, tr, cols), lambda r: (_my_chip(), r, 0))], vmem_mb=32)[0]


def _small_allreduce(part):
    rows = part.shape[0]
    rh = rows // 2

    def body(p_ref, o_ref, sib_ref, slots, send_sems, recv_sems):
        x, y, c = _place()
        k = 2 * x + y
        sib = (x, y, 1 - c)
        half = pl.ds(pl.multiple_of(c * rh, 8), rh)
        cp = _remote(p_ref, sib_ref, send_sems.at[0], recv_sems.at[0], sib)
        cp.start()
        cp.wait()
        slots[k] = p_ref[half, :] + sib_ref[half, :]
        cps = []
        for j, (cx, cy) in enumerate(_other_chips(x, y)):
            cp = _remote(slots.at[k], slots.at[k], send_sems.at[1 + j], recv_sems.at[1 + j], (cx, cy, c))
            cp.start()
            cps.append(cp)
        for cp in cps:
            cp.wait()
        o_ref[half, :] = ((slots[0] + slots[1]) + slots[2]) + slots[3]
        cp = _remote(o_ref.at[half], o_ref.at[half], send_sems.at[4], recv_sems.at[4], sib)
        cp.start()
        cp.wait()

    vm = pl.BlockSpec(memory_space=pltpu.VMEM)
    return pl.pallas_call(
        body, name="small_allreduce", in_specs=[vm], out_specs=vm,
        out_shape=jax.ShapeDtypeStruct(part.shape, F32),
        scratch_shapes=[pltpu.VMEM(part.shape, F32), pltpu.VMEM((N_CHIPS, rh, LANES), F32),
                        pltpu.SemaphoreType.DMA((5,)), pltpu.SemaphoreType.DMA((5,))],
        compiler_params=pltpu.CompilerParams(vmem_limit_bytes=32 * MIB),
    )(part)


def _comm_only(name, carry):
    ns, nl = len(carry.srcs), len(carry.lands)
    lands_in = [l for l in carry.lands if not isinstance(l, jax.ShapeDtypeStruct)]
    assert len(lands_in) in (0, nl)

    def body(*refs):
        srcs = refs[:ns]
        lands = refs[ns + len(lands_in):ns + len(lands_in) + nl]
        ssem, rsem = refs[ns + len(lands_in) + nl:]
        carry.start(srcs, lands, ssem, rsem)
        if carry.middle is not None:
            carry.middle(srcs, lands, ssem, rsem)
        carry.finish(srcs, lands, ssem, rsem)

    return pl.pallas_call(
        body, name=name, in_specs=[ANY] * (ns + len(lands_in)), out_specs=[ANY] * nl,
        out_shape=[jax.ShapeDtypeStruct(l.shape, l.dtype) for l in carry.lands],
        input_output_aliases={ns + i: i for i in range(len(lands_in))},
        scratch_shapes=[pltpu.SemaphoreType.DMA((carry.n_sems,)), pltpu.SemaphoreType.DMA((carry.n_sems,))],
    )(*carry.srcs, *lands_in)


def _share_carry(grads):
    n = len(grads)

    def copies(srcs, lands, ssem, rsem):
        x, y, c = _place()
        return [_remote(lands[w].at[c], lands[w].at[c], ssem.at[w], rsem.at[w], (x, y, 1 - c)) for w in range(n)]

    def start(*a):
        for cp in copies(*a):
            cp.start()

    def finish(*a):
        for cp in copies(*a):
            cp.wait()

    return _Carry([], grads, n, start, finish)


def _add_sibling(dw, got, tr):
    _, _, rh, cols = dw.shape

    def body(ins, outs, scratch, ids):
        outs[0][...] = (ins[0][...].astype(F32) + ins[1][...].astype(F32)).astype(BF16)

    blk = pl.BlockSpec((None, tr, cols), lambda j, r: (j, r, 0))
    return _call("add_sibling", (N_CHIPS, rh // tr), body, [dw, got],
                 [pl.BlockSpec((None, None, tr, cols), lambda j, r: (j, _my_core(), r, 0)), blk],
                 [jax.ShapeDtypeStruct((N_CHIPS, rh, cols), BF16)], [blk], vmem_mb=32)[0]


def _add_chips(sums, got, tr):
    _, rh, cols = sums.shape

    def body(ins, outs, scratch, ids):
        b = ins[1][...].astype(F32)
        outs[0][...] = ((ins[0][...].astype(F32) + b[0]) + b[1]) + b[2]

    return _call("add_chips", (rh // tr,), body, [sums, got],
                 [pl.BlockSpec((None, tr, cols), lambda r: (_my_chip(), r, 0)),
                  pl.BlockSpec((3, tr, cols), lambda r: (0, r, 0))],
                 [jax.ShapeDtypeStruct((2, rh, cols), F32)],
                 [pl.BlockSpec((None, tr, cols), lambda r: (_my_core(), r, 0))], vmem_mb=32)[0]


def _adamw_math(w, g, m, v):
    m = ADAM_B1 * m + (1.0 - ADAM_B1) * g
    v = ADAM_B2 * v + (1.0 - ADAM_B2) * (g * g)
    m_hat = m / (1.0 - ADAM_B1 ** ADAM_STEP)
    v_hat = v / (1.0 - ADAM_B2 ** ADAM_STEP)
    delta = -ADAM_LR * (m_hat / (jnp.sqrt(v_hat) + ADAM_EPS) + ADAM_WD * w)
    return delta, m, v


def _adamw(name, w, g, m, v, tr, carry=None):
    rows, cols = w.shape

    def body(ins, outs, scratch, ids):
        g_val = ins[1][...]
        outs[0][...] = g_val
        outs[1][...], outs[2][...], outs[3][...] = _adamw_math(ins[0][...], g_val, ins[2][...], ins[3][...])

    blk = pl.BlockSpec((tr, cols), lambda r: (r, 0))
    return _call(name, (rows // tr,), body, [w, g, m, v], [blk] * 4,
                 [jax.ShapeDtypeStruct(w.shape, F32)] * 4, [blk] * 4, vmem_mb=40, carry=carry)


def _rows(a):
    return a.reshape(-1, LANES)


def kernel(x, g_mix, w_in, g_v, w_s, b_s, w_pool, pool_scale, w_out, g_ffn, w_up, w_down, g_final, loss_target, m_g_mix, m_w_in, m_g_v, m_w_s, m_b_s, m_w_pool, m_pool_scale, m_w_out, m_g_ffn, m_w_up, m_w_down, m_g_final, v_g_mix, v_w_in, v_g_v, v_w_s, v_b_s, v_w_pool, v_pool_scale, v_w_out, v_g_ffn, v_w_up, v_w_down, v_g_final):
    tm = 512
    xs = x[0]
    tgt = loss_target[0]
    chip = _my_chip()
    everyone, near, far = (0, 1, 2), (0, 1), (2,)

    win_g = _cast_place("cast_w_in", w_in[0], 256)
    wpool_g = _cast_place("cast_w_pool", w_pool[0].reshape(4 * 64, GD), 128)
    wout_g = _cast_place("cast_w_out", w_out[0], 128)
    wup_g = _cast_place("cast_w_up", w_up[0], 256)
    wdown_g = _cast_place("cast_w_down", w_down[0], 256)
    win_g, wpool_g = _comm_only("allgather_first", _gather_whole([win_g, wpool_g]))
    wpool_f = wpool_g.reshape(N_CHIPS, 4, 64, GD).transpose(1, 0, 2, 3).reshape(4, GD, GD)
    tril = jnp.tril(jnp.ones((HD, HD), dtype=bool))
    wt = jnp.where(tril[None], w_s[0], 0.0).astype(BF16)
    wtt = wt.transpose(0, 2, 1)
    bst = jnp.broadcast_to(b_s[0][:, :, None], (HEADS, HD, HD))
    gfin = g_final.reshape(1, D)

    h1 = _norm1(xs, g_mix, tm)
    proj, wout_g = _inproj(h1, win_g, tm, _gather_whole([wout_g]))
    mixed, wup_g = _mixer_fwd(proj, wt, bst, g_v, wpool_f, pool_scale, tm, _gather_near([wup_g]))
    wout_f = wout_g.reshape(D, D)
    x2, h2, wup_g = _outproj(mixed, wout_f, xs, g_ffn, 256, _gather_far([wup_g]))
    act, wdown_g = _up(h2, wup_g, tm, 2048, _gather_whole([wdown_g]))
    wdown_f = wdown_g.reshape(D_FF, D)
    dx3b, dgf, lossv = _down(act, wdown_f, x2, tgt, gfin, tm, 2048)

    halves = lambda dw, rows, cols: dw.reshape(N_CHIPS, 2, rows // (2 * N_CHIPS), cols)
    cw = IN_W // N_CHIPS
    dwdown = halves(_dweight("dw_down", act, dx3b, 1, D_FF, D, 512)[0], D_FF, D)
    da, sib_down = _dact(dx3b, wdown_f, act, tm, 2048, _sibling_carry(dwdown))
    sum_down = _add_sibling(dwdown, sib_down, 256)
    dx2b, dgffn, got_down = _dh2(da, wup_g, x2, dx3b, g_ffn, tm, 2048, _chips_carry(sum_down, None, everyone))
    half_down = _add_chips(sum_down, got_down, 256)
    dwup = halves(_dweight("dw_up", h2, da, N_CHIPS, D, D, 512)[0], D_FF, D)
    dmix, sib_up, half_down = _dmixed(dx2b, wout_f, 256, _join(_sibling_carry(dwup), _share_carry([half_down])))
    sum_up = _add_sibling(dwup, sib_up, 256)
    g_down, d_down, nm_down, nv_down = _adamw("adamw_down", w_down[0], half_down.reshape(D, D), m_w_down[0],
                                              v_w_down[0], 128)
    dproj, dws, dbs, dgv, dwp, dsc, got_up = _mixer_bwd(proj, dmix, wt, wtt, bst, g_v, wpool_f, pool_scale, tm,
                                                        _chips_carry(sum_up, None, near))
    dwout = halves(_dweight("dw_out", mixed, dx2b, 1, D, D, 512)[0], D, D)
    dwin, got_up, sib_out = _dweight("dw_in", h1, dproj, N_CHIPS, D, cw, 512, nc=256,
                                     carry=_join(_chips_carry(sum_up, got_up, far), _sibling_carry(dwout)))
    dwin = halves(dwin, N_CHIPS * D, cw)
    sum_out = _add_sibling(dwout, sib_out, 256)
    half_up = _add_chips(sum_up, got_up, 256)
    sib_in = _comm_only("sibling_in", _sibling_carry(dwin))[0]
    sum_in = _add_sibling(dwin, sib_in, 256)
    grad_x, dgmix, got_out, got_in, half_up = _dh1(
        dproj, win_g, xs, dx2b, g_mix, tm,
        _join(_chips_carry(sum_out, None, everyone), _chips_carry(sum_in, None, everyone), _share_carry([half_up])))
    g_up, d_up, nm_up, nv_up = _adamw("adamw_up", w_up[0], half_up.reshape(D, D), m_w_up[0], v_w_up[0], 128)
    half_out = _add_chips(sum_out, got_out, 256)
    half_in = _add_chips(sum_in, got_in, 256)
    half_out, half_in = _comm_only("share_halves", _share_carry([half_out, half_in]))
    g_out, d_out, nm_out, nv_out = _adamw("adamw_out", w_out[0], half_out.reshape(D // N_CHIPS, D), m_w_out[0],
                                          v_w_out[0], 128)
    g_in, d_in, nm_in, nv_in = _adamw("adamw_in", w_in[0], half_in.reshape(D, cw), m_w_in[0], v_w_in[0], 128)

    pieces = [dgmix, dgv, dws, dbs, dwp, dsc, dgffn, dgf, lossv, jnp.zeros((8 * LANES,), F32)]
    sizes = [p.size // LANES for p in pieces]
    tot = _small_allreduce(jnp.concatenate([_rows(p) for p in pieces], axis=0))
    offs = [sum(sizes[:i]) for i in range(len(sizes))]
    take = lambda i: tot[offs[i]:offs[i] + sizes[i]]
    s_gmix, s_gv, s_ws, s_bs, s_wp, s_sc, s_gffn, s_gf = [take(i) for i in range(8)]
    loss = (0.5 / D) * jnp.sum(take(8))
    s_wp_mine = lax.dynamic_slice_in_dim(s_wp.reshape(4, GD, GD), chip * 64, 64, axis=1)
    small_g = [s_gmix, s_gv, s_ws, s_bs, _rows(s_wp_mine), s_sc, s_gffn, s_gf]
    small_w = [g_mix, g_v, w_s, b_s, w_pool, pool_scale, g_ffn, g_final]
    small_m = [m_g_mix, m_g_v, m_w_s, m_b_s, m_w_pool, m_pool_scale, m_g_ffn, m_g_final]
    small_v = [v_g_mix, v_g_v, v_w_s, v_b_s, v_w_pool, v_pool_scale, v_g_ffn, v_g_final]
    cat = lambda parts: jnp.concatenate([_rows(p) for p in parts], axis=0)
    sg = cat(small_g)
    sg, sd, snm, snv = _adamw("adamw_small", cat(small_w), sg, cat(small_m), cat(small_v), sg.shape[0])
    ssz = [p.size // LANES for p in small_w]
    soff = [sum(ssz[:i]) for i in range(len(ssz))]
    split = lambda a: [a[soff[i]:soff[i] + ssz[i]].reshape(small_w[i].shape) for i in range(len(ssz))]
    gs, ds, nms, nvs = split(sg), split(sd), split(snm), split(snv)

    def ordered(small, w_in_, w_out_, w_up_, w_down_):
        return [small[0], w_in_[None], small[1], small[2], small[3], small[4], small[5], w_out_[None], small[6],
                w_up_[None], w_down_[None], small[7]]

    return (loss, grad_x[None],
            *ordered(gs, g_in, g_out, g_up, g_down),
            *ordered(ds, d_in, d_out, d_up, d_down),
            *ordered(nms, nm_in, nm_out, nm_up, nm_down),
            *ordered(nvs, nv_in, nv_out, nv_up, nv_down))
```

```python
import functools

import jax
import jax.numpy as jnp
from jax import lax
from jax.experimental import pallas as pl
from jax.experimental.pallas import tpu as pltpu

F32 = jnp.float32
BF16 = jnp.bfloat16
EPS = 1e-6
D = 2048
A_W = 1024
HEADS = 8
HD = 128
POOL_WINDOWS = (2, 4, 8, 16)
GD = 256
IN_W = 3072
D_FF = 8192
N_CHIPS = 4
HALO = 16
LANES = 128
MIB = 2 ** 20

ADAM_LR, ADAM_B1, ADAM_B2, ADAM_EPS, ADAM_WD, ADAM_STEP = 0.001, 0.9, 0.999, 1e-08, 0.01, 10

ANY = pl.BlockSpec(memory_space=pl.ANY)
MESH = pl.DeviceIdType.MESH

NN = ((1,), (0,))
NT = ((1,), (1,))
TN = ((0,), (0,))


def _place():
    return lax.axis_index("x"), lax.axis_index("y"), lax.axis_index("c")


def _my_chip():
    return 2 * lax.axis_index("x") + lax.axis_index("y")


def _my_core():
    return lax.axis_index("c")


def _other_chips(x, y):
    return [(1 - x, y), (x, 1 - y), (1 - x, 1 - y)]


def _remote(src, dst, send_sem, recv_sem, dev):
    return pltpu.make_async_remote_copy(src_ref=src, dst_ref=dst, send_sem=send_sem, recv_sem=recv_sem,
                                        device_id=dev, device_id_type=MESH)


class _SemView:
    def __init__(self, sems, base):
        self.sems, self.base = sems, base

    @property
    def at(self):
        return self

    def __getitem__(self, i):
        return self.sems.at[self.base + i]


class _Carry:
    def __init__(self, srcs, lands, n_sems, start, finish, middle=None):
        self.srcs, self.lands, self.n_sems, self.start, self.finish = list(srcs), list(lands), n_sems, start, finish
        self.middle = middle


def _join(*carries):
    def run(which):
        def go(srcs, lands, ssem, rsem):
            so = lo = qo = 0
            for c in carries:
                if getattr(c, which) is not None:
                    getattr(c, which)(srcs[so:so + len(c.srcs)], lands[lo:lo + len(c.lands)],
                                      _SemView(ssem, qo), _SemView(rsem, qo))
                so, lo, qo = so + len(c.srcs), lo + len(c.lands), qo + c.n_sems
        return go

    middle = run("middle") if any(c.middle is not None for c in carries) else None
    return _Carry([s for c in carries for s in c.srcs], [l for c in carries for l in c.lands],
                  sum(c.n_sems for c in carries), run("start"), run("finish"), middle)


GATHER_SEMS = 7


def _gather_copies(lands, ssem, rsem):
    x, y, c = _place()
    k, kx, ky, kd = 2 * x + y, 2 * (1 - x) + y, 2 * x + (1 - y), 2 * (1 - x) + (1 - y)
    to_x, to_y, sib = (1 - x, y, c), (x, 1 - y, c), (x, y, 1 - c)
    out = []
    for w, land in enumerate(lands):
        rh = land.shape[1] // 2
        rq = rh // 2
        half = pl.ds(pl.multiple_of(c * rh, 16), rh)
        quarters = [pl.ds(pl.multiple_of(c * rh + q * rq, 16), rq) for q in range(2)]

        def cp(i, piece, dev, w=w):
            return _remote(piece, piece, ssem.at[GATHER_SEMS * w + i], rsem.at[GATHER_SEMS * w + i], dev)

        out.append(dict(
            ax=cp(0, land.at[k, half], to_x), ay=cp(1, land.at[k, half], to_y),
            rx=cp(2, land.at[kx, quarters[0]], to_y), ry=cp(3, land.at[ky, quarters[1]], to_x),
            fx=cp(4, land.at[kx, half], sib), fy=cp(5, land.at[ky, half], sib), fd=cp(6, land.at[kd, half], sib)))
    return out


def _gather_whole(gathered):
    def start(srcs, lands, ssem, rsem):
        for d in _gather_copies(lands, ssem, rsem):
            d["ax"].start()
            d["ay"].start()

    def middle(srcs, lands, ssem, rsem):
        for d in _gather_copies(lands, ssem, rsem):
            d["ax"].wait_recv()
            d["rx"].start()
            d["fx"].start()
            d["ay"].wait_recv()
            d["ry"].start()
            d["fy"].start()

    def finish(srcs, lands, ssem, rsem):
        for d in _gather_copies(lands, ssem, rsem):
            d["rx"].wait_recv()
            d["ry"].wait_recv()
            d["fd"].start()
            for name in ("fx", "fy", "fd"):
                d[name].wait_recv()
            for cp in d.values():
                cp.wait_send()

    return _Carry([], gathered, GATHER_SEMS * len(gathered), start, finish, middle)


def _gather_near(gathered):
    def start(srcs, lands, ssem, rsem):
        for d in _gather_copies(lands, ssem, rsem):
            d["ax"].start()
            d["ay"].start()

    def finish(srcs, lands, ssem, rsem):
        for d in _gather_copies(lands, ssem, rsem):
            for name in ("ax", "ay"):
                d[name].wait_recv()
                d[name].wait_send()

    return _Carry([], gathered, GATHER_SEMS * len(gathered), start, finish)


def _gather_far(gathered):
    def start(srcs, lands, ssem, rsem):
        for d in _gather_copies(lands, ssem, rsem):
            for name in ("rx", "ry", "fx", "fy"):
                d[name].start()

    def finish(srcs, lands, ssem, rsem):
        for d in _gather_copies(lands, ssem, rsem):
            d["rx"].wait_recv()
            d["ry"].wait_recv()
            d["fd"].start()
            for name in ("fx", "fy", "fd"):
                d[name].wait_recv()
            for name in ("rx", "ry", "fx", "fy", "fd"):
                d[name].wait_send()

    return _Carry([], gathered, GATHER_SEMS * len(gathered), start, finish)


def _sibling_carry(dw):
    def copies(srcs, lands, ssem, rsem):
        x, y, c = _place()
        return [_remote(srcs[0].at[j, 1 - c], lands[0].at[j], ssem.at[j], rsem.at[j], (x, y, 1 - c))
                for j in range(N_CHIPS)]

    def start(*a):
        for cp in copies(*a):
            cp.start()

    def finish(*a):
        for cp in copies(*a):
            cp.wait()

    return _Carry([dw], [jax.ShapeDtypeStruct((N_CHIPS,) + dw.shape[2:], dw.dtype)], N_CHIPS, start, finish)


def _chips_carry(sums, land, part=(0, 1, 1)):
    a, b, p = part
    rh = sums.shape[1]
    rows = pl.ds(a * rh // p, (b - a) * rh // p)

    def copies(srcs, lands, ssem, rsem):
        x, y, c = _place()
        out = []
        for j, (cx, cy) in enumerate(_other_chips(x, y)):
            out.append(_remote(srcs[0].at[2 * cx + cy, rows], lands[0].at[j, rows], ssem.at[j], rsem.at[j],
                               (cx, cy, c)))
        return out

    def start(*a):
        for cp in copies(*a):
            cp.start()

    def finish(*a):
        for cp in copies(*a):
            cp.wait()

    if land is None:
        land = jax.ShapeDtypeStruct((3,) + sums.shape[1:], sums.dtype)
    return _Carry([sums], [land], 3, start, finish)


def _call(name, grid, body, ins, in_specs, out_shapes, out_specs, scratch=(), vmem_mb=48, carry=None):
    n_in, n_out, n_sc = len(ins), len(out_shapes), len(scratch)
    ins, in_specs = list(ins), list(in_specs)
    out_shapes, out_specs, scratch = list(out_shapes), list(out_specs), list(scratch)
    aliases = {}
    if carry is not None:
        ins += carry.srcs
        in_specs += [ANY] * len(carry.srcs)
        for land in carry.lands:
            if not isinstance(land, jax.ShapeDtypeStruct):
                aliases[len(ins)] = len(out_shapes)
                ins.append(land)
                in_specs.append(ANY)
                land = jax.ShapeDtypeStruct(land.shape, land.dtype)
            out_shapes.append(land)
            out_specs.append(ANY)
        scratch += [pltpu.SemaphoreType.DMA((carry.n_sems,)), pltpu.SemaphoreType.DMA((carry.n_sems,))]
    n_in_all, n_out_all = len(ins), len(out_shapes)
    ins = [pltpu.with_memory_space_constraint(a, pltpu.HBM) for a in ins]

    def kbody(*refs):
        in_refs, out_refs, sc = refs[:n_in_all], refs[n_in_all:n_in_all + n_out_all], refs[n_in_all + n_out_all:]
        ids = tuple(pl.program_id(a) for a in range(len(grid)))
        if carry is not None:
            first = functools.reduce(jnp.logical_and, [i == 0 for i in ids])
            last = functools.reduce(jnp.logical_and, [i == g - 1 for i, g in zip(ids, grid)])
            comm = (in_refs[n_in:n_in + len(carry.srcs)], out_refs[n_out:], sc[n_sc], sc[n_sc + 1])

            @pl.when(first)
            def _():
                carry.start(*comm)

            if carry.middle is not None:
                step, total = ids[0], grid[0]
                for i, g in zip(ids[1:], grid[1:]):
                    step, total = step * g + i, total * g

                @pl.when(step == total // 2)
                def _():
                    carry.middle(*comm)

        body(in_refs[:n_in], out_refs[:n_out], sc[:n_sc], ids)
        if carry is not None:
            @pl.when(last)
            def _():
                carry.finish(*comm)

    return pl.pallas_call(
        kbody, name=name, grid=grid, in_specs=in_specs, out_specs=out_specs,
        out_shape=[pltpu.HBM(o.shape, o.dtype) for o in out_shapes],
        scratch_shapes=scratch, input_output_aliases=aliases,
        compiler_params=pltpu.CompilerParams(dimension_semantics=("arbitrary",) * len(grid),
                                             vmem_limit_bytes=vmem_mb * MIB),
    )(*ins)


def _matmul(name, grid, kaxis, ins, in_specs, out_shapes, out_specs, dims, epi, tail=None, acc_shape=None, nc=512,
            vmem_mb=48, carry=None):
    nk = grid[kaxis] if kaxis is not None else 1

    def body(in_refs, out_refs, scratch, ids):
        a = in_refs[0][...]
        b_ref = in_refs[1]
        n = b_ref.shape[0] if dims == NT else b_ref.shape[1]

        def prod(c0):
            b = b_ref[c0:c0 + nc, :] if dims == NT else b_ref[:, c0:c0 + nc]
            return lax.dot_general(a, b, (dims, ((), ())), preferred_element_type=F32)

        if kaxis is None:
            for c0 in range(0, n, nc):
                epi(prod(c0), slice(c0, c0 + nc), ids, in_refs[2:], out_refs)
            if tail is not None:
                tail(ids, in_refs[2:], out_refs)
        else:
            acc = scratch[0]
            _zero_when(ids[kaxis] == 0, acc)
            for c0 in range(0, n, nc):
                acc[:, c0:c0 + nc] += prod(c0)

            @pl.when(ids[kaxis] == nk - 1)
            def _():
                epi(acc, ids, in_refs[2:], out_refs)

    return _call(name, grid, body, ins, in_specs, out_shapes, out_specs,
                 [pltpu.VMEM(acc_shape, F32)] if kaxis is not None else [], vmem_mb, carry)


def _row_rsqrt(xf):
    return lax.rsqrt(jnp.mean(xf * xf, axis=-1, keepdims=True) + EPS)


def _norm_bwd(dh, xf, g, resid):
    r = _row_rsqrt(xf)
    xh = xf * r
    dg = jnp.sum(dh * xh, axis=0, keepdims=True)
    dxh = dh * g
    dx = resid + r * (dxh - xh * jnp.mean(dxh * xh, axis=-1, keepdims=True))
    return dx, dg


def _for_rows(n, fn, sub=128):
    def step(q, carry):
        fn(pl.ds(pl.multiple_of(q * sub, sub), sub))
        return carry

    lax.fori_loop(0, n // sub, step, 0)


def _zero_when(first, *refs):
    @pl.when(first)
    def _():
        for ref in refs:
            ref[...] = jnp.zeros_like(ref)


_GELU_K = 0.7978845608028654
_GELU_C = 0.044715


def _gelu(x):
    t = jnp.tanh(_GELU_K * (x + _GELU_C * x * x * x))
    return 0.5 * x * (1.0 + t)


def _gelu_and_grad(x):
    x2 = x * x
    t = jnp.tanh(_GELU_K * (x + _GELU_C * x * x2))
    g = 0.5 * x * (1.0 + t)
    dg = 0.5 * (1.0 + t) + 0.5 * x * (1.0 - t * t) * (_GELU_K * (1.0 + 3.0 * _GELU_C * x2))
    return g, dg


def _window_sum(ext, w, causal):
    n = ext.shape[0]
    s, d = ext, 1
    while d < w:
        s = s + pltpu.roll(s, d if causal else n - d, 0)
        d *= 2
    return s


def _inv_count(t, w):
    return 1.0 / jnp.minimum(t + 1, w).astype(F32)


def _pooled(z_ref, zh_ref, g, w, i, tm):
    cols = slice(GD * g, GD * (g + 1))
    zb = z_ref[:, cols]
    zh = jnp.where(i > 0, zh_ref[:, cols], 0.0)
    ext = jnp.concatenate([zh, zb], axis=0)
    s = _window_sum(ext, w, True)[HALO:, :]
    t = i * tm + lax.broadcasted_iota(jnp.int32, (tm, 1), 0)
    return s * _inv_count(t, w) - zb


def _full(shape, n_axes=1):
    return pl.BlockSpec(shape, lambda *ids: (0,) * len(shape))


def _norm1(x, g, tm):
    s = x.shape[0]

    def body(ins, outs, scratch, ids):
        xf = ins[0][...]
        outs[0][...] = (xf * _row_rsqrt(xf) * ins[1][...]).astype(BF16)

    row = pl.BlockSpec((tm, D), lambda i: (i, 0))
    return _call("norm1", (s // tm,), body, [x, g], [row, _full((1, D))],
                 [jax.ShapeDtypeStruct((s, D), BF16)], [row], vmem_mb=32)[0]


def _mixer_fwd(proj, wt, bst, gv, wpool, scale, tm, carry):
    s = proj.shape[0]
    nq = tm // HD

    def body(ins, outs, scratch, ids):
        up_ref, vp_ref, z_ref, zh_ref, wt_ref, bst_ref, gv_ref, wp_ref, sc_ref = ins
        out_ref = outs[0]
        i = ids[0]
        for h in range(HEADS):
            cols = slice(HD * h, HD * (h + 1))
            gvh = gv_ref[:, cols]
            bcol = bst_ref[h]
            wth = wt_ref[h]

            def chunk(q, c_):
                rows = pl.ds(pl.multiple_of(q * HD, HD), HD)
                u = _gelu(up_ref[rows, cols])
                v = _gelu(vp_ref[rows, cols])
                vh = (v * _row_rsqrt(v) * gvh).astype(BF16)
                mixed = jnp.dot(wth, vh, preferred_element_type=F32) + bcol
                out_ref[rows, cols] = (u * mixed).astype(BF16)
                return c_

            lax.fori_loop(0, nq, chunk, 0)
        for g, w in enumerate(POOL_WINDOWS):
            cols = slice(GD * g, GD * (g + 1))
            pooled = _pooled(z_ref, zh_ref, g, w, i, tm)
            yv = jnp.dot(pooled.astype(BF16), wp_ref[g], preferred_element_type=F32)
            out_ref[:, A_W + GD * g:A_W + GD * (g + 1)] = (yv * sc_ref[:, cols]).astype(BF16)

    hb = tm // HALO
    return _call(
        "mixer_fwd", (s // tm,), body, [proj, proj, proj, proj, wt, bst, gv, wpool, scale],
        [pl.BlockSpec((tm, A_W), lambda i: (i, 0)),
         pl.BlockSpec((tm, A_W), lambda i: (i, 1)),
         pl.BlockSpec((tm, A_W), lambda i: (i, 2)),
         pl.BlockSpec((HALO, A_W), lambda i: (jnp.maximum(i * hb - 1, 0), 2)),
         _full((HEADS, HD, HD)), _full((HEADS, HD, HD)), _full((1, A_W)), _full((4, GD, GD)), _full((1, A_W))],
        [jax.ShapeDtypeStruct((s, D), BF16)], [pl.BlockSpec((tm, D), lambda i: (i, 0))], vmem_mb=40, carry=carry)


def _mixer_bwd(proj, dmix, wt, wtt, bst, gv, wpool, scale, tm, carry):
    s = proj.shape[0]
    nb = s // tm
    nq = tm // HD
    hb = tm // HALO

    def body(ins, outs, scratch, ids):
        (up_ref, vp_ref, z_ref, zh_ref, doa_ref, dob_ref, dobh_ref, wt_ref, wtt_ref, bst_ref, gv_ref, wp_ref,
         sc_ref) = ins
        dproj_ref, dws_ref, dbs_ref, dgv_ref, dwp_ref, dsc_ref = outs
        dbfull = scratch[0]
        i = ids[0]
        last = i == nb - 1
        _zero_when(i == 0, dws_ref, dbfull, dgv_ref, dwp_ref, dsc_ref)

        for h in range(HEADS):
            cols = slice(HD * h, HD * (h + 1))
            gvh = gv_ref[:, cols]
            bcol = bst_ref[h]
            wth = wt_ref[h]
            wtth = wtt_ref[h]

            def chunk(q, c_):
                rows = pl.ds(pl.multiple_of(q * HD, HD), HD)
                u, du_dup = _gelu_and_grad(up_ref[rows, cols])
                v, dv_dvp = _gelu_and_grad(vp_ref[rows, cols])
                rv = _row_rsqrt(v)
                vn = v * rv
                vh = (vn * gvh).astype(BF16)
                mixed = jnp.dot(wth, vh, preferred_element_type=F32) + bcol
                doa = doa_ref[rows, cols].astype(F32)
                dmx = doa * u
                dmxb = dmx.astype(BF16)
                dbfull[h] += dmx
                dws_ref[h] += lax.dot_general(dmxb, vh, (NT, ((), ())), preferred_element_type=F32)
                dvh = jnp.dot(wtth, dmxb, preferred_element_type=F32)
                dgv_ref[:, cols] += jnp.sum(dvh * vn, axis=0, keepdims=True)
                dvn = dvh * gvh
                dv = rv * (dvn - vn * jnp.mean(dvn * vn, axis=-1, keepdims=True))
                dproj_ref[rows, cols] = (doa * mixed * du_dup).astype(BF16)
                dproj_ref[rows, A_W + HD * h:A_W + HD * (h + 1)] = (dv * dv_dvp).astype(BF16)
                return c_

            lax.fori_loop(0, nq, chunk, 0)

        t = i * tm + lax.broadcasted_iota(jnp.int32, (tm, 1), 0)
        th = (i + 1) * tm + lax.broadcasted_iota(jnp.int32, (HALO, 1), 0)
        for g, w in enumerate(POOL_WINDOWS):
            cols = slice(GD * g, GD * (g + 1))
            wpg = wp_ref[g]
            scg = sc_ref[:, cols]
            pb = _pooled(z_ref, zh_ref, g, w, i, tm).astype(BF16)
            ypre = jnp.dot(pb, wpg, preferred_element_type=F32)
            dob = dob_ref[:, cols].astype(F32)
            dsc_ref[:, cols] += jnp.sum(dob * ypre, axis=0, keepdims=True)
            dyb = (dob * scg).astype(BF16)
            dwp_ref[g] += lax.dot_general(pb, dyb, (TN, ((), ())), preferred_element_type=F32)
            dpo = lax.dot_general(dyb, wpg, (NT, ((), ())), preferred_element_type=F32)
            dyh = (dobh_ref[:, cols].astype(F32) * scg).astype(BF16)
            dpoh = lax.dot_general(dyh, wpg, (NT, ((), ())), preferred_element_type=F32)
            dpoh = jnp.where(last, 0.0, dpoh * _inv_count(th, w))
            ext = jnp.concatenate([dpo * _inv_count(t, w), dpoh], axis=0)
            dz = _window_sum(ext, w, False)[:tm, :] - dpo
            dproj_ref[:, 2 * A_W + GD * g:2 * A_W + GD * (g + 1)] = dz.astype(BF16)

        @pl.when(last)
        def _():
            r = lax.broadcasted_iota(jnp.int32, (HD, HD), 0)
            c = lax.broadcasted_iota(jnp.int32, (HD, HD), 1)
            for h in range(HEADS):
                dws_ref[h] = jnp.where(r >= c, dws_ref[h], 0.0)
                dbs_ref[h] = jnp.sum(dbfull[h], axis=-1, keepdims=True)

    return _call(
        "mixer_bwd", (nb,), body, [proj, proj, proj, proj, dmix, dmix, dmix, wt, wtt, bst, gv, wpool, scale],
        [pl.BlockSpec((tm, A_W), lambda i: (i, 0)),
         pl.BlockSpec((tm, A_W), lambda i: (i, 1)),
         pl.BlockSpec((tm, A_W), lambda i: (i, 2)),
         pl.BlockSpec((HALO, A_W), lambda i: (jnp.maximum(i * hb - 1, 0), 2)),
         pl.BlockSpec((tm, A_W), lambda i: (i, 0)),
         pl.BlockSpec((tm, A_W), lambda i: (i, 1)),
         pl.BlockSpec((HALO, A_W), lambda i: (jnp.minimum((i + 1) * hb, s // HALO - 1), 1)),
         _full((HEADS, HD, HD)), _full((HEADS, HD, HD)), _full((HEADS, HD, HD)), _full((1, A_W)),
         _full((4, GD, GD)), _full((1, A_W))],
        [jax.ShapeDtypeStruct((s, IN_W), BF16),
         jax.ShapeDtypeStruct((HEADS, HD, HD), F32),
         jax.ShapeDtypeStruct((HEADS, HD, 1), F32),
         jax.ShapeDtypeStruct((1, A_W), F32),
         jax.ShapeDtypeStruct((4, GD, GD), F32),
         jax.ShapeDtypeStruct((1, A_W), F32)],
        [pl.BlockSpec((tm, IN_W), lambda i: (i, 0)),
         _full((HEADS, HD, HD)), _full((HEADS, HD, 1)), _full((1, A_W)), _full((4, GD, GD)), _full((1, A_W))],
        [pltpu.VMEM((HEADS, HD, HD), F32)], vmem_mb=48, carry=carry)


def _inproj(h1, win_g, tm, carry):
    s = h1.shape[0]
    cw = IN_W // N_CHIPS

    def epi(p, cols, ids, extra, outs):
        outs[0][:, cols] = p

    return _matmul(
        "inproj", (N_CHIPS, s // tm), None, [h1, win_g],
        [pl.BlockSpec((tm, D), lambda j, i: (i, 0)), pl.BlockSpec((None, D, cw), lambda j, i: (j, 0, 0))],
        [jax.ShapeDtypeStruct((s, IN_W), F32)], [pl.BlockSpec((tm, cw), lambda j, i: (i, j))], NN, epi,
        nc=256, vmem_mb=32, carry=carry)


def _outproj(mixed, wout, x, g_ffn, tm, carry):
    s = x.shape[0]

    def epi(p, cols, ids, extra, outs):
        outs[0][:, cols] = extra[0][:, cols] + p

    def tail(ids, extra, outs):
        x2 = outs[0][...]
        outs[1][...] = (x2 * _row_rsqrt(x2) * extra[1][...]).astype(BF16)

    row = pl.BlockSpec((tm, D), lambda i: (i, 0))
    return _matmul(
        "outproj", (s // tm,), None, [mixed, wout, x, g_ffn], [row, _full((D, D)), row, _full((1, D))],
        [jax.ShapeDtypeStruct((s, D), F32), jax.ShapeDtypeStruct((s, D), BF16)], [row, row], NN, epi, tail,
        vmem_mb=48, carry=carry)


def _up(h2, wup_g, tm, tn, carry):
    s = h2.shape[0]
    per = D // tn

    def epi(p, cols, ids, extra, outs):
        a = jnp.maximum(p, 0.0)
        outs[0][:, cols] = (a * a).astype(BF16)

    return _matmul(
        "up", (D_FF // tn, s // tm), None, [h2, wup_g],
        [pl.BlockSpec((tm, D), lambda j, i: (i, 0)), pl.BlockSpec((None, D, tn), lambda j, i: (j // per, 0, j % per))],
        [jax.ShapeDtypeStruct((s, D_FF), BF16)], [pl.BlockSpec((tm, tn), lambda j, i: (i, j))], NN, epi,
        vmem_mb=40, carry=carry)


def _down(act, wdown, x2, tgt, g_final, tm, tk):
    s = x2.shape[0]

    def epi(acc, ids, extra, outs):
        x2_ref, t_ref, g_ref = extra
        dxb_ref, dgf_ref, loss_ref = outs
        g = g_ref[...]
        _zero_when(ids[0] == 0, dgf_ref, loss_ref)

        def block(rows):
            x3 = x2_ref[rows, :] + acc[rows, :]
            r = _row_rsqrt(x3)
            xh = x3 * r
            diff = xh * g - t_ref[rows, :]
            dy = diff * (1.0 / D)
            dxh = dy * g
            dx = r * (dxh - xh * jnp.mean(dxh * xh, axis=-1, keepdims=True))
            dxb_ref[rows, :] = dx.astype(BF16)
            dgf_ref[...] += jnp.sum(dy * xh, axis=0, keepdims=True)
            loss_ref[...] += jnp.sum(diff * diff, axis=0, keepdims=True)

        _for_rows(tm, block)

    row = pl.BlockSpec((tm, D), lambda i, k: (i, 0))
    vec = pl.BlockSpec((1, D), lambda i, k: (0, 0))
    return _matmul(
        "down", (s // tm, D_FF // tk), 1, [act, wdown, x2, tgt, g_final],
        [pl.BlockSpec((tm, tk), lambda i, k: (i, k)), pl.BlockSpec((tk, D), lambda i, k: (k, 0)), row, row, vec],
        [jax.ShapeDtypeStruct((s, D), BF16), jax.ShapeDtypeStruct((1, D), F32), jax.ShapeDtypeStruct((1, D), F32)],
        [row, vec, vec], NN, epi, acc_shape=(tm, D), vmem_mb=56)


def _dact(dx3b, wdown, act, tm, tn, carry):
    s = dx3b.shape[0]

    def epi(p, cols, ids, extra, outs):
        outs[0][:, cols] = (p * (2.0 * jnp.sqrt(extra[0][:, cols].astype(F32)))).astype(BF16)

    tile = pl.BlockSpec((tm, tn), lambda j, i: (i, j))
    return _matmul(
        "dact", (D_FF // tn, s // tm), None, [dx3b, wdown, act],
        [pl.BlockSpec((tm, D), lambda j, i: (i, 0)), pl.BlockSpec((tn, D), lambda j, i: (j, 0)), tile],
        [jax.ShapeDtypeStruct((s, D_FF), BF16)], [tile], NT, epi, vmem_mb=40, carry=carry)


def _dweight(name, lhs, rhs, n_shards, rows, cols, tm, nc=512, carry=None):
    s = lhs.shape[0]

    def epi(p, cs, ids, extra, outs):
        outs[0][:, cs] = p.astype(BF16)

    return _matmul(
        name, (n_shards, rows // tm), None, [lhs, rhs],
        [pl.BlockSpec((s, tm), lambda j, i: (0, i)), pl.BlockSpec((s, cols), lambda j, i: (0, j))],
        [jax.ShapeDtypeStruct((n_shards, rows, cols), BF16)],
        [pl.BlockSpec((None, tm, cols), lambda j, i: (j, i, 0))], TN, epi, nc=nc, vmem_mb=56, carry=carry)


def _dh2(da, wup_g, x2, dx3, g_ffn, tm, tk, carry):
    s = x2.shape[0]
    per = D // tk

    def epi(acc, ids, extra, outs):
        x2_ref, dx3_ref, g_ref = extra
        g = g_ref[...]
        _zero_when(ids[0] == 0, outs[1])

        def block(rows):
            dx, dg = _norm_bwd(acc[rows, :], x2_ref[rows, :], g, dx3_ref[rows, :].astype(F32))
            outs[0][rows, :] = dx.astype(BF16)
            outs[1][...] += dg

        _for_rows(tm, block)

    row = pl.BlockSpec((tm, D), lambda i, k: (i, 0))
    vec = pl.BlockSpec((1, D), lambda i, k: (0, 0))
    return _matmul(
        "dh2", (s // tm, D_FF // tk), 1, [da, wup_g, x2, dx3, g_ffn],
        [pl.BlockSpec((tm, tk), lambda i, k: (i, k)),
         pl.BlockSpec((None, D, tk), lambda i, k: (k // per, 0, k % per)), row, row, vec],
        [jax.ShapeDtypeStruct((s, D), BF16), jax.ShapeDtypeStruct((1, D), F32)],
        [row, vec], NT, epi, acc_shape=(tm, D), vmem_mb=56, carry=carry)


def _dmixed(dx2b, wout, tm, carry):
    s = dx2b.shape[0]

    def epi(p, cols, ids, extra, outs):
        outs[0][:, cols] = p.astype(BF16)

    row = pl.BlockSpec((tm, D), lambda i: (i, 0))
    return _matmul(
        "dmixed", (s // tm,), None, [dx2b, wout], [row, _full((D, D))],
        [jax.ShapeDtypeStruct((s, D), BF16)], [row], NT, epi, vmem_mb=40, carry=carry)


def _dh1(dproj, win_g, x, dx2, g_mix, tm, carry):
    s = x.shape[0]
    cw = IN_W // N_CHIPS

    def epi(acc, ids, extra, outs):
        x_ref, dx2_ref, g_ref = extra
        g = g_ref[...]
        _zero_when(ids[0] == 0, outs[1])

        def block(rows):
            dx, dg = _norm_bwd(acc[rows, :], x_ref[rows, :], g, dx2_ref[rows, :].astype(F32))
            outs[0][rows, :] = dx
            outs[1][...] += dg

        _for_rows(tm, block)

    row = pl.BlockSpec((tm, D), lambda i, j: (i, 0))
    vec = pl.BlockSpec((1, D), lambda i, j: (0, 0))
    return _matmul(
        "dh1", (s // tm, N_CHIPS), 1, [dproj, win_g, x, dx2, g_mix],
        [pl.BlockSpec((tm, cw), lambda i, j: (i, j)), pl.BlockSpec((None, D, cw), lambda i, j: (j, 0, 0)),
         row, row, vec],
        [jax.ShapeDtypeStruct((s, D), F32), jax.ShapeDtypeStruct((1, D), F32)],
        [row, vec], NT, epi, acc_shape=(tm, D), vmem_mb=52, carry=carry)


def _cast_place(name, w, tr):
    rows, cols = w.shape

    def body(ins, outs, scratch, ids):
        outs[0][...] = ins[0][...].astype(BF16)

    return _call(name, (rows // tr,), body, [w], [pl.BlockSpec((tr, cols), lambda r: (r, 0))],
                 [jax.ShapeDtypeStruct((N_CHIPS, rows, cols), BF16)],
                 [pl.BlockSpec((None, tr, cols), lambda r: (_my_chip(), r, 0))], vmem_mb=32)[0]


def _small_allreduce(part):
    rows = part.shape[0]
    rh = rows // 2

    def body(p_ref, o_ref, sib_ref, slots, send_sems, recv_sems):
        x, y, c = _place()
        k = 2 * x + y
        sib = (x, y, 1 - c)
        half = pl.ds(pl.multiple_of(c * rh, 8), rh)
        cp = _remote(p_ref, sib_ref, send_sems.at[0], recv_sems.at[0], sib)
        cp.start()
        cp.wait()
        slots[k] = p_ref[half, :] + sib_ref[half, :]
        cps = []
        for j, (cx, cy) in enumerate(_other_chips(x, y)):
            cp = _remote(slots.at[k], slots.at[k], send_sems.at[1 + j], recv_sems.at[1 + j], (cx, cy, c))
            cp.start()
            cps.append(cp)
        for cp in cps:
            cp.wait()
        o_ref[half, :] = ((slots[0] + slots[1]) + slots[2]) + slots[3]
        cp = _remote(o_ref.at[half], o_ref.at[half], send_sems.at[4], recv_sems.at[4], sib)
        cp.start()
        cp.wait()

    vm = pl.BlockSpec(memory_space=pltpu.VMEM)
    return pl.pallas_call(
        body, name="small_allreduce", in_specs=[vm], out_specs=vm,
        out_shape=jax.ShapeDtypeStruct(part.shape, F32),
        scratch_shapes=[pltpu.VMEM(part.shape, F32), pltpu.VMEM((N_CHIPS, rh, LANES), F32),
                        pltpu.SemaphoreType.DMA((5,)), pltpu.SemaphoreType.DMA((5,))],
        compiler_params=pltpu.CompilerParams(vmem_limit_bytes=32 * MIB),
    )(part)


def _comm_only(name, carry):
    ns, nl = len(carry.srcs), len(carry.lands)
    lands_in = [l for l in carry.lands if not isinstance(l, jax.ShapeDtypeStruct)]
    assert len(lands_in) in (0, nl)

    def body(*refs):
        srcs = refs[:ns]
        lands = refs[ns + len(lands_in):ns + len(lands_in) + nl]
        ssem, rsem = refs[ns + len(lands_in) + nl:]
        carry.start(srcs, lands, ssem, rsem)
        if carry.middle is not None:
            carry.middle(srcs, lands, ssem, rsem)
        carry.finish(srcs, lands, ssem, rsem)

    return pl.pallas_call(
        body, name=name, in_specs=[ANY] * (ns + len(lands_in)), out_specs=[ANY] * nl,
        out_shape=[jax.ShapeDtypeStruct(l.shape, l.dtype) for l in carry.lands],
        input_output_aliases={ns + i: i for i in range(len(lands_in))},
        scratch_shapes=[pltpu.SemaphoreType.DMA((carry.n_sems,)), pltpu.SemaphoreType.DMA((carry.n_sems,))],
    )(*carry.srcs, *lands_in)


def _share_carry(grads):
    n = len(grads)

    def copies(srcs, lands, ssem, rsem):
        x, y, c = _place()
        return [_remote(lands[w].at[c], lands[w].at[c], ssem.at[w], rsem.at[w], (x, y, 1 - c)) for w in range(n)]

    def start(*a):
        for cp in copies(*a):
            cp.start()

    def finish(*a):
        for cp in copies(*a):
            cp.wait()

    return _Carry([], grads, n, start, finish)


def _add_sibling(dw, got, tr):
    _, _, rh, cols = dw.shape

    def body(ins, outs, scratch, ids):
        outs[0][...] = (ins[0][...].astype(F32) + ins[1][...].astype(F32)).astype(BF16)

    blk = pl.BlockSpec((None, tr, cols), lambda j, r: (j, r, 0))
    return _call("add_sibling", (N_CHIPS, rh // tr), body, [dw, got],
                 [pl.BlockSpec((None, None, tr, cols), lambda j, r: (j, _my_core(), r, 0)), blk],
                 [jax.ShapeDtypeStruct((N_CHIPS, rh, cols), BF16)], [blk], vmem_mb=32)[0]


def _add_chips(sums, got, tr):
    _, rh, cols = sums.shape

    def body(ins, outs, scratch, ids):
        b = ins[1][...].astype(F32)
        outs[0][...] = ((ins[0][...].astype(F32) + b[0]) + b[1]) + b[2]

    return _call("add_chips", (rh // tr,), body, [sums, got],
                 [pl.BlockSpec((None, tr, cols), lambda r: (_my_chip(), r, 0)),
                  pl.BlockSpec((3, tr, cols), lambda r: (0, r, 0))],
                 [jax.ShapeDtypeStruct((2, rh, cols), F32)],
                 [pl.BlockSpec((None, tr, cols), lambda r: (_my_core(), r, 0))], vmem_mb=32)[0]


def _adamw_math(w, g, m, v):
    m = ADAM_B1 * m + (1.0 - ADAM_B1) * g
    v = ADAM_B2 * v + (1.0 - ADAM_B2) * (g * g)
    m_hat = m / (1.0 - ADAM_B1 ** ADAM_STEP)
    v_hat = v / (1.0 - ADAM_B2 ** ADAM_STEP)
    delta = -ADAM_LR * (m_hat / (jnp.sqrt(v_hat) + ADAM_EPS) + ADAM_WD * w)
    return delta, m, v


def _adamw(name, w, g, m, v, tr, carry=None):
    rows, cols = w.shape

    def body(ins, outs, scratch, ids):
        g_val = ins[1][...]
        outs[0][...] = g_val
        outs[1][...], outs[2][...], outs[3][...] = _adamw_math(ins[0][...], g_val, ins[2][...], ins[3][...])

    blk = pl.BlockSpec((tr, cols), lambda r: (r, 0))
    return _call(name, (rows // tr,), body, [w, g, m, v], [blk] * 4,
                 [jax.ShapeDtypeStruct(w.shape, F32)] * 4, [blk] * 4, vmem_mb=40, carry=carry)


def _rows(a):
    return a.reshape(-1, LANES)


def kernel(x, g_mix, w_in, g_v, w_s, b_s, w_pool, pool_scale, w_out, g_ffn, w_up, w_down, g_final, loss_target, m_g_mix, m_w_in, m_g_v, m_w_s, m_b_s, m_w_pool, m_pool_scale, m_w_out, m_g_ffn, m_w_up, m_w_down, m_g_final, v_g_mix, v_w_in, v_g_v, v_w_s, v_b_s, v_w_pool, v_pool_scale, v_w_out, v_g_ffn, v_w_up, v_w_down, v_g_final):
    tm = 512
    xs = x[0]
    tgt = loss_target[0]
    chip = _my_chip()

    win_g = _cast_place("cast_w_in", w_in[0], 256)
    wpool_g = _cast_place("cast_w_pool", w_pool[0].reshape(4 * 64, GD), 128)
    wout_g = _cast_place("cast_w_out", w_out[0], 128)
    wup_g = _cast_place("cast_w_up", w_up[0], 256)
    wdown_g = _cast_place("cast_w_down", w_down[0], 256)
    win_g, wpool_g = _comm_only("allgather_first", _gather_whole([win_g, wpool_g]))
    wpool_f = wpool_g.reshape(N_CHIPS, 4, 64, GD).transpose(1, 0, 2, 3).reshape(4, GD, GD)
    tril = jnp.tril(jnp.ones((HD, HD), dtype=bool))
    wt = jnp.where(tril[None], w_s[0], 0.0).astype(BF16)
    wtt = wt.transpose(0, 2, 1)
    bst = jnp.broadcast_to(b_s[0][:, :, None], (HEADS, HD, HD))
    gfin = g_final.reshape(1, D)

    h1 = _norm1(xs, g_mix, tm)
    proj, wout_g = _inproj(h1, win_g, tm, _gather_whole([wout_g]))
    mixed, wup_g = _mixer_fwd(proj, wt, bst, g_v, wpool_f, pool_scale, tm, _gather_near([wup_g]))
    wout_f = wout_g.reshape(D, D)
    x2, h2, wup_g = _outproj(mixed, wout_f, xs, g_ffn, 256, _gather_far([wup_g]))
    act, wdown_g = _up(h2, wup_g, tm, 2048, _gather_whole([wdown_g]))
    wdown_f = wdown_g.reshape(D_FF, D)
    dx3b, dgf, lossv = _down(act, wdown_f, x2, tgt, gfin, tm, 2048)

    halves = lambda dw, rows, cols: dw.reshape(N_CHIPS, 2, rows // (2 * N_CHIPS), cols)
    cw = IN_W // N_CHIPS
    dwdown = halves(_dweight("dw_down", act, dx3b, 1, D_FF, D, 512)[0], D_FF, D)
    da, sib_down = _dact(dx3b, wdown_f, act, tm, 2048, _sibling_carry(dwdown))
    sum_down = _add_sibling(dwdown, sib_down, 256)
    dwup, got_down = _dweight("dw_up", h2, da, N_CHIPS, D, D, 512, carry=_chips_carry(sum_down, None, (0, 3, 4)))
    dwup = halves(dwup, D_FF, D)
    dx2b, dgffn, got_down, sib_up = _dh2(da, wup_g, x2, dx3b, g_ffn, tm, 2048,
                                         _join(_chips_carry(sum_down, got_down, (3, 4, 4)), _sibling_carry(dwup)))
    half_down = _add_chips(sum_down, got_down, 256)
    sum_up = _add_sibling(dwup, sib_up, 256)
    dwout, got_up = _dweight("dw_out", mixed, dx2b, 1, D, D, 512, carry=_chips_carry(sum_up, None, (0, 1, 4)))
    dwout = halves(dwout, D, D)
    dmix, got_up, sib_out, half_down = _dmixed(
        dx2b, wout_f, 256,
        _join(_chips_carry(sum_up, got_up, (1, 2, 4)), _sibling_carry(dwout), _share_carry([half_down])))
    sum_out = _add_sibling(dwout, sib_out, 256)
    g_down, d_down, nm_down, nv_down = _adamw("adamw_down", w_down[0], half_down.reshape(D, D), m_w_down[0],
                                              v_w_down[0], 128)
    dproj, dws, dbs, dgv, dwp, dsc, got_up = _mixer_bwd(proj, dmix, wt, wtt, bst, g_v, wpool_f, pool_scale, tm,
                                                        _chips_carry(sum_up, got_up, (2, 4, 4)))
    half_up = _add_chips(sum_up, got_up, 256)
    dwin, got_out, half_up = _dweight("dw_in", h1, dproj, N_CHIPS, D, cw, 512, nc=256,
                                      carry=_join(_chips_carry(sum_out, None), _share_carry([half_up])))
    dwin = halves(dwin, N_CHIPS * D, cw)
    sib_in = _comm_only("sibling_in", _sibling_carry(dwin))[0]
    sum_in = _add_sibling(dwin, sib_in, 256)
    g_up, d_up, nm_up, nv_up = _adamw("adamw_up", w_up[0], half_up.reshape(D, D), m_w_up[0], v_w_up[0], 128)
    half_out = _add_chips(sum_out, got_out, 256)
    grad_x, dgmix, got_in, half_out = _dh1(dproj, win_g, xs, dx2b, g_mix, tm,
                                           _join(_chips_carry(sum_in, None), _share_carry([half_out])))
    half_in = _add_chips(sum_in, got_in, 256)
    half_in = _comm_only("share_half_in", _share_carry([half_in]))[0]
    g_out, d_out, nm_out, nv_out = _adamw("adamw_out", w_out[0], half_out.reshape(D // N_CHIPS, D), m_w_out[0],
                                          v_w_out[0], 128)
    g_in, d_in, nm_in, nv_in = _adamw("adamw_in", w_in[0], half_in.reshape(D, cw), m_w_in[0], v_w_in[0], 128)

    pieces = [dgmix, dgv, dws, dbs, dwp, dsc, dgffn, dgf, lossv, jnp.zeros((8 * LANES,), F32)]
    sizes = [p.size // LANES for p in pieces]
    tot = _small_allreduce(jnp.concatenate([_rows(p) for p in pieces], axis=0))
    offs = [sum(sizes[:i]) for i in range(len(sizes))]
    take = lambda i: tot[offs[i]:offs[i] + sizes[i]]
    s_gmix, s_gv, s_ws, s_bs, s_wp, s_sc, s_gffn, s_gf = [take(i) for i in range(8)]
    loss = (0.5 / D) * jnp.sum(take(8))
    s_wp_mine = lax.dynamic_slice_in_dim(s_wp.reshape(4, GD, GD), chip * 64, 64, axis=1)
    small_g = [s_gmix, s_gv, s_ws, s_bs, _rows(s_wp_mine), s_sc, s_gffn, s_gf]
    small_w = [g_mix, g_v, w_s, b_s, w_pool, pool_scale, g_ffn, g_final]
    small_m = [m_g_mix, m_g_v, m_w_s, m_b_s, m_w_pool, m_pool_scale, m_g_ffn, m_g_final]
    small_v = [v_g_mix, v_g_v, v_w_s, v_b_s, v_w_pool, v_pool_scale, v_g_ffn, v_g_final]
    cat = lambda parts: jnp.concatenate([_rows(p) for p in parts], axis=0)
    sg = cat(small_g)
    sg, sd, snm, snv = _adamw("adamw_small", cat(small_w), sg, cat(small_m), cat(small_v), sg.shape[0])
    ssz = [p.size // LANES for p in small_w]
    soff = [sum(ssz[:i]) for i in range(len(ssz))]
    split = lambda a: [a[soff[i]:soff[i] + ssz[i]].reshape(small_w[i].shape) for i in range(len(ssz))]
    gs, ds, nms, nvs = split(sg), split(sd), split(snm), split(snv)

    def ordered(small, w_in_, w_out_, w_up_, w_down_):
        return [small[0], w_in_[None], small[1], small[2], small[3], small[4], small[5], w_out_[None], small[6],
                w_up_[None], w_down_[None], small[7]]

    return (loss, grad_x[None],
            *ordered(gs, g_in, g_out, g_up, g_down),
            *ordered(ds, d_in, d_out, d_up, d_down),
            *ordered(nms, nm_in, nm_out, nm_up, nm_down),
            *ordered(nvs, nv_in, nv_out, nv_up, nv_down))
```

```python
import functools

import jax
import jax.numpy as jnp
from jax import lax
from jax.experimental import pallas as pl
from jax.experimental.pallas import tpu as pltpu

F32 = jnp.float32
BF16 = jnp.bfloat16
EPS = 1e-6
D = 2048
A_W = 1024
HEADS = 8
HD = 128
POOL_WINDOWS = (2, 4, 8, 16)
GD = 256
IN_W = 3072
D_FF = 8192
N_CHIPS = 4
HALO = 16
LANES = 128
MIB = 2 ** 20

ADAM_LR, ADAM_B1, ADAM_B2, ADAM_EPS, ADAM_WD, ADAM_STEP = 0.001, 0.9, 0.999, 1e-08, 0.01, 10

ANY = pl.BlockSpec(memory_space=pl.ANY)
MESH = pl.DeviceIdType.MESH

NN = ((1,), (0,))
NT = ((1,), (1,))
TN = ((0,), (0,))


def _place():
    return lax.axis_index("x"), lax.axis_index("y"), lax.axis_index("c")


def _my_chip():
    return 2 * lax.axis_index("x") + lax.axis_index("y")


def _my_core():
    return lax.axis_index("c")


def _other_chips(x, y):
    return [(1 - x, y), (x, 1 - y), (1 - x, 1 - y)]


def _remote(src, dst, send_sem, recv_sem, dev):
    return pltpu.make_async_remote_copy(src_ref=src, dst_ref=dst, send_sem=send_sem, recv_sem=recv_sem,
                                        device_id=dev, device_id_type=MESH)


class _SemView:
    def __init__(self, sems, base):
        self.sems, self.base = sems, base

    @property
    def at(self):
        return self

    def __getitem__(self, i):
        return self.sems.at[self.base + i]


class _Carry:
    def __init__(self, srcs, lands, n_sems, start, finish, middle=None):
        self.srcs, self.lands, self.n_sems, self.start, self.finish = list(srcs), list(lands), n_sems, start, finish
        self.middle = middle


def _join(*carries):
    def run(which):
        def go(srcs, lands, ssem, rsem):
            so = lo = qo = 0
            for c in carries:
                if getattr(c, which) is not None:
                    getattr(c, which)(srcs[so:so + len(c.srcs)], lands[lo:lo + len(c.lands)],
                                      _SemView(ssem, qo), _SemView(rsem, qo))
                so, lo, qo = so + len(c.srcs), lo + len(c.lands), qo + c.n_sems
        return go

    middle = run("middle") if any(c.middle is not None for c in carries) else None
    return _Carry([s for c in carries for s in c.srcs], [l for c in carries for l in c.lands],
                  sum(c.n_sems for c in carries), run("start"), run("finish"), middle)


GATHER_SEMS = 7


def _gather_copies(lands, ssem, rsem):
    x, y, c = _place()
    k, kx, ky, kd = 2 * x + y, 2 * (1 - x) + y, 2 * x + (1 - y), 2 * (1 - x) + (1 - y)
    to_x, to_y, sib = (1 - x, y, c), (x, 1 - y, c), (x, y, 1 - c)
    out = []
    for w, land in enumerate(lands):
        rh = land.shape[1] // 2
        rq = rh // 2
        half = pl.ds(pl.multiple_of(c * rh, 16), rh)
        quarters = [pl.ds(pl.multiple_of(c * rh + q * rq, 16), rq) for q in range(2)]

        def cp(i, piece, dev, w=w):
            return _remote(piece, piece, ssem.at[GATHER_SEMS * w + i], rsem.at[GATHER_SEMS * w + i], dev)

        out.append(dict(
            ax=cp(0, land.at[k, half], to_x), ay=cp(1, land.at[k, half], to_y),
            rx=cp(2, land.at[kx, quarters[0]], to_y), ry=cp(3, land.at[ky, quarters[1]], to_x),
            fx=cp(4, land.at[kx, half], sib), fy=cp(5, land.at[ky, half], sib), fd=cp(6, land.at[kd, half], sib)))
    return out


def _gather_whole(gathered):
    def start(srcs, lands, ssem, rsem):
        for d in _gather_copies(lands, ssem, rsem):
            d["ax"].start()
            d["ay"].start()

    def middle(srcs, lands, ssem, rsem):
        for d in _gather_copies(lands, ssem, rsem):
            d["ax"].wait_recv()
            d["rx"].start()
            d["fx"].start()
            d["ay"].wait_recv()
            d["ry"].start()
            d["fy"].start()

    def finish(srcs, lands, ssem, rsem):
        for d in _gather_copies(lands, ssem, rsem):
            d["rx"].wait_recv()
            d["ry"].wait_recv()
            d["fd"].start()
            for name in ("fx", "fy", "fd"):
                d[name].wait_recv()
            for cp in d.values():
                cp.wait_send()

    return _Carry([], gathered, GATHER_SEMS * len(gathered), start, finish, middle)


def _gather_near(gathered):
    def start(srcs, lands, ssem, rsem):
        for d in _gather_copies(lands, ssem, rsem):
            d["ax"].start()
            d["ay"].start()

    def finish(srcs, lands, ssem, rsem):
        for d in _gather_copies(lands, ssem, rsem):
            for name in ("ax", "ay"):
                d[name].wait_recv()
                d[name].wait_send()

    return _Carry([], gathered, GATHER_SEMS * len(gathered), start, finish)


def _gather_far(gathered):
    def start(srcs, lands, ssem, rsem):
        for d in _gather_copies(lands, ssem, rsem):
            for name in ("rx", "ry", "fx", "fy"):
                d[name].start()

    def finish(srcs, lands, ssem, rsem):
        for d in _gather_copies(lands, ssem, rsem):
            d["rx"].wait_recv()
            d["ry"].wait_recv()
            d["fd"].start()
            for name in ("fx", "fy", "fd"):
                d[name].wait_recv()
            for name in ("rx", "ry", "fx", "fy", "fd"):
                d[name].wait_send()

    return _Carry([], gathered, GATHER_SEMS * len(gathered), start, finish)


def _sibling_carry(dw):
    def copies(srcs, lands, ssem, rsem):
        x, y, c = _place()
        return [_remote(srcs[0].at[j, 1 - c], lands[0].at[j], ssem.at[j], rsem.at[j], (x, y, 1 - c))
                for j in range(N_CHIPS)]

    def start(*a):
        for cp in copies(*a):
            cp.start()

    def finish(*a):
        for cp in copies(*a):
            cp.wait()

    return _Carry([dw], [jax.ShapeDtypeStruct((N_CHIPS,) + dw.shape[2:], dw.dtype)], N_CHIPS, start, finish)


def _chips_carry(sums, land, part=(0, 1, 1)):
    a, b, p = part
    rh = sums.shape[1]
    rows = pl.ds(a * rh // p, (b - a) * rh // p)

    def copies(srcs, lands, ssem, rsem):
        x, y, c = _place()
        out = []
        for j, (cx, cy) in enumerate(_other_chips(x, y)):
            out.append(_remote(srcs[0].at[2 * cx + cy, rows], lands[0].at[j, rows], ssem.at[j], rsem.at[j],
                               (cx, cy, c)))
        return out

    def start(*a):
        for cp in copies(*a):
            cp.start()

    def finish(*a):
        for cp in copies(*a):
            cp.wait()

    if land is None:
        land = jax.ShapeDtypeStruct((3,) + sums.shape[1:], sums.dtype)
    return _Carry([sums], [land], 3, start, finish)


def _call(name, grid, body, ins, in_specs, out_shapes, out_specs, scratch=(), vmem_mb=48, carry=None):
    n_in, n_out, n_sc = len(ins), len(out_shapes), len(scratch)
    ins, in_specs = list(ins), list(in_specs)
    out_shapes, out_specs, scratch = list(out_shapes), list(out_specs), list(scratch)
    aliases = {}
    if carry is not None:
        ins += carry.srcs
        in_specs += [ANY] * len(carry.srcs)
        for land in carry.lands:
            if not isinstance(land, jax.ShapeDtypeStruct):
                aliases[len(ins)] = len(out_shapes)
                ins.append(land)
                in_specs.append(ANY)
                land = jax.ShapeDtypeStruct(land.shape, land.dtype)
            out_shapes.append(land)
            out_specs.append(ANY)
        scratch += [pltpu.SemaphoreType.DMA((carry.n_sems,)), pltpu.SemaphoreType.DMA((carry.n_sems,))]
    n_in_all, n_out_all = len(ins), len(out_shapes)
    ins = [pltpu.with_memory_space_constraint(a, pltpu.HBM) for a in ins]

    def kbody(*refs):
        in_refs, out_refs, sc = refs[:n_in_all], refs[n_in_all:n_in_all + n_out_all], refs[n_in_all + n_out_all:]
        ids = tuple(pl.program_id(a) for a in range(len(grid)))
        if carry is not None:
            first = functools.reduce(jnp.logical_and, [i == 0 for i in ids])
            last = functools.reduce(jnp.logical_and, [i == g - 1 for i, g in zip(ids, grid)])
            comm = (in_refs[n_in:n_in + len(carry.srcs)], out_refs[n_out:], sc[n_sc], sc[n_sc + 1])

            @pl.when(first)
            def _():
                carry.start(*comm)

            if carry.middle is not None:
                step, total = ids[0], grid[0]
                for i, g in zip(ids[1:], grid[1:]):
                    step, total = step * g + i, total * g

                @pl.when(step == (2 * total) // 3)
                def _():
                    carry.middle(*comm)

        body(in_refs[:n_in], out_refs[:n_out], sc[:n_sc], ids)
        if carry is not None:
            @pl.when(last)
            def _():
                carry.finish(*comm)

    return pl.pallas_call(
        kbody, name=name, grid=grid, in_specs=in_specs, out_specs=out_specs,
        out_shape=[pltpu.HBM(o.shape, o.dtype) for o in out_shapes],
        scratch_shapes=scratch, input_output_aliases=aliases,
        compiler_params=pltpu.CompilerParams(dimension_semantics=("arbitrary",) * len(grid),
                                             vmem_limit_bytes=vmem_mb * MIB),
    )(*ins)


def _matmul(name, grid, kaxis, ins, in_specs, out_shapes, out_specs, dims, epi, tail=None, acc_shape=None, nc=512,
            vmem_mb=48, carry=None):
    nk = grid[kaxis] if kaxis is not None else 1

    def body(in_refs, out_refs, scratch, ids):
        a = in_refs[0][...]
        b_ref = in_refs[1]
        n = b_ref.shape[0] if dims == NT else b_ref.shape[1]

        def prod(c0):
            b = b_ref[c0:c0 + nc, :] if dims == NT else b_ref[:, c0:c0 + nc]
            return lax.dot_general(a, b, (dims, ((), ())), preferred_element_type=F32)

        if kaxis is None:
            for c0 in range(0, n, nc):
                epi(prod(c0), slice(c0, c0 + nc), ids, in_refs[2:], out_refs)
            if tail is not None:
                tail(ids, in_refs[2:], out_refs)
        else:
            acc = scratch[0]
            _zero_when(ids[kaxis] == 0, acc)
            for c0 in range(0, n, nc):
                acc[:, c0:c0 + nc] += prod(c0)

            @pl.when(ids[kaxis] == nk - 1)
            def _():
                epi(acc, ids, in_refs[2:], out_refs)

    return _call(name, grid, body, ins, in_specs, out_shapes, out_specs,
                 [pltpu.VMEM(acc_shape, F32)] if kaxis is not None else [], vmem_mb, carry)


def _row_rsqrt(xf):
    return lax.rsqrt(jnp.mean(xf * xf, axis=-1, keepdims=True) + EPS)


def _norm_bwd(dh, xf, g, resid):
    r = _row_rsqrt(xf)
    xh = xf * r
    dg = jnp.sum(dh * xh, axis=0, keepdims=True)
    dxh = dh * g
    dx = resid + r * (dxh - xh * jnp.mean(dxh * xh, axis=-1, keepdims=True))
    return dx, dg


def _for_rows(n, fn, sub=128):
    def step(q, carry):
        fn(pl.ds(pl.multiple_of(q * sub, sub), sub))
        return carry

    lax.fori_loop(0, n // sub, step, 0)


def _zero_when(first, *refs):
    @pl.when(first)
    def _():
        for ref in refs:
            ref[...] = jnp.zeros_like(ref)


_GELU_K = 0.7978845608028654
_GELU_C = 0.044715


def _gelu(x):
    t = jnp.tanh(_GELU_K * (x + _GELU_C * x * x * x))
    return 0.5 * x * (1.0 + t)


def _gelu_and_grad(x):
    x2 = x * x
    t = jnp.tanh(_GELU_K * (x + _GELU_C * x * x2))
    g = 0.5 * x * (1.0 + t)
    dg = 0.5 * (1.0 + t) + 0.5 * x * (1.0 - t * t) * (_GELU_K * (1.0 + 3.0 * _GELU_C * x2))
    return g, dg


def _window_sum(ext, w, causal):
    n = ext.shape[0]
    s, d = ext, 1
    while d < w:
        s = s + pltpu.roll(s, d if causal else n - d, 0)
        d *= 2
    return s


def _inv_count(t, w):
    return 1.0 / jnp.minimum(t + 1, w).astype(F32)


def _pooled(z_ref, zh_ref, g, w, i, tm):
    cols = slice(GD * g, GD * (g + 1))
    zb = z_ref[:, cols]
    zh = jnp.where(i > 0, zh_ref[:, cols], 0.0)
    ext = jnp.concatenate([zh, zb], axis=0)
    s = _window_sum(ext, w, True)[HALO:, :]
    t = i * tm + lax.broadcasted_iota(jnp.int32, (tm, 1), 0)
    return s * _inv_count(t, w) - zb


def _full(shape, n_axes=1):
    return pl.BlockSpec(shape, lambda *ids: (0,) * len(shape))


def _norm1(x, g, tm, carry):
    s = x.shape[0]

    def body(ins, outs, scratch, ids):
        xf = ins[0][...]
        outs[0][...] = (xf * _row_rsqrt(xf) * ins[1][...]).astype(BF16)

    row = pl.BlockSpec((tm, D), lambda i: (i, 0))
    return _call("norm1", (s // tm,), body, [x, g], [row, _full((1, D))],
                 [jax.ShapeDtypeStruct((s, D), BF16)], [row], vmem_mb=32, carry=carry)


def _mixer_fwd(proj, wt, bst, gv, wpool, scale, tm, carry):
    s = proj.shape[0]
    nq = tm // HD

    def body(ins, outs, scratch, ids):
        up_ref, vp_ref, z_ref, zh_ref, wt_ref, bst_ref, gv_ref, wp_ref, sc_ref = ins
        out_ref = outs[0]
        i = ids[0]
        for h in range(HEADS):
            cols = slice(HD * h, HD * (h + 1))
            gvh = gv_ref[:, cols]
            bcol = bst_ref[h]
            wth = wt_ref[h]

            def chunk(q, c_):
                rows = pl.ds(pl.multiple_of(q * HD, HD), HD)
                u = _gelu(up_ref[rows, cols])
                v = _gelu(vp_ref[rows, cols])
                vh = (v * _row_rsqrt(v) * gvh).astype(BF16)
                mixed = jnp.dot(wth, vh, preferred_element_type=F32) + bcol
                out_ref[rows, cols] = (u * mixed).astype(BF16)
                return c_

            lax.fori_loop(0, nq, chunk, 0, unroll=True)
        for g, w in enumerate(POOL_WINDOWS):
            cols = slice(GD * g, GD * (g + 1))
            pooled = _pooled(z_ref, zh_ref, g, w, i, tm)
            yv = jnp.dot(pooled.astype(BF16), wp_ref[g], preferred_element_type=F32)
            out_ref[:, A_W + GD * g:A_W + GD * (g + 1)] = (yv * sc_ref[:, cols]).astype(BF16)

    hb = tm // HALO
    return _call(
        "mixer_fwd", (s // tm,), body, [proj, proj, proj, proj, wt, bst, gv, wpool, scale],
        [pl.BlockSpec((tm, A_W), lambda i: (i, 0)),
         pl.BlockSpec((tm, A_W), lambda i: (i, 1)),
         pl.BlockSpec((tm, A_W), lambda i: (i, 2)),
         pl.BlockSpec((HALO, A_W), lambda i: (jnp.maximum(i * hb - 1, 0), 2)),
         _full((HEADS, HD, HD)), _full((HEADS, HD, HD)), _full((1, A_W)), _full((4, GD, GD)), _full((1, A_W))],
        [jax.ShapeDtypeStruct((s, D), BF16)], [pl.BlockSpec((tm, D), lambda i: (i, 0))], vmem_mb=40, carry=carry)


def _mixer_bwd(proj, dmix, wt, wtt, bst, gv, wpool, scale, tm, carry):
    s = proj.shape[0]
    nb = s // tm
    nq = tm // HD
    hb = tm // HALO

    def body(ins, outs, scratch, ids):
        (up_ref, vp_ref, z_ref, zh_ref, doa_ref, dob_ref, dobh_ref, wt_ref, wtt_ref, bst_ref, gv_ref, wp_ref,
         sc_ref) = ins
        dproj_ref, dws_ref, dbs_ref, dgv_ref, dwp_ref, dsc_ref = outs
        dbfull = scratch[0]
        i = ids[0]
        last = i == nb - 1
        _zero_when(i == 0, dws_ref, dbfull, dgv_ref, dwp_ref, dsc_ref)

        for h in range(HEADS):
            cols = slice(HD * h, HD * (h + 1))
            gvh = gv_ref[:, cols]
            bcol = bst_ref[h]
            wth = wt_ref[h]
            wtth = wtt_ref[h]

            def chunk(q, c_):
                rows = pl.ds(pl.multiple_of(q * HD, HD), HD)
                u, du_dup = _gelu_and_grad(up_ref[rows, cols])
                v, dv_dvp = _gelu_and_grad(vp_ref[rows, cols])
                rv = _row_rsqrt(v)
                vn = v * rv
                vh = (vn * gvh).astype(BF16)
                mixed = jnp.dot(wth, vh, preferred_element_type=F32) + bcol
                doa = doa_ref[rows, cols].astype(F32)
                dmx = doa * u
                dmxb = dmx.astype(BF16)
                dbfull[h] += dmx
                dws_ref[h] += lax.dot_general(dmxb, vh, (NT, ((), ())), preferred_element_type=F32)
                dvh = jnp.dot(wtth, dmxb, preferred_element_type=F32)
                dgv_ref[:, cols] += jnp.sum(dvh * vn, axis=0, keepdims=True)
                dvn = dvh * gvh
                dv = rv * (dvn - vn * jnp.mean(dvn * vn, axis=-1, keepdims=True))
                dproj_ref[rows, cols] = (doa * mixed * du_dup).astype(BF16)
                dproj_ref[rows, A_W + HD * h:A_W + HD * (h + 1)] = (dv * dv_dvp).astype(BF16)
                return c_

            lax.fori_loop(0, nq, chunk, 0, unroll=True)

        t = i * tm + lax.broadcasted_iota(jnp.int32, (tm, 1), 0)
        th = (i + 1) * tm + lax.broadcasted_iota(jnp.int32, (HALO, 1), 0)
        for g, w in enumerate(POOL_WINDOWS):
            cols = slice(GD * g, GD * (g + 1))
            wpg = wp_ref[g]
            scg = sc_ref[:, cols]
            pb = _pooled(z_ref, zh_ref, g, w, i, tm).astype(BF16)
            ypre = jnp.dot(pb, wpg, preferred_element_type=F32)
            dob = dob_ref[:, cols].astype(F32)
            dsc_ref[:, cols] += jnp.sum(dob * ypre, axis=0, keepdims=True)
            dyb = (dob * scg).astype(BF16)
            dwp_ref[g] += lax.dot_general(pb, dyb, (TN, ((), ())), preferred_element_type=F32)
            dpo = lax.dot_general(dyb, wpg, (NT, ((), ())), preferred_element_type=F32)
            dyh = (dobh_ref[:, cols].astype(F32) * scg).astype(BF16)
            dpoh = lax.dot_general(dyh, wpg, (NT, ((), ())), preferred_element_type=F32)
            dpoh = jnp.where(last, 0.0, dpoh * _inv_count(th, w))
            ext = jnp.concatenate([dpo * _inv_count(t, w), dpoh], axis=0)
            dz = _window_sum(ext, w, False)[:tm, :] - dpo
            dproj_ref[:, 2 * A_W + GD * g:2 * A_W + GD * (g + 1)] = dz.astype(BF16)

        @pl.when(last)
        def _():
            r = lax.broadcasted_iota(jnp.int32, (HD, HD), 0)
            c = lax.broadcasted_iota(jnp.int32, (HD, HD), 1)
            for h in range(HEADS):
                dws_ref[h] = jnp.where(r >= c, dws_ref[h], 0.0)
                dbs_ref[h] = jnp.sum(dbfull[h], axis=-1, keepdims=True)

    return _call(
        "mixer_bwd", (nb,), body, [proj, proj, proj, proj, dmix, dmix, dmix, wt, wtt, bst, gv, wpool, scale],
        [pl.BlockSpec((tm, A_W), lambda i: (i, 0)),
         pl.BlockSpec((tm, A_W), lambda i: (i, 1)),
         pl.BlockSpec((tm, A_W), lambda i: (i, 2)),
         pl.BlockSpec((HALO, A_W), lambda i: (jnp.maximum(i * hb - 1, 0), 2)),
         pl.BlockSpec((tm, A_W), lambda i: (i, 0)),
         pl.BlockSpec((tm, A_W), lambda i: (i, 1)),
         pl.BlockSpec((HALO, A_W), lambda i: (jnp.minimum((i + 1) * hb, s // HALO - 1), 1)),
         _full((HEADS, HD, HD)), _full((HEADS, HD, HD)), _full((HEADS, HD, HD)), _full((1, A_W)),
         _full((4, GD, GD)), _full((1, A_W))],
        [jax.ShapeDtypeStruct((s, IN_W), BF16),
         jax.ShapeDtypeStruct((HEADS, HD, HD), F32),
         jax.ShapeDtypeStruct((HEADS, HD, 1), F32),
         jax.ShapeDtypeStruct((1, A_W), F32),
         jax.ShapeDtypeStruct((4, GD, GD), F32),
         jax.ShapeDtypeStruct((1, A_W), F32)],
        [pl.BlockSpec((tm, IN_W), lambda i: (i, 0)),
         _full((HEADS, HD, HD)), _full((HEADS, HD, 1)), _full((1, A_W)), _full((4, GD, GD)), _full((1, A_W))],
        [pltpu.VMEM((HEADS, HD, HD), F32)], vmem_mb=48, carry=carry)


def _inproj(h1, win_g, tm, carry):
    s = h1.shape[0]
    cw = IN_W // N_CHIPS

    def epi(p, cols, ids, extra, outs):
        outs[0][:, cols] = p

    return _matmul(
        "inproj", (N_CHIPS, s // tm), None, [h1, win_g],
        [pl.BlockSpec((tm, D), lambda j, i: (i, 0)), pl.BlockSpec((None, D, cw), lambda j, i: (j, 0, 0))],
        [jax.ShapeDtypeStruct((s, IN_W), F32)], [pl.BlockSpec((tm, cw), lambda j, i: (i, j))], NN, epi,
        nc=256, vmem_mb=32, carry=carry)


def _outproj(mixed, wout, x, g_ffn, tm, carry):
    s = x.shape[0]

    def epi(p, cols, ids, extra, outs):
        outs[0][:, cols] = extra[0][:, cols] + p

    def tail(ids, extra, outs):
        x2 = outs[0][...]
        outs[1][...] = (x2 * _row_rsqrt(x2) * extra[1][...]).astype(BF16)

    row = pl.BlockSpec((tm, D), lambda i: (i, 0))
    return _matmul(
        "outproj", (s // tm,), None, [mixed, wout, x, g_ffn], [row, _full((D, D)), row, _full((1, D))],
        [jax.ShapeDtypeStruct((s, D), F32), jax.ShapeDtypeStruct((s, D), BF16)], [row, row], NN, epi, tail,
        vmem_mb=48, carry=carry)


def _up(h2, wup_g, tm, tn, carry):
    s = h2.shape[0]
    per = D // tn

    def epi(p, cols, ids, extra, outs):
        a = jnp.maximum(p, 0.0)
        outs[0][:, cols] = (a * a).astype(BF16)

    return _matmul(
        "up", (D_FF // tn, s // tm), None, [h2, wup_g],
        [pl.BlockSpec((tm, D), lambda j, i: (i, 0)), pl.BlockSpec((None, D, tn), lambda j, i: (j // per, 0, j % per))],
        [jax.ShapeDtypeStruct((s, D_FF), BF16)], [pl.BlockSpec((tm, tn), lambda j, i: (i, j))], NN, epi,
        vmem_mb=40, carry=carry)


def _down(act, wdown, x2, tgt, g_final, tm, tk):
    s = x2.shape[0]

    def epi(acc, ids, extra, outs):
        x2_ref, t_ref, g_ref = extra
        dxb_ref, dgf_ref, loss_ref = outs
        g = g_ref[...]
        _zero_when(ids[0] == 0, dgf_ref, loss_ref)

        def block(rows):
            x3 = x2_ref[rows, :] + acc[rows, :]
            r = _row_rsqrt(x3)
            xh = x3 * r
            diff = xh * g - t_ref[rows, :]
            dy = diff * (1.0 / D)
            dxh = dy * g
            dx = r * (dxh - xh * jnp.mean(dxh * xh, axis=-1, keepdims=True))
            dxb_ref[rows, :] = dx.astype(BF16)
            dgf_ref[...] += jnp.sum(dy * xh, axis=0, keepdims=True)
            loss_ref[...] += jnp.sum(diff * diff, axis=0, keepdims=True)

        _for_rows(tm, block)

    row = pl.BlockSpec((tm, D), lambda i, k: (i, 0))
    vec = pl.BlockSpec((1, D), lambda i, k: (0, 0))
    return _matmul(
        "down", (s // tm, D_FF // tk), 1, [act, wdown, x2, tgt, g_final],
        [pl.BlockSpec((tm, tk), lambda i, k: (i, k)), pl.BlockSpec((tk, D), lambda i, k: (k, 0)), row, row, vec],
        [jax.ShapeDtypeStruct((s, D), BF16), jax.ShapeDtypeStruct((1, D), F32), jax.ShapeDtypeStruct((1, D), F32)],
        [row, vec, vec], NN, epi, acc_shape=(tm, D), vmem_mb=56)


def _dact(dx3b, wdown, act, tm, tn, carry):
    s = dx3b.shape[0]

    def epi(p, cols, ids, extra, outs):
        outs[0][:, cols] = (p * (2.0 * jnp.sqrt(extra[0][:, cols].astype(F32)))).astype(BF16)

    tile = pl.BlockSpec((tm, tn), lambda j, i: (i, j))
    return _matmul(
        "dact", (D_FF // tn, s // tm), None, [dx3b, wdown, act],
        [pl.BlockSpec((tm, D), lambda j, i: (i, 0)), pl.BlockSpec((tn, D), lambda j, i: (j, 0)), tile],
        [jax.ShapeDtypeStruct((s, D_FF), BF16)], [tile], NT, epi, vmem_mb=40, carry=carry)


def _dweight(name, lhs, rhs, n_shards, rows, cols, tm, nc=512, carry=None):
    s = lhs.shape[0]

    def epi(p, cs, ids, extra, outs):
        outs[0][:, cs] = p.astype(BF16)

    return _matmul(
        name, (n_shards, rows // tm), None, [lhs, rhs],
        [pl.BlockSpec((s, tm), lambda j, i: (0, i)), pl.BlockSpec((s, cols), lambda j, i: (0, j))],
        [jax.ShapeDtypeStruct((n_shards, rows, cols), BF16)],
        [pl.BlockSpec((None, tm, cols), lambda j, i: (j, i, 0))], TN, epi, nc=nc, vmem_mb=56, carry=carry)


def _dh2(da, wup_g, x2, dx3, g_ffn, tm, tk, carry):
    s = x2.shape[0]
    per = D // tk

    def epi(acc, ids, extra, outs):
        x2_ref, dx3_ref, g_ref = extra
        g = g_ref[...]
        _zero_when(ids[0] == 0, outs[1])

        def block(rows):
            dx, dg = _norm_bwd(acc[rows, :], x2_ref[rows, :], g, dx3_ref[rows, :].astype(F32))
            outs[0][rows, :] = dx.astype(BF16)
            outs[1][...] += dg

        _for_rows(tm, block)

    row = pl.BlockSpec((tm, D), lambda i, k: (i, 0))
    vec = pl.BlockSpec((1, D), lambda i, k: (0, 0))
    return _matmul(
        "dh2", (s // tm, D_FF // tk), 1, [da, wup_g, x2, dx3, g_ffn],
        [pl.BlockSpec((tm, tk), lambda i, k: (i, k)),
         pl.BlockSpec((None, D, tk), lambda i, k: (k // per, 0, k % per)), row, row, vec],
        [jax.ShapeDtypeStruct((s, D), BF16), jax.ShapeDtypeStruct((1, D), F32)],
        [row, vec], NT, epi, acc_shape=(tm, D), vmem_mb=56, carry=carry)


def _dmixed(dx2b, wout, tm, carry):
    s = dx2b.shape[0]

    def epi(p, cols, ids, extra, outs):
        outs[0][:, cols] = p.astype(BF16)

    row = pl.BlockSpec((tm, D), lambda i: (i, 0))
    return _matmul(
        "dmixed", (s // tm,), None, [dx2b, wout], [row, _full((D, D))],
        [jax.ShapeDtypeStruct((s, D), BF16)], [row], NT, epi, vmem_mb=40, carry=carry)


def _dh1(dproj, win_g, x, dx2, g_mix, tm, carry):
    s = x.shape[0]
    cw = IN_W // N_CHIPS

    def epi(acc, ids, extra, outs):
        x_ref, dx2_ref, g_ref = extra
        g = g_ref[...]
        _zero_when(ids[0] == 0, outs[1])

        def block(rows):
            dx, dg = _norm_bwd(acc[rows, :], x_ref[rows, :], g, dx2_ref[rows, :].astype(F32))
            outs[0][rows, :] = dx
            outs[1][...] += dg

        _for_rows(tm, block)

    row = pl.BlockSpec((tm, D), lambda i, j: (i, 0))
    vec = pl.BlockSpec((1, D), lambda i, j: (0, 0))
    return _matmul(
        "dh1", (s // tm, N_CHIPS), 1, [dproj, win_g, x, dx2, g_mix],
        [pl.BlockSpec((tm, cw), lambda i, j: (i, j)), pl.BlockSpec((None, D, cw), lambda i, j: (j, 0, 0)),
         row, row, vec],
        [jax.ShapeDtypeStruct((s, D), F32), jax.ShapeDtypeStruct((1, D), F32)],
        [row, vec], NT, epi, acc_shape=(tm, D), vmem_mb=52, carry=carry)


def _cast_place(name, w, tr):
    rows, cols = w.shape

    def body(ins, outs, scratch, ids):
        outs[0][...] = ins[0][...].astype(BF16)

    return _call(name, (rows // tr,), body, [w], [pl.BlockSpec((tr, cols), lambda r: (r, 0))],
                 [jax.ShapeDtypeStruct((N_CHIPS, rows, cols), BF16)],
                 [pl.BlockSpec((None, tr, cols), lambda r: (_my_chip(), r, 0))], vmem_mb=32)[0]


def _small_allreduce(part):
    rows = part.shape[0]
    rh = rows // 2

    def body(p_ref, o_ref, sib_ref, slots, send_sems, recv_sems):
        x, y, c = _place()
        k = 2 * x + y
        sib = (x, y, 1 - c)
        half = pl.ds(pl.multiple_of(c * rh, 8), rh)
        cp = _remote(p_ref, sib_ref, send_sems.at[0], recv_sems.at[0], sib)
        cp.start()
        cp.wait()
        slots[k] = p_ref[half, :] + sib_ref[half, :]
        cps = []
        for j, (cx, cy) in enumerate(_other_chips(x, y)):
            cp = _remote(slots.at[k], slots.at[k], send_sems.at[1 + j], recv_sems.at[1 + j], (cx, cy, c))
            cp.start()
            cps.append(cp)
        for cp in cps:
            cp.wait()
        o_ref[half, :] = ((slots[0] + slots[1]) + slots[2]) + slots[3]
        cp = _remote(o_ref.at[half], o_ref.at[half], send_sems.at[4], recv_sems.at[4], sib)
        cp.start()
        cp.wait()

    vm = pl.BlockSpec(memory_space=pltpu.VMEM)
    return pl.pallas_call(
        body, name="small_allreduce", in_specs=[vm], out_specs=vm,
        out_shape=jax.ShapeDtypeStruct(part.shape, F32),
        scratch_shapes=[pltpu.VMEM(part.shape, F32), pltpu.VMEM((N_CHIPS, rh, LANES), F32),
                        pltpu.SemaphoreType.DMA((5,)), pltpu.SemaphoreType.DMA((5,))],
        compiler_params=pltpu.CompilerParams(vmem_limit_bytes=32 * MIB),
    )(part)


def _comm_only(name, carry):
    ns, nl = len(carry.srcs), len(carry.lands)
    lands_in = [l for l in carry.lands if not isinstance(l, jax.ShapeDtypeStruct)]
    assert len(lands_in) in (0, nl)

    def body(*refs):
        srcs = refs[:ns]
        lands = refs[ns + len(lands_in):ns + len(lands_in) + nl]
        ssem, rsem = refs[ns + len(lands_in) + nl:]
        carry.start(srcs, lands, ssem, rsem)
        if carry.middle is not None:
            carry.middle(srcs, lands, ssem, rsem)
        carry.finish(srcs, lands, ssem, rsem)

    return pl.pallas_call(
        body, name=name, in_specs=[ANY] * (ns + len(lands_in)), out_specs=[ANY] * nl,
        out_shape=[jax.ShapeDtypeStruct(l.shape, l.dtype) for l in carry.lands],
        input_output_aliases={ns + i: i for i in range(len(lands_in))},
        scratch_shapes=[pltpu.SemaphoreType.DMA((carry.n_sems,)), pltpu.SemaphoreType.DMA((carry.n_sems,))],
    )(*carry.srcs, *lands_in)


def _share_carry(grads):
    n = len(grads)

    def copies(srcs, lands, ssem, rsem):
        x, y, c = _place()
        return [_remote(lands[w].at[c], lands[w].at[c], ssem.at[w], rsem.at[w], (x, y, 1 - c)) for w in range(n)]

    def start(*a):
        for cp in copies(*a):
            cp.start()

    def finish(*a):
        for cp in copies(*a):
            cp.wait()

    return _Carry([], grads, n, start, finish)


def _add_sibling(dw, got, tr):
    _, _, rh, cols = dw.shape

    def body(ins, outs, scratch, ids):
        outs[0][...] = (ins[0][...].astype(F32) + ins[1][...].astype(F32)).astype(BF16)

    blk = pl.BlockSpec((None, tr, cols), lambda j, r: (j, r, 0))
    return _call("add_sibling", (N_CHIPS, rh // tr), body, [dw, got],
                 [pl.BlockSpec((None, None, tr, cols), lambda j, r: (j, _my_core(), r, 0)), blk],
                 [jax.ShapeDtypeStruct((N_CHIPS, rh, cols), BF16)], [blk], vmem_mb=32)[0]


def _add_chips(sums, got, tr):
    _, rh, cols = sums.shape

    def body(ins, outs, scratch, ids):
        b = ins[1][...].astype(F32)
        outs[0][...] = ((ins[0][...].astype(F32) + b[0]) + b[1]) + b[2]

    return _call("add_chips", (rh // tr,), body, [sums, got],
                 [pl.BlockSpec((None, tr, cols), lambda r: (_my_chip(), r, 0)),
                  pl.BlockSpec((3, tr, cols), lambda r: (0, r, 0))],
                 [jax.ShapeDtypeStruct((2, rh, cols), F32)],
                 [pl.BlockSpec((None, tr, cols), lambda r: (_my_core(), r, 0))], vmem_mb=32)[0]


def _adamw_math(w, g, m, v):
    m = ADAM_B1 * m + (1.0 - ADAM_B1) * g
    v = ADAM_B2 * v + (1.0 - ADAM_B2) * (g * g)
    m_hat = m / (1.0 - ADAM_B1 ** ADAM_STEP)
    v_hat = v / (1.0 - ADAM_B2 ** ADAM_STEP)
    delta = -ADAM_LR * (m_hat / (jnp.sqrt(v_hat) + ADAM_EPS) + ADAM_WD * w)
    return delta, m, v


def _adamw(name, w, g, m, v, tr, carry=None):
    rows, cols = w.shape

    def body(ins, outs, scratch, ids):
        g_val = ins[1][...]
        outs[0][...] = g_val
        outs[1][...], outs[2][...], outs[3][...] = _adamw_math(ins[0][...], g_val, ins[2][...], ins[3][...])

    blk = pl.BlockSpec((tr, cols), lambda r: (r, 0))
    return _call(name, (rows // tr,), body, [w, g, m, v], [blk] * 4,
                 [jax.ShapeDtypeStruct(w.shape, F32)] * 4, [blk] * 4, vmem_mb=40, carry=carry)


def _rows(a):
    return a.reshape(-1, LANES)


def kernel(x, g_mix, w_in, g_v, w_s, b_s, w_pool, pool_scale, w_out, g_ffn, w_up, w_down, g_final, loss_target, m_g_mix, m_w_in, m_g_v, m_w_s, m_b_s, m_w_pool, m_pool_scale, m_w_out, m_g_ffn, m_w_up, m_w_down, m_g_final, v_g_mix, v_w_in, v_g_v, v_w_s, v_b_s, v_w_pool, v_pool_scale, v_w_out, v_g_ffn, v_w_up, v_w_down, v_g_final):
    tm = 512
    xs = x[0]
    tgt = loss_target[0]
    chip = _my_chip()

    win_g = _cast_place("cast_w_in", w_in[0], 256)
    wpool_g = _cast_place("cast_w_pool", w_pool[0].reshape(4 * 64, GD), 128)
    wout_g = _cast_place("cast_w_out", w_out[0], 128)
    wup_g = _cast_place("cast_w_up", w_up[0], 256)
    wdown_g = _cast_place("cast_w_down", w_down[0], 256)
    h1, win_g, wpool_g = _norm1(xs, g_mix, tm, _gather_whole([win_g, wpool_g]))
    wpool_f = wpool_g.reshape(N_CHIPS, 4, 64, GD).transpose(1, 0, 2, 3).reshape(4, GD, GD)
    tril = jnp.tril(jnp.ones((HD, HD), dtype=bool))
    wt = jnp.where(tril[None], w_s[0], 0.0).astype(BF16)
    wtt = wt.transpose(0, 2, 1)
    bst = jnp.broadcast_to(b_s[0][:, :, None], (HEADS, HD, HD))
    gfin = g_final.reshape(1, D)

    proj, wout_g = _inproj(h1, win_g, tm, _gather_whole([wout_g]))
    mixed, wup_g = _mixer_fwd(proj, wt, bst, g_v, wpool_f, pool_scale, tm, _gather_near([wup_g]))
    wout_f = wout_g.reshape(D, D)
    x2, h2, wup_g = _outproj(mixed, wout_f, xs, g_ffn, 256, _gather_far([wup_g]))
    act, wdown_g = _up(h2, wup_g, tm, 2048, _gather_whole([wdown_g]))
    wdown_f = wdown_g.reshape(D_FF, D)
    dx3b, dgf, lossv = _down(act, wdown_f, x2, tgt, gfin, tm, 2048)

    halves = lambda dw, rows, cols: dw.reshape(N_CHIPS, 2, rows // (2 * N_CHIPS), cols)
    cw = IN_W // N_CHIPS
    dwdown = halves(_dweight("dw_down", act, dx3b, 1, D_FF, D, 512)[0], D_FF, D)
    da, sib_down = _dact(dx3b, wdown_f, act, tm, 2048, _sibling_carry(dwdown))
    sum_down = _add_sibling(dwdown, sib_down, 256)
    dwup, got_down = _dweight("dw_up", h2, da, N_CHIPS, D, D, 512, carry=_chips_carry(sum_down, None, (0, 3, 4)))
    dwup = halves(dwup, D_FF, D)
    dx2b, dgffn, got_down, sib_up = _dh2(da, wup_g, x2, dx3b, g_ffn, tm, 2048,
                                         _join(_chips_carry(sum_down, got_down, (3, 4, 4)), _sibling_carry(dwup)))
    half_down = _add_chips(sum_down, got_down, 256)
    sum_up = _add_sibling(dwup, sib_up, 256)
    dwout, got_up = _dweight("dw_out", mixed, dx2b, 1, D, D, 512, carry=_chips_carry(sum_up, None, (0, 1, 4)))
    dwout = halves(dwout, D, D)
    dmix, got_up, sib_out, half_down = _dmixed(
        dx2b, wout_f, 256,
        _join(_chips_carry(sum_up, got_up, (1, 2, 4)), _sibling_carry(dwout), _share_carry([half_down])))
    sum_out = _add_sibling(dwout, sib_out, 256)
    g_down, d_down, nm_down, nv_down = _adamw("adamw_down", w_down[0], half_down.reshape(D, D), m_w_down[0],
                                              v_w_down[0], 128)
    dproj, dws, dbs, dgv, dwp, dsc, got_up = _mixer_bwd(proj, dmix, wt, wtt, bst, g_v, wpool_f, pool_scale, tm,
                                                        _chips_carry(sum_up, got_up, (2, 4, 4)))
    half_up = _add_chips(sum_up, got_up, 256)
    dwin, got_out, half_up = _dweight("dw_in", h1, dproj, N_CHIPS, D, cw, 512, nc=256,
                                      carry=_join(_chips_carry(sum_out, None), _share_carry([half_up])))
    dwin = halves(dwin, N_CHIPS * D, cw)
    sib_in = _comm_only("sibling_in", _sibling_carry(dwin))[0]
    sum_in = _add_sibling(dwin, sib_in, 256)
    g_up, d_up, nm_up, nv_up = _adamw("adamw_up", w_up[0], half_up.reshape(D, D), m_w_up[0], v_w_up[0], 128)
    half_out = _add_chips(sum_out, got_out, 256)
    grad_x, dgmix, got_in, half_out = _dh1(dproj, win_g, xs, dx2b, g_mix, tm,
                                           _join(_chips_carry(sum_in, None), _share_carry([half_out])))
    half_in = _add_chips(sum_in, got_in, 256)
    half_in = _comm_only("share_half_in", _share_carry([half_in]))[0]
    g_out, d_out, nm_out, nv_out = _adamw("adamw_out", w_out[0], half_out.reshape(D // N_CHIPS, D), m_w_out[0],
                                          v_w_out[0], 128)
    g_in, d_in, nm_in, nv_in = _adamw("adamw_in", w_in[0], half_in.reshape(D, cw), m_w_in[0], v_w_in[0], 128)

    pieces = [dgmix, dgv, dws, dbs, dwp, dsc, dgffn, dgf, lossv, jnp.zeros((8 * LANES,), F32)]
    sizes = [p.size // LANES for p in pieces]
    tot = _small_allreduce(jnp.concatenate([_rows(p) for p in pieces], axis=0))
    offs = [sum(sizes[:i]) for i in range(len(sizes))]
    take = lambda i: tot[offs[i]:offs[i] + sizes[i]]
    s_gmix, s_gv, s_ws, s_bs, s_wp, s_sc, s_gffn, s_gf = [take(i) for i in range(8)]
    loss = (0.5 / D) * jnp.sum(take(8))
    s_wp_mine = lax.dynamic_slice_in_dim(s_wp.reshape(4, GD, GD), chip * 64, 64, axis=1)
    small_g = [s_gmix, s_gv, s_ws, s_bs, _rows(s_wp_mine), s_sc, s_gffn, s_gf]
    small_w = [g_mix, g_v, w_s, b_s, w_pool, pool_scale, g_ffn, g_final]
    small_m = [m_g_mix, m_g_v, m_w_s, m_b_s, m_w_pool, m_pool_scale, m_g_ffn, m_g_final]
    small_v = [v_g_mix, v_g_v, v_w_s, v_b_s, v_w_pool, v_pool_scale, v_g_ffn, v_g_final]
    cat = lambda parts: jnp.concatenate([_rows(p) for p in parts], axis=0)
    sg = cat(small_g)
    sg, sd, snm, snv = _adamw("adamw_small", cat(small_w), sg, cat(small_m), cat(small_v), sg.shape[0])
    ssz = [p.size // LANES for p in small_w]
    soff = [sum(ssz[:i]) for i in range(len(ssz))]
    split = lambda a: [a[soff[i]:soff[i] + ssz[i]].reshape(small_w[i].shape) for i in range(len(ssz))]
    gs, ds, nms, nvs = split(sg), split(sd), split(snm), split(snv)

    def ordered(small, w_in_, w_out_, w_up_, w_down_):
        return [small[0], w_in_[None], small[1], small[2], small[3], small[4], small[5], w_out_[None], small[6],
                w_up_[None], w_down_[None], small[7]]

    return (loss, grad_x[None],
            *ordered(gs, g_in, g_out, g_up, g_down),
            *ordered(ds, d_in, d_out, d_up, d_down),
            *ordered(nms, nm_in, nm_out, nm_up, nm_down),
            *ordered(nvs, nv_in, nv_out, nv_up, nv_down))
```

```python
import functools

import jax
import jax.numpy as jnp
from jax import lax
from jax.experimental import pallas as pl
from jax.experimental.pallas import tpu as pltpu

F32 = jnp.float32
BF16 = jnp.bfloat16
EPS = 1e-6
D = 2048
A_W = 1024
HEADS = 8
HD = 128
POOL_WINDOWS = (2, 4, 8, 16)
GD = 256
IN_W = 3072
D_FF = 8192
N_CHIPS = 4
HALO = 16
EPI_ROWS = 128
LANES = 128
MIB = 2 ** 20

ADAM_LR, ADAM_B1, ADAM_B2, ADAM_EPS, ADAM_WD, ADAM_STEP = 0.001, 0.9, 0.999, 1e-08, 0.01, 10

ANY = pl.BlockSpec(memory_space=pl.ANY)
MESH = pl.DeviceIdType.MESH

NN = ((1,), (0,))
NT = ((1,), (1,))
TN = ((0,), (0,))


def _place():
    return lax.axis_index("x"), lax.axis_index("y"), lax.axis_index("c")


def _my_chip():
    return 2 * lax.axis_index("x") + lax.axis_index("y")


def _my_core():
    return lax.axis_index("c")


def _other_chips(x, y):
    return [(1 - x, y), (x, 1 - y), (1 - x, 1 - y)]


def _remote(src, dst, send_sem, recv_sem, dev):
    return pltpu.make_async_remote_copy(src_ref=src, dst_ref=dst, send_sem=send_sem, recv_sem=recv_sem,
                                        device_id=dev, device_id_type=MESH)


class _SemView:
    def __init__(self, sems, base):
        self.sems, self.base = sems, base

    @property
    def at(self):
        return self

    def __getitem__(self, i):
        return self.sems.at[self.base + i]


class _Carry:
    def __init__(self, srcs, lands, n_sems, start, finish, middle=None):
        self.srcs, self.lands, self.n_sems, self.start, self.finish = list(srcs), list(lands), n_sems, start, finish
        self.middle = middle


def _join(*carries):
    def run(which):
        def go(srcs, lands, ssem, rsem):
            so = lo = qo = 0
            for c in carries:
                if getattr(c, which) is not None:
                    getattr(c, which)(srcs[so:so + len(c.srcs)], lands[lo:lo + len(c.lands)],
                                      _SemView(ssem, qo), _SemView(rsem, qo))
                so, lo, qo = so + len(c.srcs), lo + len(c.lands), qo + c.n_sems
        return go

    middle = run("middle") if any(c.middle is not None for c in carries) else None
    return _Carry([s for c in carries for s in c.srcs], [l for c in carries for l in c.lands],
                  sum(c.n_sems for c in carries), run("start"), run("finish"), middle)


GATHER_SEMS = 7


def _gather_copies(lands, ssem, rsem):
    x, y, c = _place()
    k, kx, ky, kd = 2 * x + y, 2 * (1 - x) + y, 2 * x + (1 - y), 2 * (1 - x) + (1 - y)
    to_x, to_y, sib = (1 - x, y, c), (x, 1 - y, c), (x, y, 1 - c)
    out = []
    for w, land in enumerate(lands):
        rh = land.shape[1] // 2
        rq = rh // 2
        half = pl.ds(pl.multiple_of(c * rh, 16), rh)
        quarters = [pl.ds(pl.multiple_of(c * rh + q * rq, 16), rq) for q in range(2)]

        def cp(i, piece, dev, w=w):
            return _remote(piece, piece, ssem.at[GATHER_SEMS * w + i], rsem.at[GATHER_SEMS * w + i], dev)

        out.append(dict(
            ax=cp(0, land.at[k, half], to_x), ay=cp(1, land.at[k, half], to_y),
            rx=cp(2, land.at[kx, quarters[0]], to_y), ry=cp(3, land.at[ky, quarters[1]], to_x),
            fx=cp(4, land.at[kx, half], sib), fy=cp(5, land.at[ky, half], sib), fd=cp(6, land.at[kd, half], sib)))
    return out


def _gather_whole(gathered):
    def start(srcs, lands, ssem, rsem):
        for d in _gather_copies(lands, ssem, rsem):
            d["ax"].start()
            d["ay"].start()

    def middle(srcs, lands, ssem, rsem):
        for d in _gather_copies(lands, ssem, rsem):
            d["ax"].wait_recv()
            d["rx"].start()
            d["fx"].start()
            d["ay"].wait_recv()
            d["ry"].start()
            d["fy"].start()

    def finish(srcs, lands, ssem, rsem):
        for d in _gather_copies(lands, ssem, rsem):
            d["rx"].wait_recv()
            d["ry"].wait_recv()
            d["fd"].start()
            for name in ("fx", "fy", "fd"):
                d[name].wait_recv()
            for cp in d.values():
                cp.wait_send()

    return _Carry([], gathered, GATHER_SEMS * len(gathered), start, finish, middle)


def _gather_near(gathered):
    def start(srcs, lands, ssem, rsem):
        for d in _gather_copies(lands, ssem, rsem):
            d["ax"].start()
            d["ay"].start()

    def finish(srcs, lands, ssem, rsem):
        for d in _gather_copies(lands, ssem, rsem):
            for name in ("ax", "ay"):
                d[name].wait_recv()
                d[name].wait_send()

    return _Carry([], gathered, GATHER_SEMS * len(gathered), start, finish)


def _gather_far(gathered):
    def start(srcs, lands, ssem, rsem):
        for d in _gather_copies(lands, ssem, rsem):
            for name in ("rx", "ry", "fx", "fy"):
                d[name].start()

    def finish(srcs, lands, ssem, rsem):
        for d in _gather_copies(lands, ssem, rsem):
            d["rx"].wait_recv()
            d["ry"].wait_recv()
            d["fd"].start()
            for name in ("fx", "fy", "fd"):
                d[name].wait_recv()
            for name in ("rx", "ry", "fx", "fy", "fd"):
                d[name].wait_send()

    return _Carry([], gathered, GATHER_SEMS * len(gathered), start, finish)


def _sibling_carry(dw):
    def copies(srcs, lands, ssem, rsem):
        x, y, c = _place()
        return [_remote(srcs[0].at[j, 1 - c], lands[0].at[j], ssem.at[j], rsem.at[j], (x, y, 1 - c))
                for j in range(N_CHIPS)]

    def start(*a):
        for cp in copies(*a):
            cp.start()

    def finish(*a):
        for cp in copies(*a):
            cp.wait()

    return _Carry([dw], [jax.ShapeDtypeStruct((N_CHIPS,) + dw.shape[2:], dw.dtype)], N_CHIPS, start, finish)


def _chips_carry(sums, land, part=(0, 1, 1)):
    a, b, p = part
    rh = sums.shape[1]
    rows = pl.ds(a * rh // p, (b - a) * rh // p)

    def copies(srcs, lands, ssem, rsem):
        x, y, c = _place()
        out = []
        for j, (cx, cy) in enumerate(_other_chips(x, y)):
            out.append(_remote(srcs[0].at[2 * cx + cy, rows], lands[0].at[j, rows], ssem.at[j], rsem.at[j],
                               (cx, cy, c)))
        return out

    def start(*a):
        for cp in copies(*a):
            cp.start()

    def finish(*a):
        for cp in copies(*a):
            cp.wait()

    if land is None:
        land = jax.ShapeDtypeStruct((3,) + sums.shape[1:], sums.dtype)
    return _Carry([sums], [land], 3, start, finish)


def _call(name, grid, body, ins, in_specs, out_shapes, out_specs, scratch=(), vmem_mb=48, carry=None):
    n_in, n_out, n_sc = len(ins), len(out_shapes), len(scratch)
    ins, in_specs = list(ins), list(in_specs)
    out_shapes, out_specs, scratch = list(out_shapes), list(out_specs), list(scratch)
    aliases = {}
    if carry is not None:
        ins += carry.srcs
        in_specs += [ANY] * len(carry.srcs)
        for land in carry.lands:
            if not isinstance(land, jax.ShapeDtypeStruct):
                aliases[len(ins)] = len(out_shapes)
                ins.append(land)
                in_specs.append(ANY)
                land = jax.ShapeDtypeStruct(land.shape, land.dtype)
            out_shapes.append(land)
            out_specs.append(ANY)
        scratch += [pltpu.SemaphoreType.DMA((carry.n_sems,)), pltpu.SemaphoreType.DMA((carry.n_sems,))]
    n_in_all, n_out_all = len(ins), len(out_shapes)
    ins = [pltpu.with_memory_space_constraint(a, pltpu.HBM) for a in ins]

    def kbody(*refs):
        in_refs, out_refs, sc = refs[:n_in_all], refs[n_in_all:n_in_all + n_out_all], refs[n_in_all + n_out_all:]
        ids = tuple(pl.program_id(a) for a in range(len(grid)))
        if carry is not None:
            first = functools.reduce(jnp.logical_and, [i == 0 for i in ids])
            last = functools.reduce(jnp.logical_and, [i == g - 1 for i, g in zip(ids, grid)])
            comm = (in_refs[n_in:n_in + len(carry.srcs)], out_refs[n_out:], sc[n_sc], sc[n_sc + 1])

            @pl.when(first)
            def _():
                carry.start(*comm)

            if carry.middle is not None:
                step, total = ids[0], grid[0]
                for i, g in zip(ids[1:], grid[1:]):
                    step, total = step * g + i, total * g

                @pl.when(step == (2 * total) // 3)
                def _():
                    carry.middle(*comm)

        body(in_refs[:n_in], out_refs[:n_out], sc[:n_sc], ids)
        if carry is not None:
            @pl.when(last)
            def _():
                carry.finish(*comm)

    return pl.pallas_call(
        kbody, name=name, grid=grid, in_specs=in_specs, out_specs=out_specs,
        out_shape=[pltpu.HBM(o.shape, o.dtype) for o in out_shapes],
        scratch_shapes=scratch, input_output_aliases=aliases,
        compiler_params=pltpu.CompilerParams(dimension_semantics=("arbitrary",) * len(grid),
                                             vmem_limit_bytes=vmem_mb * MIB),
    )(*ins)


def _matmul(name, grid, kaxis, ins, in_specs, out_shapes, out_specs, dims, epi, tail=None, acc_shape=None, nc=512,
            vmem_mb=48, carry=None):
    nk = grid[kaxis] if kaxis is not None else 1

    def body(in_refs, out_refs, scratch, ids):
        a_ref, b_ref = in_refs[0], in_refs[1]
        n = b_ref.shape[0] if dims == NT else b_ref.shape[1]

        def prod(a, c0):
            b = b_ref[c0:c0 + nc, :] if dims == NT else b_ref[:, c0:c0 + nc]
            return lax.dot_general(a, b, (dims, ((), ())), preferred_element_type=F32)

        if kaxis is None:
            a = a_ref[...]
            for c0 in range(0, n, nc):
                epi(prod(a, c0), slice(c0, c0 + nc), ids, in_refs[2:], out_refs)
            if tail is not None:
                tail(ids, in_refs[2:], out_refs)
        else:
            acc = scratch[0]
            tm = acc.shape[0]
            last = ids[kaxis] == nk - 1
            _zero_when(ids[kaxis] == 0, acc)

            @pl.when(jnp.logical_not(last))
            def _():
                a = a_ref[...]
                for c0 in range(0, n, nc):
                    acc[:, c0:c0 + nc] += prod(a, c0)

            @pl.when(last)
            def _():
                block = epi(acc, ids, in_refs[2:], out_refs)
                for r0 in range(0, tm, tm // 2):
                    a = a_ref[r0:r0 + tm // 2, :]
                    for c0 in range(0, n, nc):
                        acc[r0:r0 + tm // 2, c0:c0 + nc] += prod(a, c0)
                    for q0 in range(r0, r0 + tm // 2, EPI_ROWS):
                        block(slice(q0, q0 + EPI_ROWS))

    return _call(name, grid, body, ins, in_specs, out_shapes, out_specs,
                 [pltpu.VMEM(acc_shape, F32)] if kaxis is not None else [], vmem_mb, carry)


def _row_rsqrt(xf):
    return lax.rsqrt(jnp.mean(xf * xf, axis=-1, keepdims=True) + EPS)


def _norm_bwd(dh, xf, g, resid):
    r = _row_rsqrt(xf)
    xh = xf * r
    dg = jnp.sum(dh * xh, axis=0, keepdims=True)
    dxh = dh * g
    dx = resid + r * (dxh - xh * jnp.mean(dxh * xh, axis=-1, keepdims=True))
    return dx, dg


def _zero_when(first, *refs):
    @pl.when(first)
    def _():
        for ref in refs:
            ref[...] = jnp.zeros_like(ref)


_GELU_K = 0.7978845608028654
_GELU_C = 0.044715


def _gelu(x):
    t = jnp.tanh(_GELU_K * (x + _GELU_C * x * x * x))
    return 0.5 * x * (1.0 + t)


def _gelu_and_grad(x):
    x2 = x * x
    t = jnp.tanh(_GELU_K * (x + _GELU_C * x * x2))
    g = 0.5 * x * (1.0 + t)
    dg = 0.5 * (1.0 + t) + 0.5 * x * (1.0 - t * t) * (_GELU_K * (1.0 + 3.0 * _GELU_C * x2))
    return g, dg


def _window_sum(ext, w, causal):
    n = ext.shape[0]
    s, d = ext, 1
    while d < w:
        s = s + pltpu.roll(s, d if causal else n - d, 0)
        d *= 2
    return s


def _inv_count(t, w):
    return 1.0 / jnp.minimum(t + 1, w).astype(F32)


def _pooled(z_ref, zh_ref, g, w, i, tm):
    cols = slice(GD * g, GD * (g + 1))
    zb = z_ref[:, cols]
    zh = jnp.where(i > 0, zh_ref[:, cols], 0.0)
    ext = jnp.concatenate([zh, zb], axis=0)
    s = _window_sum(ext, w, True)[HALO:, :]
    t = i * tm + lax.broadcasted_iota(jnp.int32, (tm, 1), 0)
    return s * _inv_count(t, w) - zb


def _full(shape, n_axes=1):
    return pl.BlockSpec(shape, lambda *ids: (0,) * len(shape))


def _norm1(x, g, tm, carry):
    s = x.shape[0]

    def body(ins, outs, scratch, ids):
        xf = ins[0][...]
        outs[0][...] = (xf * _row_rsqrt(xf) * ins[1][...]).astype(BF16)

    row = pl.BlockSpec((tm, D), lambda i: (i, 0))
    return _call("norm1", (s // tm,), body, [x, g], [row, _full((1, D))],
                 [jax.ShapeDtypeStruct((s, D), BF16)], [row], vmem_mb=32, carry=carry)


def _mixer_fwd(proj, wt, bst, gv, wpool, scale, tm, carry):
    s = proj.shape[0]
    nq = tm // HD

    def body(ins, outs, scratch, ids):
        up_ref, vp_ref, z_ref, zh_ref, wt_ref, bst_ref, gv_ref, wp_ref, sc_ref = ins
        out_ref = outs[0]
        i = ids[0]
        for h in range(HEADS):
            cols = slice(HD * h, HD * (h + 1))
            gvh = gv_ref[:, cols]
            bcol = bst_ref[h]
            wth = wt_ref[h]

            def chunk(q, c_):
                rows = pl.ds(pl.multiple_of(q * HD, HD), HD)
                u = _gelu(up_ref[rows, cols])
                v = _gelu(vp_ref[rows, cols])
                vh = (v * _row_rsqrt(v) * gvh).astype(BF16)
                mixed = jnp.dot(wth, vh, preferred_element_type=F32) + bcol
                out_ref[rows, cols] = (u * mixed).astype(BF16)
                return c_

            lax.fori_loop(0, nq, chunk, 0, unroll=True)
        for g, w in enumerate(POOL_WINDOWS):
            cols = slice(GD * g, GD * (g + 1))
            pooled = _pooled(z_ref, zh_ref, g, w, i, tm)
            yv = jnp.dot(pooled.astype(BF16), wp_ref[g], preferred_element_type=F32)
            out_ref[:, A_W + GD * g:A_W + GD * (g + 1)] = (yv * sc_ref[:, cols]).astype(BF16)

    hb = tm // HALO
    return _call(
        "mixer_fwd", (s // tm,), body, [proj, proj, proj, proj, wt, bst, gv, wpool, scale],
        [pl.BlockSpec((tm, A_W), lambda i: (i, 0)),
         pl.BlockSpec((tm, A_W), lambda i: (i, 1)),
         pl.BlockSpec((tm, A_W), lambda i: (i, 2)),
         pl.BlockSpec((HALO, A_W), lambda i: (jnp.maximum(i * hb - 1, 0), 2)),
         _full((HEADS, HD, HD)), _full((HEADS, HD, HD)), _full((1, A_W)), _full((4, GD, GD)), _full((1, A_W))],
        [jax.ShapeDtypeStruct((s, D), BF16)], [pl.BlockSpec((tm, D), lambda i: (i, 0))], vmem_mb=40, carry=carry)


def _mixer_bwd(proj, dmix, wt, wtt, bst, gv, wpool, scale, tm, carry):
    s = proj.shape[0]
    nb = s // tm
    nq = tm // HD
    hb = tm // HALO

    def body(ins, outs, scratch, ids):
        (up_ref, vp_ref, z_ref, zh_ref, doa_ref, dob_ref, dobh_ref, wt_ref, wtt_ref, bst_ref, gv_ref, wp_ref,
         sc_ref) = ins
        dproj_ref, dws_ref, dbs_ref, dgv_ref, dwp_ref, dsc_ref = outs
        dbfull = scratch[0]
        i = ids[0]
        last = i == nb - 1
        _zero_when(i == 0, dws_ref, dbfull, dgv_ref, dwp_ref, dsc_ref)

        for h in range(HEADS):
            cols = slice(HD * h, HD * (h + 1))
            gvh = gv_ref[:, cols]
            bcol = bst_ref[h]
            wth = wt_ref[h]
            wtth = wtt_ref[h]

            def chunk(q, c_):
                rows = pl.ds(pl.multiple_of(q * HD, HD), HD)
                u, du_dup = _gelu_and_grad(up_ref[rows, cols])
                v, dv_dvp = _gelu_and_grad(vp_ref[rows, cols])
                rv = _row_rsqrt(v)
                vn = v * rv
                vh = (vn * gvh).astype(BF16)
                mixed = jnp.dot(wth, vh, preferred_element_type=F32) + bcol
                doa = doa_ref[rows, cols].astype(F32)
                dmx = doa * u
                dmxb = dmx.astype(BF16)
                dbfull[h] += dmx
                dws_ref[h] += lax.dot_general(dmxb, vh, (NT, ((), ())), preferred_element_type=F32)
                dvh = jnp.dot(wtth, dmxb, preferred_element_type=F32)
                dgv_ref[:, cols] += jnp.sum(dvh * vn, axis=0, keepdims=True)
                dvn = dvh * gvh
                dv = rv * (dvn - vn * jnp.mean(dvn * vn, axis=-1, keepdims=True))
                dproj_ref[rows, cols] = (doa * mixed * du_dup).astype(BF16)
                dproj_ref[rows, A_W + HD * h:A_W + HD * (h + 1)] = (dv * dv_dvp).astype(BF16)
                return c_

            lax.fori_loop(0, nq, chunk, 0, unroll=True)

        t = i * tm + lax.broadcasted_iota(jnp.int32, (tm, 1), 0)
        th = (i + 1) * tm + lax.broadcasted_iota(jnp.int32, (HALO, 1), 0)
        for g, w in enumerate(POOL_WINDOWS):
            cols = slice(GD * g, GD * (g + 1))
            wpg = wp_ref[g]
            scg = sc_ref[:, cols]
            pb = _pooled(z_ref, zh_ref, g, w, i, tm).astype(BF16)
            ypre = jnp.dot(pb, wpg, preferred_element_type=F32)
            dob = dob_ref[:, cols].astype(F32)
            dsc_ref[:, cols] += jnp.sum(dob * ypre, axis=0, keepdims=True)
            dyb = (dob * scg).astype(BF16)
            dwp_ref[g] += lax.dot_general(pb, dyb, (TN, ((), ())), preferred_element_type=F32)
            dpo = lax.dot_general(dyb, wpg, (NT, ((), ())), preferred_element_type=F32)
            dyh = (dobh_ref[:, cols].astype(F32) * scg).astype(BF16)
            dpoh = lax.dot_general(dyh, wpg, (NT, ((), ())), preferred_element_type=F32)
            dpoh = jnp.where(last, 0.0, dpoh * _inv_count(th, w))
            ext = jnp.concatenate([dpo * _inv_count(t, w), dpoh], axis=0)
            dz = _window_sum(ext, w, False)[:tm, :] - dpo
            dproj_ref[:, 2 * A_W + GD * g:2 * A_W + GD * (g + 1)] = dz.astype(BF16)

        @pl.when(last)
        def _():
            r = lax.broadcasted_iota(jnp.int32, (HD, HD), 0)
            c = lax.broadcasted_iota(jnp.int32, (HD, HD), 1)
            for h in range(HEADS):
                dws_ref[h] = jnp.where(r >= c, dws_ref[h], 0.0)
                dbs_ref[h] = jnp.sum(dbfull[h], axis=-1, keepdims=True)

    return _call(
        "mixer_bwd", (nb,), body, [proj, proj, proj, proj, dmix, dmix, dmix, wt, wtt, bst, gv, wpool, scale],
        [pl.BlockSpec((tm, A_W), lambda i: (i, 0)),
         pl.BlockSpec((tm, A_W), lambda i: (i, 1)),
         pl.BlockSpec((tm, A_W), lambda i: (i, 2)),
         pl.BlockSpec((HALO, A_W), lambda i: (jnp.maximum(i * hb - 1, 0), 2)),
         pl.BlockSpec((tm, A_W), lambda i: (i, 0)),
         pl.BlockSpec((tm, A_W), lambda i: (i, 1)),
         pl.BlockSpec((HALO, A_W), lambda i: (jnp.minimum((i + 1) * hb, s // HALO - 1), 1)),
         _full((HEADS, HD, HD)), _full((HEADS, HD, HD)), _full((HEADS, HD, HD)), _full((1, A_W)),
         _full((4, GD, GD)), _full((1, A_W))],
        [jax.ShapeDtypeStruct((s, IN_W), BF16),
         jax.ShapeDtypeStruct((HEADS, HD, HD), F32),
         jax.ShapeDtypeStruct((HEADS, HD, 1), F32),
         jax.ShapeDtypeStruct((1, A_W), F32),
         jax.ShapeDtypeStruct((4, GD, GD), F32),
         jax.ShapeDtypeStruct((1, A_W), F32)],
        [pl.BlockSpec((tm, IN_W), lambda i: (i, 0)),
         _full((HEADS, HD, HD)), _full((HEADS, HD, 1)), _full((1, A_W)), _full((4, GD, GD)), _full((1, A_W))],
        [pltpu.VMEM((HEADS, HD, HD), F32)], vmem_mb=48, carry=carry)


def _inproj(h1, win_g, tm, carry):
    s = h1.shape[0]
    cw = IN_W // N_CHIPS

    def epi(p, cols, ids, extra, outs):
        outs[0][:, cols] = p

    return _matmul(
        "inproj", (N_CHIPS, s // tm), None, [h1, win_g],
        [pl.BlockSpec((tm, D), lambda j, i: (i, 0)), pl.BlockSpec((None, D, cw), lambda j, i: (j, 0, 0))],
        [jax.ShapeDtypeStruct((s, IN_W), F32)], [pl.BlockSpec((tm, cw), lambda j, i: (i, j))], NN, epi,
        nc=256, vmem_mb=32, carry=carry)


def _outproj(mixed, wout, x, g_ffn, tm, carry):
    s = x.shape[0]

    def epi(p, cols, ids, extra, outs):
        outs[0][:, cols] = extra[0][:, cols] + p

    def tail(ids, extra, outs):
        x2 = outs[0][...]
        outs[1][...] = (x2 * _row_rsqrt(x2) * extra[1][...]).astype(BF16)

    row = pl.BlockSpec((tm, D), lambda i: (i, 0))
    return _matmul(
        "outproj", (s // tm,), None, [mixed, wout, x, g_ffn], [row, _full((D, D)), row, _full((1, D))],
        [jax.ShapeDtypeStruct((s, D), F32), jax.ShapeDtypeStruct((s, D), BF16)], [row, row], NN, epi, tail,
        vmem_mb=48, carry=carry)


def _up(h2, wup_g, tm, tn, carry):
    s = h2.shape[0]
    per = D // tn

    def epi(p, cols, ids, extra, outs):
        a = jnp.maximum(p, 0.0)
        outs[0][:, cols] = (a * a).astype(BF16)

    return _matmul(
        "up", (D_FF // tn, s // tm), None, [h2, wup_g],
        [pl.BlockSpec((tm, D), lambda j, i: (i, 0)), pl.BlockSpec((None, D, tn), lambda j, i: (j // per, 0, j % per))],
        [jax.ShapeDtypeStruct((s, D_FF), BF16)], [pl.BlockSpec((tm, tn), lambda j, i: (i, j))], NN, epi,
        vmem_mb=40, carry=carry)


def _down(act, wdown, x2, tgt, g_final, tm, tk):
    s = x2.shape[0]

    def epi(acc, ids, extra, outs):
        x2_ref, t_ref, g_ref = extra
        dxb_ref, dgf_ref, loss_ref = outs
        g = g_ref[...]
        _zero_when(ids[0] == 0, dgf_ref, loss_ref)

        def block(rows):
            x3 = x2_ref[rows, :] + acc[rows, :]
            r = _row_rsqrt(x3)
            xh = x3 * r
            diff = xh * g - t_ref[rows, :]
            dy = diff * (1.0 / D)
            dxh = dy * g
            dx = r * (dxh - xh * jnp.mean(dxh * xh, axis=-1, keepdims=True))
            dxb_ref[rows, :] = dx.astype(BF16)
            dgf_ref[...] += jnp.sum(dy * xh, axis=0, keepdims=True)
            loss_ref[...] += jnp.sum(diff * diff, axis=0, keepdims=True)

        return block

    row = pl.BlockSpec((tm, D), lambda i, k: (i, 0))
    vec = pl.BlockSpec((1, D), lambda i, k: (0, 0))
    return _matmul(
        "down", (s // tm, D_FF // tk), 1, [act, wdown, x2, tgt, g_final],
        [pl.BlockSpec((tm, tk), lambda i, k: (i, k)), pl.BlockSpec((tk, D), lambda i, k: (k, 0)), row, row, vec],
        [jax.ShapeDtypeStruct((s, D), BF16), jax.ShapeDtypeStruct((1, D), F32), jax.ShapeDtypeStruct((1, D), F32)],
        [row, vec, vec], NN, epi, acc_shape=(tm, D), vmem_mb=56)


def _dact(dx3b, wdown, act, tm, tn, carry):
    s = dx3b.shape[0]

    def epi(p, cols, ids, extra, outs):
        outs[0][:, cols] = (p * (2.0 * jnp.sqrt(extra[0][:, cols].astype(F32)))).astype(BF16)

    tile = pl.BlockSpec((tm, tn), lambda j, i: (i, j))
    return _matmul(
        "dact", (D_FF // tn, s // tm), None, [dx3b, wdown, act],
        [pl.BlockSpec((tm, D), lambda j, i: (i, 0)), pl.BlockSpec((tn, D), lambda j, i: (j, 0)), tile],
        [jax.ShapeDtypeStruct((s, D_FF), BF16)], [tile], NT, epi, vmem_mb=40, carry=carry)


def _dweight(name, lhs, rhs, n_shards, rows, cols, tm, nc=512, carry=None):
    s = lhs.shape[0]

    def epi(p, cs, ids, extra, outs):
        outs[0][:, cs] = p.astype(BF16)

    return _matmul(
        name, (n_shards, rows // tm), None, [lhs, rhs],
        [pl.BlockSpec((s, tm), lambda j, i: (0, i)), pl.BlockSpec((s, cols), lambda j, i: (0, j))],
        [jax.ShapeDtypeStruct((n_shards, rows, cols), BF16)],
        [pl.BlockSpec((None, tm, cols), lambda j, i: (j, i, 0))], TN, epi, nc=nc, vmem_mb=56, carry=carry)


def _dh2(da, wup_g, x2, dx3, g_ffn, tm, tk, carry):
    s = x2.shape[0]
    per = D // tk

    def epi(acc, ids, extra, outs):
        x2_ref, dx3_ref, g_ref = extra
        g = g_ref[...]
        _zero_when(ids[0] == 0, outs[1])

        def block(rows):
            dx, dg = _norm_bwd(acc[rows, :], x2_ref[rows, :], g, dx3_ref[rows, :].astype(F32))
            outs[0][rows, :] = dx.astype(BF16)
            outs[1][...] += dg

        return block

    row = pl.BlockSpec((tm, D), lambda i, k: (i, 0))
    vec = pl.BlockSpec((1, D), lambda i, k: (0, 0))
    return _matmul(
        "dh2", (s // tm, D_FF // tk), 1, [da, wup_g, x2, dx3, g_ffn],
        [pl.BlockSpec((tm, tk), lambda i, k: (i, k)),
         pl.BlockSpec((None, D, tk), lambda i, k: (k // per, 0, k % per)), row, row, vec],
        [jax.ShapeDtypeStruct((s, D), BF16), jax.ShapeDtypeStruct((1, D), F32)],
        [row, vec], NT, epi, acc_shape=(tm, D), vmem_mb=56, carry=carry)


def _dmixed(dx2b, wout, tm, carry):
    s = dx2b.shape[0]

    def epi(p, cols, ids, extra, outs):
        outs[0][:, cols] = p.astype(BF16)

    row = pl.BlockSpec((tm, D), lambda i: (i, 0))
    return _matmul(
        "dmixed", (s // tm,), None, [dx2b, wout], [row, _full((D, D))],
        [jax.ShapeDtypeStruct((s, D), BF16)], [row], NT, epi, vmem_mb=40, carry=carry)


def _dh1(dproj, win_g, x, dx2, g_mix, tm, carry):
    s = x.shape[0]
    cw = IN_W // N_CHIPS

    def epi(acc, ids, extra, outs):
        x_ref, dx2_ref, g_ref = extra
        g = g_ref[...]
        _zero_when(ids[0] == 0, outs[1])

        def block(rows):
            dx, dg = _norm_bwd(acc[rows, :], x_ref[rows, :], g, dx2_ref[rows, :].astype(F32))
            outs[0][rows, :] = dx
            outs[1][...] += dg

        return block

    row = pl.BlockSpec((tm, D), lambda i, j: (i, 0))
    vec = pl.BlockSpec((1, D), lambda i, j: (0, 0))
    return _matmul(
        "dh1", (s // tm, N_CHIPS), 1, [dproj, win_g, x, dx2, g_mix],
        [pl.BlockSpec((tm, cw), lambda i, j: (i, j)), pl.BlockSpec((None, D, cw), lambda i, j: (j, 0, 0)),
         row, row, vec],
        [jax.ShapeDtypeStruct((s, D), F32), jax.ShapeDtypeStruct((1, D), F32)],
        [row, vec], NT, epi, acc_shape=(tm, D), vmem_mb=52, carry=carry)


def _cast_place(name, w, tr):
    rows, cols = w.shape

    def body(ins, outs, scratch, ids):
        outs[0][...] = ins[0][...].astype(BF16)

    return _call(name, (rows // tr,), body, [w], [pl.BlockSpec((tr, cols), lambda r: (r, 0))],
                 [jax.ShapeDtypeStruct((N_CHIPS, rows, cols), BF16)],
                 [pl.BlockSpec((None, tr, cols), lambda r: (_my_chip(), r, 0))], vmem_mb=32)[0]


def _small_allreduce(part):
    rows = part.shape[0]
    rh = rows // 2

    def body(p_ref, o_ref, sib_ref, slots, send_sems, recv_sems):
        x, y, c = _place()
        k = 2 * x + y
        sib = (x, y, 1 - c)
        half = pl.ds(pl.multiple_of(c * rh, 8), rh)
        cp = _remote(p_ref, sib_ref, send_sems.at[0], recv_sems.at[0], sib)
        cp.start()
        cp.wait()
        slots[k] = p_ref[half, :] + sib_ref[half, :]
        cps = []
        for j, (cx, cy) in enumerate(_other_chips(x, y)):
            cp = _remote(slots.at[k], slots.at[k], send_sems.at[1 + j], recv_sems.at[1 + j], (cx, cy, c))
            cp.start()
            cps.append(cp)
        for cp in cps:
            cp.wait()
        o_ref[half, :] = ((slots[0] + slots[1]) + slots[2]) + slots[3]
        cp = _remote(o_ref.at[half], o_ref.at[half], send_sems.at[4], recv_sems.at[4], sib)
        cp.start()
        cp.wait()

    vm = pl.BlockSpec(memory_space=pltpu.VMEM)
    return pl.pallas_call(
        body, name="small_allreduce", in_specs=[vm], out_specs=vm,
        out_shape=jax.ShapeDtypeStruct(part.shape, F32),
        scratch_shapes=[pltpu.VMEM(part.shape, F32), pltpu.VMEM((N_CHIPS, rh, LANES), F32),
                        pltpu.SemaphoreType.DMA((5,)), pltpu.SemaphoreType.DMA((5,))],
        compiler_params=pltpu.CompilerParams(vmem_limit_bytes=32 * MIB),
    )(part)


def _comm_only(name, carry):
    ns, nl = len(carry.srcs), len(carry.lands)
    lands_in = [l for l in carry.lands if not isinstance(l, jax.ShapeDtypeStruct)]
    assert len(lands_in) in (0, nl)

    def body(*refs):
        srcs = refs[:ns]
        lands = refs[ns + len(lands_in):ns + len(lands_in) + nl]
        ssem, rsem = refs[ns + len(lands_in) + nl:]
        carry.start(srcs, lands, ssem, rsem)
        if carry.middle is not None:
            carry.middle(srcs, lands, ssem, rsem)
        carry.finish(srcs, lands, ssem, rsem)

    return pl.pallas_call(
        body, name=name, in_specs=[ANY] * (ns + len(lands_in)), out_specs=[ANY] * nl,
        out_shape=[jax.ShapeDtypeStruct(l.shape, l.dtype) for l in carry.lands],
        input_output_aliases={ns + i: i for i in range(len(lands_in))},
        scratch_shapes=[pltpu.SemaphoreType.DMA((carry.n_sems,)), pltpu.SemaphoreType.DMA((carry.n_sems,))],
    )(*carry.srcs, *lands_in)


def _share_carry(grads):
    n = len(grads)

    def copies(srcs, lands, ssem, rsem):
        x, y, c = _place()
        return [_remote(lands[w].at[c], lands[w].at[c], ssem.at[w], rsem.at[w], (x, y, 1 - c)) for w in range(n)]

    def start(*a):
        for cp in copies(*a):
            cp.start()

    def finish(*a):
        for cp in copies(*a):
            cp.wait()

    return _Carry([], grads, n, start, finish)


def _add_sibling(dw, got, tr):
    _, _, rh, cols = dw.shape

    def body(ins, outs, scratch, ids):
        outs[0][...] = (ins[0][...].astype(F32) + ins[1][...].astype(F32)).astype(BF16)

    blk = pl.BlockSpec((None, tr, cols), lambda j, r: (j, r, 0))
    return _call("add_sibling", (N_CHIPS, rh // tr), body, [dw, got],
                 [pl.BlockSpec((None, None, tr, cols), lambda j, r: (j, _my_core(), r, 0)), blk],
                 [jax.ShapeDtypeStruct((N_CHIPS, rh, cols), BF16)], [blk], vmem_mb=32)[0]


def _add_chips(sums, got, tr):
    _, rh, cols = sums.shape

    def body(ins, outs, scratch, ids):
        b = ins[1][...].astype(F32)
        outs[0][...] = ((ins[0][...].astype(F32) + b[0]) + b[1]) + b[2]

    return _call("add_chips", (rh // tr,), body, [sums, got],
                 [pl.BlockSpec((None, tr, cols), lambda r: (_my_chip(), r, 0)),
                  pl.BlockSpec((3, tr, cols), lambda r: (0, r, 0))],
                 [jax.ShapeDtypeStruct((2, rh, cols), F32)],
                 [pl.BlockSpec((None, tr, cols), lambda r: (_my_core(), r, 0))], vmem_mb=32)[0]


def _adamw_math(w, g, m, v):
    m = ADAM_B1 * m + (1.0 - ADAM_B1) * g
    v = ADAM_B2 * v + (1.0 - ADAM_B2) * (g * g)
    m_hat = m / (1.0 - ADAM_B1 ** ADAM_STEP)
    v_hat = v / (1.0 - ADAM_B2 ** ADAM_STEP)
    delta = -ADAM_LR * (m_hat / (jnp.sqrt(v_hat) + ADAM_EPS) + ADAM_WD * w)
    return delta, m, v


def _adamw(name, w, g, m, v, tr, carry=None):
    rows, cols = w.shape

    def body(ins, outs, scratch, ids):
        g_val = ins[1][...]
        outs[0][...] = g_val
        outs[1][...], outs[2][...], outs[3][...] = _adamw_math(ins[0][...], g_val, ins[2][...], ins[3][...])

    blk = pl.BlockSpec((tr, cols), lambda r: (r, 0))
    return _call(name, (rows // tr,), body, [w, g, m, v], [blk] * 4,
                 [jax.ShapeDtypeStruct(w.shape, F32)] * 4, [blk] * 4, vmem_mb=40, carry=carry)


def _rows(a):
    return a.reshape(-1, LANES)


def kernel(x, g_mix, w_in, g_v, w_s, b_s, w_pool, pool_scale, w_out, g_ffn, w_up, w_down, g_final, loss_target, m_g_mix, m_w_in, m_g_v, m_w_s, m_b_s, m_w_pool, m_pool_scale, m_w_out, m_g_ffn, m_w_up, m_w_down, m_g_final, v_g_mix, v_w_in, v_g_v, v_w_s, v_b_s, v_w_pool, v_pool_scale, v_w_out, v_g_ffn, v_w_up, v_w_down, v_g_final):
    tm = 512
    xs = x[0]
    tgt = loss_target[0]
    chip = _my_chip()

    win_g = _cast_place("cast_w_in", w_in[0], 256)
    wpool_g = _cast_place("cast_w_pool", w_pool[0].reshape(4 * 64, GD), 128)
    wout_g = _cast_place("cast_w_out", w_out[0], 128)
    wup_g = _cast_place("cast_w_up", w_up[0], 256)
    wdown_g = _cast_place("cast_w_down", w_down[0], 256)
    h1, win_g, wpool_g = _norm1(xs, g_mix, tm, _gather_whole([win_g, wpool_g]))
    wpool_f = wpool_g.reshape(N_CHIPS, 4, 64, GD).transpose(1, 0, 2, 3).reshape(4, GD, GD)
    tril = jnp.tril(jnp.ones((HD, HD), dtype=bool))
    wt = jnp.where(tril[None], w_s[0], 0.0).astype(BF16)
    wtt = wt.transpose(0, 2, 1)
    bst = jnp.broadcast_to(b_s[0][:, :, None], (HEADS, HD, HD))
    gfin = g_final.reshape(1, D)

    proj, wout_g = _inproj(h1, win_g, tm, _gather_whole([wout_g]))
    mixed, wup_g = _mixer_fwd(proj, wt, bst, g_v, wpool_f, pool_scale, tm, _gather_near([wup_g]))
    wout_f = wout_g.reshape(D, D)
    x2, h2, wup_g = _outproj(mixed, wout_f, xs, g_ffn, 256, _gather_far([wup_g]))
    act, wdown_g = _up(h2, wup_g, tm, 2048, _gather_whole([wdown_g]))
    wdown_f = wdown_g.reshape(D_FF, D)
    dx3b, dgf, lossv = _down(act, wdown_f, x2, tgt, gfin, tm, 2048)

    halves = lambda dw, rows, cols: dw.reshape(N_CHIPS, 2, rows // (2 * N_CHIPS), cols)
    cw = IN_W // N_CHIPS
    dwdown = halves(_dweight("dw_down", act, dx3b, 1, D_FF, D, 512)[0], D_FF, D)
    da, sib_down = _dact(dx3b, wdown_f, act, tm, 2048, _sibling_carry(dwdown))
    sum_down = _add_sibling(dwdown, sib_down, 256)
    dwup, got_down = _dweight("dw_up", h2, da, N_CHIPS, D, D, 512, carry=_chips_carry(sum_down, None, (0, 3, 4)))
    dwup = halves(dwup, D_FF, D)
    dx2b, dgffn, got_down, sib_up = _dh2(da, wup_g, x2, dx3b, g_ffn, tm, 2048,
                                         _join(_chips_carry(sum_down, got_down, (3, 4, 4)), _sibling_carry(dwup)))
    half_down = _add_chips(sum_down, got_down, 256)
    sum_up = _add_sibling(dwup, sib_up, 256)
    dwout, got_up = _dweight("dw_out", mixed, dx2b, 1, D, D, 512, carry=_chips_carry(sum_up, None, (0, 1, 4)))
    dwout = halves(dwout, D, D)
    dmix, got_up, sib_out, half_down = _dmixed(
        dx2b, wout_f, 256,
        _join(_chips_carry(sum_up, got_up, (1, 2, 4)), _sibling_carry(dwout), _share_carry([half_down])))
    sum_out = _add_sibling(dwout, sib_out, 256)
    g_down, d_down, nm_down, nv_down = _adamw("adamw_down", w_down[0], half_down.reshape(D, D), m_w_down[0],
                                              v_w_down[0], 128)
    dproj, dws, dbs, dgv, dwp, dsc, got_up = _mixer_bwd(proj, dmix, wt, wtt, bst, g_v, wpool_f, pool_scale, tm,
                                                        _chips_carry(sum_up, got_up, (2, 4, 4)))
    half_up = _add_chips(sum_up, got_up, 256)
    dwin, got_out, half_up = _dweight("dw_in", h1, dproj, N_CHIPS, D, cw, 512, nc=256,
                                      carry=_join(_chips_carry(sum_out, None), _share_carry([half_up])))
    dwin = halves(dwin, N_CHIPS * D, cw)
    sib_in = _comm_only("sibling_in", _sibling_carry(dwin))[0]
    sum_in = _add_sibling(dwin, sib_in, 256)
    g_up, d_up, nm_up, nv_up = _adamw("adamw_up", w_up[0], half_up.reshape(D, D), m_w_up[0], v_w_up[0], 128)
    half_out = _add_chips(sum_out, got_out, 256)
    grad_x, dgmix, got_in, half_out = _dh1(dproj, win_g, xs, dx2b, g_mix, tm,
                                           _join(_chips_carry(sum_in, None), _share_carry([half_out])))
    half_in = _add_chips(sum_in, got_in, 256)
    half_in = _comm_only("share_half_in", _share_carry([half_in]))[0]
    g_out, d_out, nm_out, nv_out = _adamw("adamw_out", w_out[0], half_out.reshape(D // N_CHIPS, D), m_w_out[0],
                                          v_w_out[0], 128)
    g_in, d_in, nm_in, nv_in = _adamw("adamw_in", w_in[0], half_in.reshape(D, cw), m_w_in[0], v_w_in[0], 128)

    pieces = [dgmix, dgv, dws, dbs, dwp, dsc, dgffn, dgf, lossv, jnp.zeros((8 * LANES,), F32)]
    sizes = [p.size // LANES for p in pieces]
    tot = _small_allreduce(jnp.concatenate([_rows(p) for p in pieces], axis=0))
    offs = [sum(sizes[:i]) for i in range(len(sizes))]
    take = lambda i: tot[offs[i]:offs[i] + sizes[i]]
    s_gmix, s_gv, s_ws, s_bs, s_wp, s_sc, s_gffn, s_gf = [take(i) for i in range(8)]
    loss = (0.5 / D) * jnp.sum(take(8))
    s_wp_mine = lax.dynamic_slice_in_dim(s_wp.reshape(4, GD, GD), chip * 64, 64, axis=1)
    small_g = [s_gmix, s_gv, s_ws, s_bs, _rows(s_wp_mine), s_sc, s_gffn, s_gf]
    small_w = [g_mix, g_v, w_s, b_s, w_pool, pool_scale, g_ffn, g_final]
    small_m = [m_g_mix, m_g_v, m_w_s, m_b_s, m_w_pool, m_pool_scale, m_g_ffn, m_g_final]
    small_v = [v_g_mix, v_g_v, v_w_s, v_b_s, v_w_pool, v_pool_scale, v_g_ffn, v_g_final]
    cat = lambda parts: jnp.concatenate([_rows(p) for p in parts], axis=0)
    sg = cat(small_g)
    sg, sd, snm, snv = _adamw("adamw_small", cat(small_w), sg, cat(small_m), cat(small_v), sg.shape[0])
    ssz = [p.size // LANES for p in small_w]
    soff = [sum(ssz[:i]) for i in range(len(ssz))]
    split = lambda a: [a[soff[i]:soff[i] + ssz[i]].reshape(small_w[i].shape) for i in range(len(ssz))]
    gs, ds, nms, nvs = split(sg), split(sd), split(snm), split(snv)

    def ordered(small, w_in_, w_out_, w_up_, w_down_):
        return [small[0], w_in_[None], small[1], small[2], small[3], small[4], small[5], w_out_[None], small[6],
                w_up_[None], w_down_[None], small[7]]

    return (loss, grad_x[None],
            *ordered(gs, g_in, g_out, g_up, g_down),
            *ordered(ds, d_in, d_out, d_up, d_down),
            *ordered(nms, nm_in, nm_out, nm_up, nm_down),
            *ordered(nvs, nv_in, nv_out, nv_up, nv_down))
```

```python
import functools

import jax
import jax.numpy as jnp
from jax import lax
from jax.experimental import pallas as pl
from jax.experimental.pallas import tpu as pltpu

F32 = jnp.float32
BF16 = jnp.bfloat16
EPS = 1e-6
D = 2048
A_W = 1024
HEADS = 8
HD = 128
POOL_WINDOWS = (2, 4, 8, 16)
GD = 256
IN_W = 3072
D_FF = 8192
N_CHIPS = 4
HALO = 16
EPI_ROWS = 128
LANES = 128
MIB = 2 ** 20

ADAM_LR, ADAM_B1, ADAM_B2, ADAM_EPS, ADAM_WD, ADAM_STEP = 0.001, 0.9, 0.999, 1e-08, 0.01, 10

ANY = pl.BlockSpec(memory_space=pl.ANY)
MESH = pl.DeviceIdType.MESH

NN = ((1,), (0,))
NT = ((1,), (1,))
TN = ((0,), (0,))


def _place():
    return lax.axis_index("x"), lax.axis_index("y"), lax.axis_index("c")


def _my_chip():
    return 2 * lax.axis_index("x") + lax.axis_index("y")


def _my_core():
    return lax.axis_index("c")


def _other_chips(x, y):
    return [(1 - x, y), (x, 1 - y), (1 - x, 1 - y)]


def _remote(src, dst, send_sem, recv_sem, dev):
    return pltpu.make_async_remote_copy(src_ref=src, dst_ref=dst, send_sem=send_sem, recv_sem=recv_sem,
                                        device_id=dev, device_id_type=MESH)


class _SemView:
    def __init__(self, sems, base):
        self.sems, self.base = sems, base

    @property
    def at(self):
        return self

    def __getitem__(self, i):
        return self.sems.at[self.base + i]


class _Carry:
    def __init__(self, srcs, lands, n_sems, start, finish, middle=None):
        self.srcs, self.lands, self.n_sems, self.start, self.finish = list(srcs), list(lands), n_sems, start, finish
        self.middle = middle


def _join(*carries):
    def run(which):
        def go(srcs, lands, ssem, rsem):
            so = lo = qo = 0
            for c in carries:
                if getattr(c, which) is not None:
                    getattr(c, which)(srcs[so:so + len(c.srcs)], lands[lo:lo + len(c.lands)],
                                      _SemView(ssem, qo), _SemView(rsem, qo))
                so, lo, qo = so + len(c.srcs), lo + len(c.lands), qo + c.n_sems
        return go

    middle = run("middle") if any(c.middle is not None for c in carries) else None
    return _Carry([s for c in carries for s in c.srcs], [l for c in carries for l in c.lands],
                  sum(c.n_sems for c in carries), run("start"), run("finish"), middle)


GATHER_SEMS = 7


def _gather_copies(lands, ssem, rsem, part=(0, 1, 1)):
    x, y, c = _place()
    k, kx, ky, kd = 2 * x + y, 2 * (1 - x) + y, 2 * x + (1 - y), 2 * (1 - x) + (1 - y)
    to_x, to_y, sib = (1 - x, y, c), (x, 1 - y, c), (x, y, 1 - c)
    out = []
    for w, land in enumerate(lands):
        rh = land.shape[1] // 2
        rq = rh // 2
        half = pl.ds(pl.multiple_of(c * rh, 16), rh)
        quarters = [pl.ds(pl.multiple_of(c * rh + q * rq, 16), rq) for q in range(2)]
        pa, pb, pp = part
        sent = pl.ds(pl.multiple_of(c * rh + pa * rh // pp, 16), (pb - pa) * rh // pp)

        def cp(i, piece, dev, w=w):
            return _remote(piece, piece, ssem.at[GATHER_SEMS * w + i], rsem.at[GATHER_SEMS * w + i], dev)

        out.append(dict(
            ax=cp(0, land.at[k, sent], to_x), ay=cp(1, land.at[k, sent], to_y),
            rx=cp(2, land.at[kx, quarters[0]], to_y), ry=cp(3, land.at[ky, quarters[1]], to_x),
            fx=cp(4, land.at[kx, half], sib), fy=cp(5, land.at[ky, half], sib), fd=cp(6, land.at[kd, half], sib)))
    return out


def _gather_whole(gathered):
    def start(srcs, lands, ssem, rsem):
        for d in _gather_copies(lands, ssem, rsem):
            d["ax"].start()
            d["ay"].start()

    def middle(srcs, lands, ssem, rsem):
        for d in _gather_copies(lands, ssem, rsem):
            d["ax"].wait_recv()
            d["rx"].start()
            d["fx"].start()
            d["ay"].wait_recv()
            d["ry"].start()
            d["fy"].start()

    def finish(srcs, lands, ssem, rsem):
        for d in _gather_copies(lands, ssem, rsem):
            d["rx"].wait_recv()
            d["ry"].wait_recv()
            d["fd"].start()
            for name in ("fx", "fy", "fd"):
                d[name].wait_recv()
            for cp in d.values():
                cp.wait_send()

    return _Carry([], gathered, GATHER_SEMS * len(gathered), start, finish, middle)


def _gather_near(gathered, part=(0, 1, 1)):
    def start(srcs, lands, ssem, rsem):
        for d in _gather_copies(lands, ssem, rsem, part):
            d["ax"].start()
            d["ay"].start()

    def finish(srcs, lands, ssem, rsem):
        for d in _gather_copies(lands, ssem, rsem, part):
            for name in ("ax", "ay"):
                d[name].wait_recv()
                d[name].wait_send()

    return _Carry([], gathered, GATHER_SEMS * len(gathered), start, finish)


def _gather_far(gathered):
    def start(srcs, lands, ssem, rsem):
        for d in _gather_copies(lands, ssem, rsem):
            for name in ("rx", "ry", "fx", "fy"):
                d[name].start()

    def finish(srcs, lands, ssem, rsem):
        for d in _gather_copies(lands, ssem, rsem):
            d["rx"].wait_recv()
            d["ry"].wait_recv()
            d["fd"].start()
            for name in ("fx", "fy", "fd"):
                d[name].wait_recv()
            for name in ("rx", "ry", "fx", "fy", "fd"):
                d[name].wait_send()

    return _Carry([], gathered, GATHER_SEMS * len(gathered), start, finish)


def _sibling_carry(dw):
    def copies(srcs, lands, ssem, rsem):
        x, y, c = _place()
        return [_remote(srcs[0].at[j, 1 - c], lands[0].at[j], ssem.at[j], rsem.at[j], (x, y, 1 - c))
                for j in range(N_CHIPS)]

    def start(*a):
        for cp in copies(*a):
            cp.start()

    def finish(*a):
        for cp in copies(*a):
            cp.wait()

    return _Carry([dw], [jax.ShapeDtypeStruct((N_CHIPS,) + dw.shape[2:], dw.dtype)], N_CHIPS, start, finish)


def _chips_carry(sums, land, part=(0, 1, 1)):
    a, b, p = part
    rh = sums.shape[1]
    rows = pl.ds(a * rh // p, (b - a) * rh // p)

    def copies(srcs, lands, ssem, rsem):
        x, y, c = _place()
        out = []
        for j, (cx, cy) in enumerate(_other_chips(x, y)):
            out.append(_remote(srcs[0].at[2 * cx + cy, rows], lands[0].at[j, rows], ssem.at[j], rsem.at[j],
                               (cx, cy, c)))
        return out

    def start(*a):
        for cp in copies(*a):
            cp.start()

    def finish(*a):
        for cp in copies(*a):
            cp.wait()

    if land is None:
        land = jax.ShapeDtypeStruct((3,) + sums.shape[1:], sums.dtype)
    return _Carry([sums], [land], 3, start, finish)


def _call(name, grid, body, ins, in_specs, out_shapes, out_specs, scratch=(), vmem_mb=48, carry=None):
    n_in, n_out, n_sc = len(ins), len(out_shapes), len(scratch)
    ins, in_specs = list(ins), list(in_specs)
    out_shapes, out_specs, scratch = list(out_shapes), list(out_specs), list(scratch)
    aliases = {}
    if carry is not None:
        ins += carry.srcs
        in_specs += [ANY] * len(carry.srcs)
        for land in carry.lands:
            if not isinstance(land, jax.ShapeDtypeStruct):
                aliases[len(ins)] = len(out_shapes)
                ins.append(land)
                in_specs.append(ANY)
                land = jax.ShapeDtypeStruct(land.shape, land.dtype)
            out_shapes.append(land)
            out_specs.append(ANY)
        scratch += [pltpu.SemaphoreType.DMA((carry.n_sems,)), pltpu.SemaphoreType.DMA((carry.n_sems,))]
    n_in_all, n_out_all = len(ins), len(out_shapes)
    ins = [pltpu.with_memory_space_constraint(a, pltpu.HBM) for a in ins]

    def kbody(*refs):
        in_refs, out_refs, sc = refs[:n_in_all], refs[n_in_all:n_in_all + n_out_all], refs[n_in_all + n_out_all:]
        ids = tuple(pl.program_id(a) for a in range(len(grid)))
        if carry is not None:
            first = functools.reduce(jnp.logical_and, [i == 0 for i in ids])
            last = functools.reduce(jnp.logical_and, [i == g - 1 for i, g in zip(ids, grid)])
            comm = (in_refs[n_in:n_in + len(carry.srcs)], out_refs[n_out:], sc[n_sc], sc[n_sc + 1])

            @pl.when(first)
            def _():
                carry.start(*comm)

            if carry.middle is not None:
                step, total = ids[0], grid[0]
                for i, g in zip(ids[1:], grid[1:]):
                    step, total = step * g + i, total * g

                @pl.when(step == (2 * total) // 3)
                def _():
                    carry.middle(*comm)

        body(in_refs[:n_in], out_refs[:n_out], sc[:n_sc], ids)
        if carry is not None:
            @pl.when(last)
            def _():
                carry.finish(*comm)

    return pl.pallas_call(
        kbody, name=name, grid=grid, in_specs=in_specs, out_specs=out_specs,
        out_shape=[pltpu.HBM(o.shape, o.dtype) for o in out_shapes],
        scratch_shapes=scratch, input_output_aliases=aliases,
        compiler_params=pltpu.CompilerParams(dimension_semantics=("arbitrary",) * len(grid),
                                             vmem_limit_bytes=vmem_mb * MIB),
    )(*ins)


def _matmul(name, grid, kaxis, ins, in_specs, out_shapes, out_specs, dims, epi, tail=None, acc_shape=None, nc=512,
            vmem_mb=48, carry=None):
    nk = grid[kaxis] if kaxis is not None else 1

    def body(in_refs, out_refs, scratch, ids):
        a_ref, b_ref = in_refs[0], in_refs[1]
        n = b_ref.shape[0] if dims == NT else b_ref.shape[1]

        def prod(a, c0):
            b = b_ref[c0:c0 + nc, :] if dims == NT else b_ref[:, c0:c0 + nc]
            return lax.dot_general(a, b, (dims, ((), ())), preferred_element_type=F32)

        if kaxis is None:
            a = a_ref[...]
            for c0 in range(0, n, nc):
                epi(prod(a, c0), slice(c0, c0 + nc), ids, in_refs[2:], out_refs)
            if tail is not None:
                tail(ids, in_refs[2:], out_refs)
        else:
            acc = scratch[0]
            tm = acc.shape[0]
            last = ids[kaxis] == nk - 1
            _zero_when(ids[kaxis] == 0, acc)

            @pl.when(jnp.logical_not(last))
            def _():
                a = a_ref[...]
                for c0 in range(0, n, nc):
                    acc[:, c0:c0 + nc] += prod(a, c0)

            @pl.when(last)
            def _():
                block = epi(acc, ids, in_refs[2:], out_refs)
                for r0 in range(0, tm, tm // 2):
                    a = a_ref[r0:r0 + tm // 2, :]
                    for c0 in range(0, n, nc):
                        acc[r0:r0 + tm // 2, c0:c0 + nc] += prod(a, c0)
                    for q0 in range(r0, r0 + tm // 2, EPI_ROWS):
                        block(slice(q0, q0 + EPI_ROWS))

    return _call(name, grid, body, ins, in_specs, out_shapes, out_specs,
                 [pltpu.VMEM(acc_shape, F32)] if kaxis is not None else [], vmem_mb, carry)


def _row_rsqrt(xf):
    return lax.rsqrt(jnp.mean(xf * xf, axis=-1, keepdims=True) + EPS)


def _norm_bwd(dh, xf, g, resid):
    r = _row_rsqrt(xf)
    xh = xf * r
    dg = jnp.sum(dh * xh, axis=0, keepdims=True)
    dxh = dh * g
    dx = resid + r * (dxh - xh * jnp.mean(dxh * xh, axis=-1, keepdims=True))
    return dx, dg


def _zero_when(first, *refs):
    @pl.when(first)
    def _():
        for ref in refs:
            ref[...] = jnp.zeros_like(ref)


_GELU_K = 0.7978845608028654
_GELU_C = 0.044715


def _gelu(x):
    t = jnp.tanh(_GELU_K * (x + _GELU_C * x * x * x))
    return 0.5 * x * (1.0 + t)


def _gelu_and_grad(x):
    x2 = x * x
    t = jnp.tanh(_GELU_K * (x + _GELU_C * x * x2))
    g = 0.5 * x * (1.0 + t)
    dg = 0.5 * (1.0 + t) + 0.5 * x * (1.0 - t * t) * (_GELU_K * (1.0 + 3.0 * _GELU_C * x2))
    return g, dg


def _window_sum(ext, w, causal):
    n = ext.shape[0]
    s, d = ext, 1
    while d < w:
        s = s + pltpu.roll(s, d if causal else n - d, 0)
        d *= 2
    return s


def _inv_count(t, w):
    return 1.0 / jnp.minimum(t + 1, w).astype(F32)


def _pooled(z_ref, zh_ref, g, w, i, tm):
    cols = slice(GD * g, GD * (g + 1))
    zb = z_ref[:, cols]
    zh = jnp.where(i > 0, zh_ref[:, cols], 0.0)
    ext = jnp.concatenate([zh, zb], axis=0)
    s = _window_sum(ext, w, True)[HALO:, :]
    t = i * tm + lax.broadcasted_iota(jnp.int32, (tm, 1), 0)
    return s * _inv_count(t, w) - zb


def _full(shape, n_axes=1):
    return pl.BlockSpec(shape, lambda *ids: (0,) * len(shape))


def _norm1(x, g, tm, carry):
    s = x.shape[0]

    def body(ins, outs, scratch, ids):
        xf = ins[0][...]
        outs[0][...] = (xf * _row_rsqrt(xf) * ins[1][...]).astype(BF16)

    row = pl.BlockSpec((tm, D), lambda i: (i, 0))
    return _call("norm1", (s // tm,), body, [x, g], [row, _full((1, D))],
                 [jax.ShapeDtypeStruct((s, D), BF16)], [row], vmem_mb=32, carry=carry)


def _mixer_fwd(proj, wt, bst, gv, wpool, scale, tm, carry):
    s = proj.shape[0]
    nq = tm // HD

    def body(ins, outs, scratch, ids):
        up_ref, vp_ref, z_ref, zh_ref, wt_ref, bst_ref, gv_ref, wp_ref, sc_ref = ins
        out_ref = outs[0]
        i = ids[0]
        for h in range(HEADS):
            cols = slice(HD * h, HD * (h + 1))
            gvh = gv_ref[:, cols]
            bcol = bst_ref[h]
            wth = wt_ref[h]

            def chunk(q, c_):
                rows = pl.ds(pl.multiple_of(q * HD, HD), HD)
                u = _gelu(up_ref[rows, cols])
                v = _gelu(vp_ref[rows, cols])
                vh = (v * _row_rsqrt(v) * gvh).astype(BF16)
                mixed = jnp.dot(wth, vh, preferred_element_type=F32) + bcol
                out_ref[rows, cols] = (u * mixed).astype(BF16)
                return c_

            lax.fori_loop(0, nq, chunk, 0, unroll=True)
        for g, w in enumerate(POOL_WINDOWS):
            cols = slice(GD * g, GD * (g + 1))
            pooled = _pooled(z_ref, zh_ref, g, w, i, tm)
            yv = jnp.dot(pooled.astype(BF16), wp_ref[g], preferred_element_type=F32)
            out_ref[:, A_W + GD * g:A_W + GD * (g + 1)] = (yv * sc_ref[:, cols]).astype(BF16)

    hb = tm // HALO
    return _call(
        "mixer_fwd", (s // tm,), body, [proj, proj, proj, proj, wt, bst, gv, wpool, scale],
        [pl.BlockSpec((tm, A_W), lambda i: (i, 0)),
         pl.BlockSpec((tm, A_W), lambda i: (i, 1)),
         pl.BlockSpec((tm, A_W), lambda i: (i, 2)),
         pl.BlockSpec((HALO, A_W), lambda i: (jnp.maximum(i * hb - 1, 0), 2)),
         _full((HEADS, HD, HD)), _full((HEADS, HD, HD)), _full((1, A_W)), _full((4, GD, GD)), _full((1, A_W))],
        [jax.ShapeDtypeStruct((s, D), BF16)], [pl.BlockSpec((tm, D), lambda i: (i, 0))], vmem_mb=40, carry=carry)


def _mixer_bwd(proj, dmix, wt, wtt, bst, gv, wpool, scale, tm, carry):
    s = proj.shape[0]
    nb = s // tm
    nq = tm // HD
    hb = tm // HALO

    def body(ins, outs, scratch, ids):
        (up_ref, vp_ref, z_ref, zh_ref, doa_ref, dob_ref, dobh_ref, wt_ref, wtt_ref, bst_ref, gv_ref, wp_ref,
         sc_ref) = ins
        dproj_ref, dws_ref, dbs_ref, dgv_ref, dwp_ref, dsc_ref = outs
        dbfull = scratch[0]
        i = ids[0]
        last = i == nb - 1
        _zero_when(i == 0, dws_ref, dbfull, dgv_ref, dwp_ref, dsc_ref)

        for h in range(HEADS):
            cols = slice(HD * h, HD * (h + 1))
            gvh = gv_ref[:, cols]
            bcol = bst_ref[h]
            wth = wt_ref[h]
            wtth = wtt_ref[h]

            def chunk(q, c_):
                rows = pl.ds(pl.multiple_of(q * HD, HD), HD)
                u, du_dup = _gelu_and_grad(up_ref[rows, cols])
                v, dv_dvp = _gelu_and_grad(vp_ref[rows, cols])
                rv = _row_rsqrt(v)
                vn = v * rv
                vh = (vn * gvh).astype(BF16)
                mixed = jnp.dot(wth, vh, preferred_element_type=F32) + bcol
                doa = doa_ref[rows, cols].astype(F32)
                dmx = doa * u
                dmxb = dmx.astype(BF16)
                dbfull[h] += dmx
                dws_ref[h] += lax.dot_general(dmxb, vh, (NT, ((), ())), preferred_element_type=F32)
                dvh = jnp.dot(wtth, dmxb, preferred_element_type=F32)
                dgv_ref[:, cols] += jnp.sum(dvh * vn, axis=0, keepdims=True)
                dvn = dvh * gvh
                dv = rv * (dvn - vn * jnp.mean(dvn * vn, axis=-1, keepdims=True))
                dproj_ref[rows, cols] = (doa * mixed * du_dup).astype(BF16)
                dproj_ref[rows, A_W + HD * h:A_W + HD * (h + 1)] = (dv * dv_dvp).astype(BF16)
                return c_

            lax.fori_loop(0, nq, chunk, 0, unroll=True)

        t = i * tm + lax.broadcasted_iota(jnp.int32, (tm, 1), 0)
        th = (i + 1) * tm + lax.broadcasted_iota(jnp.int32, (HALO, 1), 0)
        for g, w in enumerate(POOL_WINDOWS):
            cols = slice(GD * g, GD * (g + 1))
            wpg = wp_ref[g]
            scg = sc_ref[:, cols]
            pb = _pooled(z_ref, zh_ref, g, w, i, tm).astype(BF16)
            ypre = jnp.dot(pb, wpg, preferred_element_type=F32)
            dob = dob_ref[:, cols].astype(F32)
            dsc_ref[:, cols] += jnp.sum(dob * ypre, axis=0, keepdims=True)
            dyb = (dob * scg).astype(BF16)
            dwp_ref[g] += lax.dot_general(pb, dyb, (TN, ((), ())), preferred_element_type=F32)
            dpo = lax.dot_general(dyb, wpg, (NT, ((), ())), preferred_element_type=F32)
            dyh = (dobh_ref[:, cols].astype(F32) * scg).astype(BF16)
            dpoh = lax.dot_general(dyh, wpg, (NT, ((), ())), preferred_element_type=F32)
            dpoh = jnp.where(last, 0.0, dpoh * _inv_count(th, w))
            ext = jnp.concatenate([dpo * _inv_count(t, w), dpoh], axis=0)
            dz = _window_sum(ext, w, False)[:tm, :] - dpo
            dproj_ref[:, 2 * A_W + GD * g:2 * A_W + GD * (g + 1)] = dz.astype(BF16)

        @pl.when(last)
        def _():
            r = lax.broadcasted_iota(jnp.int32, (HD, HD), 0)
            c = lax.broadcasted_iota(jnp.int32, (HD, HD), 1)
            for h in range(HEADS):
                dws_ref[h] = jnp.where(r >= c, dws_ref[h], 0.0)
                dbs_ref[h] = jnp.sum(dbfull[h], axis=-1, keepdims=True)

    return _call(
        "mixer_bwd", (nb,), body, [proj, proj, proj, proj, dmix, dmix, dmix, wt, wtt, bst, gv, wpool, scale],
        [pl.BlockSpec((tm, A_W), lambda i: (i, 0)),
         pl.BlockSpec((tm, A_W), lambda i: (i, 1)),
         pl.BlockSpec((tm, A_W), lambda i: (i, 2)),
         pl.BlockSpec((HALO, A_W), lambda i: (jnp.maximum(i * hb - 1, 0), 2)),
         pl.BlockSpec((tm, A_W), lambda i: (i, 0)),
         pl.BlockSpec((tm, A_W), lambda i: (i, 1)),
         pl.BlockSpec((HALO, A_W), lambda i: (jnp.minimum((i + 1) * hb, s // HALO - 1), 1)),
         _full((HEADS, HD, HD)), _full((HEADS, HD, HD)), _full((HEADS, HD, HD)), _full((1, A_W)),
         _full((4, GD, GD)), _full((1, A_W))],
        [jax.ShapeDtypeStruct((s, IN_W), BF16),
         jax.ShapeDtypeStruct((HEADS, HD, HD), F32),
         jax.ShapeDtypeStruct((HEADS, HD, 1), F32),
         jax.ShapeDtypeStruct((1, A_W), F32),
         jax.ShapeDtypeStruct((4, GD, GD), F32),
         jax.ShapeDtypeStruct((1, A_W), F32)],
        [pl.BlockSpec((tm, IN_W), lambda i: (i, 0)),
         _full((HEADS, HD, HD)), _full((HEADS, HD, 1)), _full((1, A_W)), _full((4, GD, GD)), _full((1, A_W))],
        [pltpu.VMEM((HEADS, HD, HD), F32)], vmem_mb=48, carry=carry)


def _inproj(h1, win_g, tm, carry):
    s = h1.shape[0]
    cw = IN_W // N_CHIPS

    def epi(p, cols, ids, extra, outs):
        outs[0][:, cols] = p

    return _matmul(
        "inproj", (N_CHIPS, s // tm), None, [h1, win_g],
        [pl.BlockSpec((tm, D), lambda j, i: (i, 0)), pl.BlockSpec((None, D, cw), lambda j, i: (j, 0, 0))],
        [jax.ShapeDtypeStruct((s, IN_W), F32)], [pl.BlockSpec((tm, cw), lambda j, i: (i, j))], NN, epi,
        nc=256, vmem_mb=32, carry=carry)


def _outproj(mixed, wout, x, g_ffn, tm, carry):
    s = x.shape[0]

    def epi(p, cols, ids, extra, outs):
        outs[0][:, cols] = extra[0][:, cols] + p

    def tail(ids, extra, outs):
        x2 = outs[0][...]
        outs[1][...] = (x2 * _row_rsqrt(x2) * extra[1][...]).astype(BF16)

    row = pl.BlockSpec((tm, D), lambda i: (i, 0))
    return _matmul(
        "outproj", (s // tm,), None, [mixed, wout, x, g_ffn], [row, _full((D, D)), row, _full((1, D))],
        [jax.ShapeDtypeStruct((s, D), F32), jax.ShapeDtypeStruct((s, D), BF16)], [row, row], NN, epi, tail,
        vmem_mb=48, carry=carry)


def _up(h2, wup_g, tm, tn, carry):
    s = h2.shape[0]
    per = D // tn

    def epi(p, cols, ids, extra, outs):
        a = jnp.maximum(p, 0.0)
        outs[0][:, cols] = (a * a).astype(BF16)

    return _matmul(
        "up", (D_FF // tn, s // tm), None, [h2, wup_g],
        [pl.BlockSpec((tm, D), lambda j, i: (i, 0)), pl.BlockSpec((None, D, tn), lambda j, i: (j // per, 0, j % per))],
        [jax.ShapeDtypeStruct((s, D_FF), BF16)], [pl.BlockSpec((tm, tn), lambda j, i: (i, j))], NN, epi,
        vmem_mb=40, carry=carry)


def _down(act, wdown, x2, tgt, g_final, tm, tk):
    s = x2.shape[0]

    def epi(acc, ids, extra, outs):
        x2_ref, t_ref, g_ref = extra
        dxb_ref, dgf_ref, loss_ref = outs
        g = g_ref[...]
        _zero_when(ids[0] == 0, dgf_ref, loss_ref)

        def block(rows):
            x3 = x2_ref[rows, :] + acc[rows, :]
            r = _row_rsqrt(x3)
            xh = x3 * r
            diff = xh * g - t_ref[rows, :]
            dy = diff * (1.0 / D)
            dxh = dy * g
            dx = r * (dxh - xh * jnp.mean(dxh * xh, axis=-1, keepdims=True))
            dxb_ref[rows, :] = dx.astype(BF16)
            dgf_ref[...] += jnp.sum(dy * xh, axis=0, keepdims=True)
            loss_ref[...] += jnp.sum(diff * diff, axis=0, keepdims=True)

        return block

    row = pl.BlockSpec((tm, D), lambda i, k: (i, 0))
    vec = pl.BlockSpec((1, D), lambda i, k: (0, 0))
    return _matmul(
        "down", (s // tm, D_FF // tk), 1, [act, wdown, x2, tgt, g_final],
        [pl.BlockSpec((tm, tk), lambda i, k: (i, k)), pl.BlockSpec((tk, D), lambda i, k: (k, 0)), row, row, vec],
        [jax.ShapeDtypeStruct((s, D), BF16), jax.ShapeDtypeStruct((1, D), F32), jax.ShapeDtypeStruct((1, D), F32)],
        [row, vec, vec], NN, epi, acc_shape=(tm, D), vmem_mb=56)


def _dact(dx3b, wdown, act, tm, tn, carry):
    s = dx3b.shape[0]

    def epi(p, cols, ids, extra, outs):
        outs[0][:, cols] = (p * (2.0 * jnp.sqrt(extra[0][:, cols].astype(F32)))).astype(BF16)

    tile = pl.BlockSpec((tm, tn), lambda j, i: (i, j))
    return _matmul(
        "dact", (D_FF // tn, s // tm), None, [dx3b, wdown, act],
        [pl.BlockSpec((tm, D), lambda j, i: (i, 0)), pl.BlockSpec((tn, D), lambda j, i: (j, 0)), tile],
        [jax.ShapeDtypeStruct((s, D_FF), BF16)], [tile], NT, epi, vmem_mb=40, carry=carry)


def _dweight(name, lhs, rhs, n_shards, rows, cols, tm, nc=512, carry=None):
    s = lhs.shape[0]

    def epi(p, cs, ids, extra, outs):
        outs[0][:, cs] = p.astype(BF16)

    return _matmul(
        name, (n_shards, rows // tm), None, [lhs, rhs],
        [pl.BlockSpec((s, tm), lambda j, i: (0, i)), pl.BlockSpec((s, cols), lambda j, i: (0, j))],
        [jax.ShapeDtypeStruct((n_shards, rows, cols), BF16)],
        [pl.BlockSpec((None, tm, cols), lambda j, i: (j, i, 0))], TN, epi, nc=nc, vmem_mb=56, carry=carry)


def _dh2(da, wup_g, x2, dx3, g_ffn, tm, tk, carry):
    s = x2.shape[0]
    per = D // tk

    def epi(acc, ids, extra, outs):
        x2_ref, dx3_ref, g_ref = extra
        g = g_ref[...]
        _zero_when(ids[0] == 0, outs[1])

        def block(rows):
            dx, dg = _norm_bwd(acc[rows, :], x2_ref[rows, :], g, dx3_ref[rows, :].astype(F32))
            outs[0][rows, :] = dx.astype(BF16)
            outs[1][...] += dg

        return block

    row = pl.BlockSpec((tm, D), lambda i, k: (i, 0))
    vec = pl.BlockSpec((1, D), lambda i, k: (0, 0))
    return _matmul(
        "dh2", (s // tm, D_FF // tk), 1, [da, wup_g, x2, dx3, g_ffn],
        [pl.BlockSpec((tm, tk), lambda i, k: (i, k)),
         pl.BlockSpec((None, D, tk), lambda i, k: (k // per, 0, k % per)), row, row, vec],
        [jax.ShapeDtypeStruct((s, D), BF16), jax.ShapeDtypeStruct((1, D), F32)],
        [row, vec], NT, epi, acc_shape=(tm, D), vmem_mb=56, carry=carry)


def _dmixed(dx2b, wout, tm, carry):
    s = dx2b.shape[0]

    def epi(p, cols, ids, extra, outs):
        outs[0][:, cols] = p.astype(BF16)

    row = pl.BlockSpec((tm, D), lambda i: (i, 0))
    return _matmul(
        "dmixed", (s // tm,), None, [dx2b, wout], [row, _full((D, D))],
        [jax.ShapeDtypeStruct((s, D), BF16)], [row], NT, epi, vmem_mb=40, carry=carry)


def _dh1(dproj, win_g, x, dx2, g_mix, tm, carry):
    s = x.shape[0]
    cw = IN_W // N_CHIPS

    def epi(acc, ids, extra, outs):
        x_ref, dx2_ref, g_ref = extra
        g = g_ref[...]
        _zero_when(ids[0] == 0, outs[1])

        def block(rows):
            dx, dg = _norm_bwd(acc[rows, :], x_ref[rows, :], g, dx2_ref[rows, :].astype(F32))
            outs[0][rows, :] = dx
            outs[1][...] += dg

        return block

    row = pl.BlockSpec((tm, D), lambda i, j: (i, 0))
    vec = pl.BlockSpec((1, D), lambda i, j: (0, 0))
    return _matmul(
        "dh1", (s // tm, N_CHIPS), 1, [dproj, win_g, x, dx2, g_mix],
        [pl.BlockSpec((tm, cw), lambda i, j: (i, j)), pl.BlockSpec((None, D, cw), lambda i, j: (j, 0, 0)),
         row, row, vec],
        [jax.ShapeDtypeStruct((s, D), F32), jax.ShapeDtypeStruct((1, D), F32)],
        [row, vec], NT, epi, acc_shape=(tm, D), vmem_mb=52, carry=carry)


def _cast_place(name, w, tr, carry=None):
    rows, cols = w.shape

    def body(ins, outs, scratch, ids):
        outs[0][...] = ins[0][...].astype(BF16)

    return _call(name, (rows // tr,), body, [w], [pl.BlockSpec((tr, cols), lambda r: (r, 0))],
                 [jax.ShapeDtypeStruct((N_CHIPS, rows, cols), BF16)],
                 [pl.BlockSpec((None, tr, cols), lambda r: (_my_chip(), r, 0))], vmem_mb=32, carry=carry)


def _small_allreduce(part):
    rows = part.shape[0]
    rh = rows // 2

    def body(p_ref, o_ref, sib_ref, slots, send_sems, recv_sems):
        x, y, c = _place()
        k = 2 * x + y
        sib = (x, y, 1 - c)
        half = pl.ds(pl.multiple_of(c * rh, 8), rh)
        cp = _remote(p_ref, sib_ref, send_sems.at[0], recv_sems.at[0], sib)
        cp.start()
        cp.wait()
        slots[k] = p_ref[half, :] + sib_ref[half, :]
        cps = []
        for j, (cx, cy) in enumerate(_other_chips(x, y)):
            cp = _remote(slots.at[k], slots.at[k], send_sems.at[1 + j], recv_sems.at[1 + j], (cx, cy, c))
            cp.start()
            cps.append(cp)
        for cp in cps:
            cp.wait()
        o_ref[half, :] = ((slots[0] + slots[1]) + slots[2]) + slots[3]
        cp = _remote(o_ref.at[half], o_ref.at[half], send_sems.at[4], recv_sems.at[4], sib)
        cp.start()
        cp.wait()

    vm = pl.BlockSpec(memory_space=pltpu.VMEM)
    return pl.pallas_call(
        body, name="small_allreduce", in_specs=[vm], out_specs=vm,
        out_shape=jax.ShapeDtypeStruct(part.shape, F32),
        scratch_shapes=[pltpu.VMEM(part.shape, F32), pltpu.VMEM((N_CHIPS, rh, LANES), F32),
                        pltpu.SemaphoreType.DMA((5,)), pltpu.SemaphoreType.DMA((5,))],
        compiler_params=pltpu.CompilerParams(vmem_limit_bytes=32 * MIB),
    )(part)


def _comm_only(name, carry):
    ns, nl = len(carry.srcs), len(carry.lands)
    lands_in = [l for l in carry.lands if not isinstance(l, jax.ShapeDtypeStruct)]
    assert len(lands_in) in (0, nl)

    def body(*refs):
        srcs = refs[:ns]
        lands = refs[ns + len(lands_in):ns + len(lands_in) + nl]
        ssem, rsem = refs[ns + len(lands_in) + nl:]
        carry.start(srcs, lands, ssem, rsem)
        if carry.middle is not None:
            carry.middle(srcs, lands, ssem, rsem)
        carry.finish(srcs, lands, ssem, rsem)

    return pl.pallas_call(
        body, name=name, in_specs=[ANY] * (ns + len(lands_in)), out_specs=[ANY] * nl,
        out_shape=[jax.ShapeDtypeStruct(l.shape, l.dtype) for l in carry.lands],
        input_output_aliases={ns + i: i for i in range(len(lands_in))},
        scratch_shapes=[pltpu.SemaphoreType.DMA((carry.n_sems,)), pltpu.SemaphoreType.DMA((carry.n_sems,))],
    )(*carry.srcs, *lands_in)


def _share_carry(grads):
    n = len(grads)

    def copies(srcs, lands, ssem, rsem):
        x, y, c = _place()
        return [_remote(lands[w].at[c], lands[w].at[c], ssem.at[w], rsem.at[w], (x, y, 1 - c)) for w in range(n)]

    def start(*a):
        for cp in copies(*a):
            cp.start()

    def finish(*a):
        for cp in copies(*a):
            cp.wait()

    return _Carry([], grads, n, start, finish)


def _add_sibling(dw, got, tr):
    _, _, rh, cols = dw.shape

    def body(ins, outs, scratch, ids):
        outs[0][...] = (ins[0][...].astype(F32) + ins[1][...].astype(F32)).astype(BF16)

    blk = pl.BlockSpec((None, tr, cols), lambda j, r: (j, r, 0))
    return _call("add_sibling", (N_CHIPS, rh // tr), body, [dw, got],
                 [pl.BlockSpec((None, None, tr, cols), lambda j, r: (j, _my_core(), r, 0)), blk],
                 [jax.ShapeDtypeStruct((N_CHIPS, rh, cols), BF16)], [blk], vmem_mb=32)[0]


def _add_chips(sums, got, tr):
    _, rh, cols = sums.shape

    def body(ins, outs, scratch, ids):
        b = ins[1][...].astype(F32)
        outs[0][...] = ((ins[0][...].astype(F32) + b[0]) + b[1]) + b[2]

    return _call("add_chips", (rh // tr,), body, [sums, got],
                 [pl.BlockSpec((None, tr, cols), lambda r: (_my_chip(), r, 0)),
                  pl.BlockSpec((3, tr, cols), lambda r: (0, r, 0))],
                 [jax.ShapeDtypeStruct((2, rh, cols), F32)],
                 [pl.BlockSpec((None, tr, cols), lambda r: (_my_core(), r, 0))], vmem_mb=32)[0]


def _adamw_math(w, g, m, v):
    m = ADAM_B1 * m + (1.0 - ADAM_B1) * g
    v = ADAM_B2 * v + (1.0 - ADAM_B2) * (g * g)
    m_hat = m / (1.0 - ADAM_B1 ** ADAM_STEP)
    v_hat = v / (1.0 - ADAM_B2 ** ADAM_STEP)
    delta = -ADAM_LR * (m_hat / (jnp.sqrt(v_hat) + ADAM_EPS) + ADAM_WD * w)
    return delta, m, v


def _adamw(name, w, g, m, v, tr, carry=None):
    rows, cols = w.shape

    def body(ins, outs, scratch, ids):
        g_val = ins[1][...]
        outs[0][...] = g_val
        outs[1][...], outs[2][...], outs[3][...] = _adamw_math(ins[0][...], g_val, ins[2][...], ins[3][...])

    blk = pl.BlockSpec((tr, cols), lambda r: (r, 0))
    return _call(name, (rows // tr,), body, [w, g, m, v], [blk] * 4,
                 [jax.ShapeDtypeStruct(w.shape, F32)] * 4, [blk] * 4, vmem_mb=40, carry=carry)


def _rows(a):
    return a.reshape(-1, LANES)


def kernel(x, g_mix, w_in, g_v, w_s, b_s, w_pool, pool_scale, w_out, g_ffn, w_up, w_down, g_final, loss_target, m_g_mix, m_w_in, m_g_v, m_w_s, m_b_s, m_w_pool, m_pool_scale, m_w_out, m_g_ffn, m_w_up, m_w_down, m_g_final, v_g_mix, v_w_in, v_g_v, v_w_s, v_b_s, v_w_pool, v_pool_scale, v_w_out, v_g_ffn, v_w_up, v_w_down, v_g_final):
    tm = 512
    xs = x[0]
    tgt = loss_target[0]
    chip = _my_chip()

    win_g = _cast_place("cast_w_in", w_in[0], 256)[0]
    wpool_g = _cast_place("cast_w_pool", w_pool[0].reshape(4 * 64, GD), 128)[0]
    wout_g = _cast_place("cast_w_out", w_out[0], 128)[0]
    wdown_g, win_g, wpool_g = _cast_place("cast_w_down", w_down[0], 256, _gather_near([win_g, wpool_g], (0, 1, 2)))
    wup_g, win_g, wpool_g = _cast_place("cast_w_up", w_up[0], 256, _gather_near([win_g, wpool_g], (1, 2, 2)))
    h1, win_g, wpool_g = _norm1(xs, g_mix, tm, _gather_far([win_g, wpool_g]))
    wpool_f = wpool_g.reshape(N_CHIPS, 4, 64, GD).transpose(1, 0, 2, 3).reshape(4, GD, GD)
    tril = jnp.tril(jnp.ones((HD, HD), dtype=bool))
    wt = jnp.where(tril[None], w_s[0], 0.0).astype(BF16)
    wtt = wt.transpose(0, 2, 1)
    bst = jnp.broadcast_to(b_s[0][:, :, None], (HEADS, HD, HD))
    gfin = g_final.reshape(1, D)

    proj, wout_g = _inproj(h1, win_g, tm, _gather_whole([wout_g]))
    mixed, wup_g = _mixer_fwd(proj, wt, bst, g_v, wpool_f, pool_scale, tm, _gather_near([wup_g]))
    wout_f = wout_g.reshape(D, D)
    x2, h2, wup_g = _outproj(mixed, wout_f, xs, g_ffn, 256, _gather_far([wup_g]))
    act, wdown_g = _up(h2, wup_g, tm, 2048, _gather_whole([wdown_g]))
    wdown_f = wdown_g.reshape(D_FF, D)
    dx3b, dgf, lossv = _down(act, wdown_f, x2, tgt, gfin, tm, 2048)

    halves = lambda dw, rows, cols: dw.reshape(N_CHIPS, 2, rows // (2 * N_CHIPS), cols)
    cw = IN_W // N_CHIPS
    dwdown = halves(_dweight("dw_down", act, dx3b, 1, D_FF, D, 512)[0], D_FF, D)
    da, sib_down = _dact(dx3b, wdown_f, act, tm, 2048, _sibling_carry(dwdown))
    sum_down = _add_sibling(dwdown, sib_down, 256)
    dwup, got_down = _dweight("dw_up", h2, da, N_CHIPS, D, D, 512, carry=_chips_carry(sum_down, None, (0, 3, 4)))
    dwup = halves(dwup, D_FF, D)
    dx2b, dgffn, got_down, sib_up = _dh2(da, wup_g, x2, dx3b, g_ffn, tm, 2048,
                                         _join(_chips_carry(sum_down, got_down, (3, 4, 4)), _sibling_carry(dwup)))
    half_down = _add_chips(sum_down, got_down, 256)
    sum_up = _add_sibling(dwup, sib_up, 256)
    dwout, got_up = _dweight("dw_out", mixed, dx2b, 1, D, D, 512, carry=_chips_carry(sum_up, None, (0, 1, 4)))
    dwout = halves(dwout, D, D)
    dmix, got_up, sib_out, half_down = _dmixed(
        dx2b, wout_f, 256,
        _join(_chips_carry(sum_up, got_up, (1, 2, 4)), _sibling_carry(dwout), _share_carry([half_down])))
    sum_out = _add_sibling(dwout, sib_out, 256)
    g_down, d_down, nm_down, nv_down = _adamw("adamw_down", w_down[0], half_down.reshape(D, D), m_w_down[0],
                                              v_w_down[0], 128)
    dproj, dws, dbs, dgv, dwp, dsc, got_up = _mixer_bwd(proj, dmix, wt, wtt, bst, g_v, wpool_f, pool_scale, tm,
                                                        _chips_carry(sum_up, got_up, (2, 4, 4)))
    half_up = _add_chips(sum_up, got_up, 256)
    dwin, got_out, half_up = _dweight("dw_in", h1, dproj, N_CHIPS, D, cw, 512, nc=256,
                                      carry=_join(_chips_carry(sum_out, None), _share_carry([half_up])))
    dwin = halves(dwin, N_CHIPS * D, cw)
    sib_in = _comm_only("sibling_in", _sibling_carry(dwin))[0]
    sum_in = _add_sibling(dwin, sib_in, 256)
    g_up, d_up, nm_up, nv_up = _adamw("adamw_up", w_up[0], half_up.reshape(D, D), m_w_up[0], v_w_up[0], 128)
    half_out = _add_chips(sum_out, got_out, 256)
    grad_x, dgmix, got_in, half_out = _dh1(dproj, win_g, xs, dx2b, g_mix, tm,
                                           _join(_chips_carry(sum_in, None), _share_carry([half_out])))
    half_in = _add_chips(sum_in, got_in, 256)
    half_in = _comm_only("share_half_in", _share_carry([half_in]))[0]
    g_out, d_out, nm_out, nv_out = _adamw("adamw_out", w_out[0], half_out.reshape(D // N_CHIPS, D), m_w_out[0],
                                          v_w_out[0], 128)
    g_in, d_in, nm_in, nv_in = _adamw("adamw_in", w_in[0], half_in.reshape(D, cw), m_w_in[0], v_w_in[0], 128)

    pieces = [dgmix, dgv, dws, dbs, dwp, dsc, dgffn, dgf, lossv, jnp.zeros((8 * LANES,), F32)]
    sizes = [p.size // LANES for p in pieces]
    tot = _small_allreduce(jnp.concatenate([_rows(p) for p in pieces], axis=0))
    offs = [sum(sizes[:i]) for i in range(len(sizes))]
    take = lambda i: tot[offs[i]:offs[i] + sizes[i]]
    s_gmix, s_gv, s_ws, s_bs, s_wp, s_sc, s_gffn, s_gf = [take(i) for i in range(8)]
    loss = (0.5 / D) * jnp.sum(take(8))
    s_wp_mine = lax.dynamic_slice_in_dim(s_wp.reshape(4, GD, GD), chip * 64, 64, axis=1)
    small_g = [s_gmix, s_gv, s_ws, s_bs, _rows(s_wp_mine), s_sc, s_gffn, s_gf]
    small_w = [g_mix, g_v, w_s, b_s, w_pool, pool_scale, g_ffn, g_final]
    small_m = [m_g_mix, m_g_v, m_w_s, m_b_s, m_w_pool, m_pool_scale, m_g_ffn, m_g_final]
    small_v = [v_g_mix, v_g_v, v_w_s, v_b_s, v_w_pool, v_pool_scale, v_g_ffn, v_g_final]
    cat = lambda parts: jnp.concatenate([_rows(p) for p in parts], axis=0)
    sg = cat(small_g)
    sg, sd, snm, snv = _adamw("adamw_small", cat(small_w), sg, cat(small_m), cat(small_v), sg.shape[0])
    ssz = [p.size // LANES for p in small_w]
    soff = [sum(ssz[:i]) for i in range(len(ssz))]
    split = lambda a: [a[soff[i]:soff[i] + ssz[i]].reshape(small_w[i].shape) for i in range(len(ssz))]
    gs, ds, nms, nvs = split(sg), split(sd), split(snm), split(snv)

    def ordered(small, w_in_, w_out_, w_up_, w_down_):
        return [small[0], w_in_[None], small[1], small[2], small[3], small[4], small[5], w_out_[None], small[6],
                w_up_[None], w_down_[None], small[7]]

    return (loss, grad_x[None],
            *ordered(gs, g_in, g_out, g_up, g_down),
            *ordered(ds, d_in, d_out, d_up, d_down),
            *ordered(nms, nm_in, nm_out, nm_up, nm_down),
            *ordered(nvs, nv_in, nv_out, nv_up, nv_down))
```

```python
import functools

import jax
import jax.numpy as jnp
from jax import lax
from jax.experimental import pallas as pl
from jax.experimental.pallas import tpu as pltpu

F32 = jnp.float32
BF16 = jnp.bfloat16
EPS = 1e-6
D = 2048
A_W = 1024
HEADS = 8
HD = 128
POOL_WINDOWS = (2, 4, 8, 16)
GD = 256
IN_W = 3072
D_FF = 8192
N_CHIPS = 4
HALO = 16
EPI_ROWS = 128
LANES = 128
MIB = 2 ** 20

ADAM_LR, ADAM_B1, ADAM_B2, ADAM_EPS, ADAM_WD, ADAM_STEP = 0.001, 0.9, 0.999, 1e-08, 0.01, 10

ANY = pl.BlockSpec(memory_space=pl.ANY)
MESH = pl.DeviceIdType.MESH

NN = ((1,), (0,))
NT = ((1,), (1,))
TN = ((0,), (0,))


def _place():
    return lax.axis_index("x"), lax.axis_index("y"), lax.axis_index("c")


def _my_chip():
    return 2 * lax.axis_index("x") + lax.axis_index("y")


def _my_core():
    return lax.axis_index("c")


def _other_chips(x, y):
    return [(1 - x, y), (x, 1 - y), (1 - x, 1 - y)]


def _remote(src, dst, send_sem, recv_sem, dev):
    return pltpu.make_async_remote_copy(src_ref=src, dst_ref=dst, send_sem=send_sem, recv_sem=recv_sem,
                                        device_id=dev, device_id_type=MESH)


class _SemView:
    def __init__(self, sems, base):
        self.sems, self.base = sems, base

    @property
    def at(self):
        return self

    def __getitem__(self, i):
        return self.sems.at[self.base + i]


class _Carry:
    def __init__(self, srcs, lands, n_sems, start, finish, middle=None):
        self.srcs, self.lands, self.n_sems, self.start, self.finish = list(srcs), list(lands), n_sems, start, finish
        self.middle = middle


def _join(*carries):
    def run(which):
        def go(srcs, lands, ssem, rsem):
            so = lo = qo = 0
            for c in carries:
                if getattr(c, which) is not None:
                    getattr(c, which)(srcs[so:so + len(c.srcs)], lands[lo:lo + len(c.lands)],
                                      _SemView(ssem, qo), _SemView(rsem, qo))
                so, lo, qo = so + len(c.srcs), lo + len(c.lands), qo + c.n_sems
        return go

    middle = run("middle") if any(c.middle is not None for c in carries) else None
    return _Carry([s for c in carries for s in c.srcs], [l for c in carries for l in c.lands],
                  sum(c.n_sems for c in carries), run("start"), run("finish"), middle)


GATHER_SEMS = 7


def _gather_copies(lands, ssem, rsem, part=(0, 1, 1)):
    x, y, c = _place()
    k, kx, ky, kd = 2 * x + y, 2 * (1 - x) + y, 2 * x + (1 - y), 2 * (1 - x) + (1 - y)
    to_x, to_y, sib = (1 - x, y, c), (x, 1 - y, c), (x, y, 1 - c)
    out = []
    for w, land in enumerate(lands):
        rh = land.shape[1] // 2
        rq = rh // 2
        half = pl.ds(pl.multiple_of(c * rh, 16), rh)
        quarters = [pl.ds(pl.multiple_of(c * rh + q * rq, 16), rq) for q in range(2)]
        pa, pb, pp = part
        sent = pl.ds(pl.multiple_of(c * rh + pa * rh // pp, 16), (pb - pa) * rh // pp)

        def cp(i, piece, dev, w=w):
            return _remote(piece, piece, ssem.at[GATHER_SEMS * w + i], rsem.at[GATHER_SEMS * w + i], dev)

        out.append(dict(
            ax=cp(0, land.at[k, sent], to_x), ay=cp(1, land.at[k, sent], to_y),
            rx=cp(2, land.at[kx, quarters[0]], to_y), ry=cp(3, land.at[ky, quarters[1]], to_x),
            fx=cp(4, land.at[kx, half], sib), fy=cp(5, land.at[ky, half], sib), fd=cp(6, land.at[kd, half], sib)))
    return out


def _gather_whole(gathered):
    def start(srcs, lands, ssem, rsem):
        for d in _gather_copies(lands, ssem, rsem):
            d["ax"].start()
            d["ay"].start()

    def middle(srcs, lands, ssem, rsem):
        for d in _gather_copies(lands, ssem, rsem):
            d["ax"].wait_recv()
            d["rx"].start()
            d["fx"].start()
            d["ay"].wait_recv()
            d["ry"].start()
            d["fy"].start()

    def finish(srcs, lands, ssem, rsem):
        for d in _gather_copies(lands, ssem, rsem):
            d["rx"].wait_recv()
            d["ry"].wait_recv()
            d["fd"].start()
            for name in ("fx", "fy", "fd"):
                d[name].wait_recv()
            for cp in d.values():
                cp.wait_send()

    return _Carry([], gathered, GATHER_SEMS * len(gathered), start, finish, middle)


def _gather_near(gathered, part=(0, 1, 1)):
    def start(srcs, lands, ssem, rsem):
        for d in _gather_copies(lands, ssem, rsem, part):
            d["ax"].start()
            d["ay"].start()

    def finish(srcs, lands, ssem, rsem):
        for d in _gather_copies(lands, ssem, rsem, part):
            for name in ("ax", "ay"):
                d[name].wait_recv()
                d[name].wait_send()

    return _Carry([], gathered, GATHER_SEMS * len(gathered), start, finish)


def _gather_far(gathered):
    def start(srcs, lands, ssem, rsem):
        for d in _gather_copies(lands, ssem, rsem):
            for name in ("rx", "ry", "fx", "fy"):
                d[name].start()

    def finish(srcs, lands, ssem, rsem):
        for d in _gather_copies(lands, ssem, rsem):
            d["rx"].wait_recv()
            d["ry"].wait_recv()
            d["fd"].start()
            for name in ("fx", "fy", "fd"):
                d[name].wait_recv()
            for name in ("rx", "ry", "fx", "fy", "fd"):
                d[name].wait_send()

    return _Carry([], gathered, GATHER_SEMS * len(gathered), start, finish)


def _sibling_carry(dw):
    def copies(srcs, lands, ssem, rsem):
        x, y, c = _place()
        piece = (lambda j: srcs[0].at[j, 1 - c]) if dw.ndim == 4 else (lambda j: srcs[0].at[j])
        return [_remote(piece(j), lands[0].at[j], ssem.at[j], rsem.at[j], (x, y, 1 - c)) for j in range(N_CHIPS)]

    def start(*a):
        for cp in copies(*a):
            cp.start()

    def finish(*a):
        for cp in copies(*a):
            cp.wait()

    return _Carry([dw], [jax.ShapeDtypeStruct((N_CHIPS,) + dw.shape[-2:], dw.dtype)], N_CHIPS, start, finish)


def _chips_carry(sums, land, part=(0, 1, 1)):
    a, b, p = part
    rh = sums.shape[1]
    rows = pl.ds(a * rh // p, (b - a) * rh // p)

    def copies(srcs, lands, ssem, rsem):
        x, y, c = _place()
        out = []
        for j, (cx, cy) in enumerate(_other_chips(x, y)):
            out.append(_remote(srcs[0].at[2 * cx + cy, rows], lands[0].at[j, rows], ssem.at[j], rsem.at[j],
                               (cx, cy, c)))
        return out

    def start(*a):
        for cp in copies(*a):
            cp.start()

    def finish(*a):
        for cp in copies(*a):
            cp.wait()

    if land is None:
        land = jax.ShapeDtypeStruct((3,) + sums.shape[1:], sums.dtype)
    return _Carry([sums], [land], 3, start, finish)


def _call(name, grid, body, ins, in_specs, out_shapes, out_specs, scratch=(), vmem_mb=48, carry=None):
    n_in, n_out, n_sc = len(ins), len(out_shapes), len(scratch)
    ins, in_specs = list(ins), list(in_specs)
    out_shapes, out_specs, scratch = list(out_shapes), list(out_specs), list(scratch)
    aliases = {}
    if carry is not None:
        ins += carry.srcs
        in_specs += [ANY] * len(carry.srcs)
        for land in carry.lands:
            if not isinstance(land, jax.ShapeDtypeStruct):
                aliases[len(ins)] = len(out_shapes)
                ins.append(land)
                in_specs.append(ANY)
                land = jax.ShapeDtypeStruct(land.shape, land.dtype)
            out_shapes.append(land)
            out_specs.append(ANY)
        scratch += [pltpu.SemaphoreType.DMA((carry.n_sems,)), pltpu.SemaphoreType.DMA((carry.n_sems,))]
    n_in_all, n_out_all = len(ins), len(out_shapes)
    ins = [pltpu.with_memory_space_constraint(a, pltpu.HBM) for a in ins]

    def kbody(*refs):
        in_refs, out_refs, sc = refs[:n_in_all], refs[n_in_all:n_in_all + n_out_all], refs[n_in_all + n_out_all:]
        ids = tuple(pl.program_id(a) for a in range(len(grid)))
        if carry is not None:
            first = functools.reduce(jnp.logical_and, [i == 0 for i in ids])
            last = functools.reduce(jnp.logical_and, [i == g - 1 for i, g in zip(ids, grid)])
            comm = (in_refs[n_in:n_in + len(carry.srcs)], out_refs[n_out:], sc[n_sc], sc[n_sc + 1])

            @pl.when(first)
            def _():
                carry.start(*comm)

            if carry.middle is not None:
                step, total = ids[0], grid[0]
                for i, g in zip(ids[1:], grid[1:]):
                    step, total = step * g + i, total * g

                @pl.when(step == (2 * total) // 3)
                def _():
                    carry.middle(*comm)

        body(in_refs[:n_in], out_refs[:n_out], sc[:n_sc], ids)
        if carry is not None:
            @pl.when(last)
            def _():
                carry.finish(*comm)

    return pl.pallas_call(
        kbody, name=name, grid=grid, in_specs=in_specs, out_specs=out_specs,
        out_shape=[pltpu.HBM(o.shape, o.dtype) for o in out_shapes],
        scratch_shapes=scratch, input_output_aliases=aliases,
        compiler_params=pltpu.CompilerParams(dimension_semantics=("arbitrary",) * len(grid),
                                             vmem_limit_bytes=vmem_mb * MIB),
    )(*ins)


def _matmul(name, grid, kaxis, ins, in_specs, out_shapes, out_specs, dims, epi, tail=None, acc_shape=None, nc=512,
            vmem_mb=48, carry=None):
    nk = grid[kaxis] if kaxis is not None else 1

    def body(in_refs, out_refs, scratch, ids):
        a_ref, b_ref = in_refs[0], in_refs[1]
        n = b_ref.shape[0] if dims == NT else b_ref.shape[1]

        def prod(a, c0):
            b = b_ref[c0:c0 + nc, :] if dims == NT else b_ref[:, c0:c0 + nc]
            return lax.dot_general(a, b, (dims, ((), ())), preferred_element_type=F32)

        if kaxis is None:
            a = a_ref[...]
            for c0 in range(0, n, nc):
                epi(prod(a, c0), slice(c0, c0 + nc), ids, in_refs[2:], out_refs)
            if tail is not None:
                tail(ids, in_refs[2:], out_refs)
        else:
            acc = scratch[0]
            tm = acc.shape[0]
            last = ids[kaxis] == nk - 1
            _zero_when(ids[kaxis] == 0, acc)

            @pl.when(jnp.logical_not(last))
            def _():
                a = a_ref[...]
                for c0 in range(0, n, nc):
                    acc[:, c0:c0 + nc] += prod(a, c0)

            @pl.when(last)
            def _():
                block = epi(acc, ids, in_refs[2:], out_refs)
                for r0 in range(0, tm, tm // 2):
                    a = a_ref[r0:r0 + tm // 2, :]
                    for c0 in range(0, n, nc):
                        acc[r0:r0 + tm // 2, c0:c0 + nc] += prod(a, c0)
                    for q0 in range(r0, r0 + tm // 2, EPI_ROWS):
                        block(slice(q0, q0 + EPI_ROWS))

    return _call(name, grid, body, ins, in_specs, out_shapes, out_specs,
                 [pltpu.VMEM(acc_shape, F32)] if kaxis is not None else [], vmem_mb, carry)


def _row_rsqrt(xf):
    return lax.rsqrt(jnp.mean(xf * xf, axis=-1, keepdims=True) + EPS)


def _norm_bwd(dh, xf, g, resid):
    r = _row_rsqrt(xf)
    xh = xf * r
    dg = jnp.sum(dh * xh, axis=0, keepdims=True)
    dxh = dh * g
    dx = resid + r * (dxh - xh * jnp.mean(dxh * xh, axis=-1, keepdims=True))
    return dx, dg


def _zero_when(first, *refs):
    @pl.when(first)
    def _():
        for ref in refs:
            ref[...] = jnp.zeros_like(ref)


_GELU_K = 0.7978845608028654
_GELU_C = 0.044715


def _gelu(x):
    t = jnp.tanh(_GELU_K * (x + _GELU_C * x * x * x))
    return 0.5 * x * (1.0 + t)


def _gelu_and_grad(x):
    x2 = x * x
    t = jnp.tanh(_GELU_K * (x + _GELU_C * x * x2))
    g = 0.5 * x * (1.0 + t)
    dg = 0.5 * (1.0 + t) + 0.5 * x * (1.0 - t * t) * (_GELU_K * (1.0 + 3.0 * _GELU_C * x2))
    return g, dg


def _window_sum(ext, w, causal):
    n = ext.shape[0]
    s, d = ext, 1
    while d < w:
        s = s + pltpu.roll(s, d if causal else n - d, 0)
        d *= 2
    return s


def _inv_count(t, w):
    return 1.0 / jnp.minimum(t + 1, w).astype(F32)


def _pooled(z_ref, zh_ref, g, w, i, tm):
    cols = slice(GD * g, GD * (g + 1))
    zb = z_ref[:, cols]
    zh = jnp.where(i > 0, zh_ref[:, cols], 0.0)
    ext = jnp.concatenate([zh, zb], axis=0)
    s = _window_sum(ext, w, True)[HALO:, :]
    t = i * tm + lax.broadcasted_iota(jnp.int32, (tm, 1), 0)
    return s * _inv_count(t, w) - zb


def _full(shape, n_axes=1):
    return pl.BlockSpec(shape, lambda *ids: (0,) * len(shape))


def _norm1(x, g, tm, carry):
    s = x.shape[0]

    def body(ins, outs, scratch, ids):
        xf = ins[0][...]
        outs[0][...] = (xf * _row_rsqrt(xf) * ins[1][...]).astype(BF16)

    row = pl.BlockSpec((tm, D), lambda i: (i, 0))
    return _call("norm1", (s // tm,), body, [x, g], [row, _full((1, D))],
                 [jax.ShapeDtypeStruct((s, D), BF16)], [row], vmem_mb=32, carry=carry)


def _mixer_fwd(proj, wt, bst, gv, wpool, scale, tm, carry):
    s = proj.shape[0]
    nq = tm // HD

    def body(ins, outs, scratch, ids):
        up_ref, vp_ref, z_ref, zh_ref, wt_ref, bst_ref, gv_ref, wp_ref, sc_ref = ins
        out_ref = outs[0]
        i = ids[0]
        for h in range(HEADS):
            cols = slice(HD * h, HD * (h + 1))
            gvh = gv_ref[:, cols]
            bcol = bst_ref[h]
            wth = wt_ref[h]

            def chunk(q, c_):
                rows = pl.ds(pl.multiple_of(q * HD, HD), HD)
                u = _gelu(up_ref[rows, cols])
                v = _gelu(vp_ref[rows, cols])
                vh = (v * _row_rsqrt(v) * gvh).astype(BF16)
                mixed = jnp.dot(wth, vh, preferred_element_type=F32) + bcol
                out_ref[rows, cols] = (u * mixed).astype(BF16)
                return c_

            lax.fori_loop(0, nq, chunk, 0, unroll=True)
        for g, w in enumerate(POOL_WINDOWS):
            cols = slice(GD * g, GD * (g + 1))
            pooled = _pooled(z_ref, zh_ref, g, w, i, tm)
            yv = jnp.dot(pooled.astype(BF16), wp_ref[g], preferred_element_type=F32)
            out_ref[:, A_W + GD * g:A_W + GD * (g + 1)] = (yv * sc_ref[:, cols]).astype(BF16)

    hb = tm // HALO
    return _call(
        "mixer_fwd", (s // tm,), body, [proj, proj, proj, proj, wt, bst, gv, wpool, scale],
        [pl.BlockSpec((tm, A_W), lambda i: (i, 0)),
         pl.BlockSpec((tm, A_W), lambda i: (i, 1)),
         pl.BlockSpec((tm, A_W), lambda i: (i, 2)),
         pl.BlockSpec((HALO, A_W), lambda i: (jnp.maximum(i * hb - 1, 0), 2)),
         _full((HEADS, HD, HD)), _full((HEADS, HD, HD)), _full((1, A_W)), _full((4, GD, GD)), _full((1, A_W))],
        [jax.ShapeDtypeStruct((s, D), BF16)], [pl.BlockSpec((tm, D), lambda i: (i, 0))], vmem_mb=40, carry=carry)


def _mixer_bwd(proj, dmix, wt, wtt, bst, gv, wpool, scale, tm, carry):
    s = proj.shape[0]
    nb = s // tm
    nq = tm // HD
    hb = tm // HALO

    def body(ins, outs, scratch, ids):
        (up_ref, vp_ref, z_ref, zh_ref, doa_ref, dob_ref, dobh_ref, wt_ref, wtt_ref, bst_ref, gv_ref, wp_ref,
         sc_ref) = ins
        dproj_ref, dws_ref, dbs_ref, dgv_ref, dwp_ref, dsc_ref = outs
        dbfull = scratch[0]
        i = ids[0]
        last = i == nb - 1
        _zero_when(i == 0, dws_ref, dbfull, dgv_ref, dwp_ref, dsc_ref)

        for h in range(HEADS):
            cols = slice(HD * h, HD * (h + 1))
            gvh = gv_ref[:, cols]
            bcol = bst_ref[h]
            wth = wt_ref[h]
            wtth = wtt_ref[h]

            def chunk(q, c_):
                rows = pl.ds(pl.multiple_of(q * HD, HD), HD)
                u, du_dup = _gelu_and_grad(up_ref[rows, cols])
                v, dv_dvp = _gelu_and_grad(vp_ref[rows, cols])
                rv = _row_rsqrt(v)
                vn = v * rv
                vh = (vn * gvh).astype(BF16)
                mixed = jnp.dot(wth, vh, preferred_element_type=F32) + bcol
                doa = doa_ref[rows, cols].astype(F32)
                dmx = doa * u
                dmxb = dmx.astype(BF16)
                dbfull[h] += dmx
                dws_ref[h] += lax.dot_general(dmxb, vh, (NT, ((), ())), preferred_element_type=F32)
                dvh = jnp.dot(wtth, dmxb, preferred_element_type=F32)
                dgv_ref[:, cols] += jnp.sum(dvh * vn, axis=0, keepdims=True)
                dvn = dvh * gvh
                dv = rv * (dvn - vn * jnp.mean(dvn * vn, axis=-1, keepdims=True))
                dproj_ref[rows, cols] = (doa * mixed * du_dup).astype(BF16)
                dproj_ref[rows, A_W + HD * h:A_W + HD * (h + 1)] = (dv * dv_dvp).astype(BF16)
                return c_

            lax.fori_loop(0, nq, chunk, 0, unroll=True)

        t = i * tm + lax.broadcasted_iota(jnp.int32, (tm, 1), 0)
        th = (i + 1) * tm + lax.broadcasted_iota(jnp.int32, (HALO, 1), 0)
        for g, w in enumerate(POOL_WINDOWS):
            cols = slice(GD * g, GD * (g + 1))
            wpg = wp_ref[g]
            scg = sc_ref[:, cols]
            pb = _pooled(z_ref, zh_ref, g, w, i, tm).astype(BF16)
            ypre = jnp.dot(pb, wpg, preferred_element_type=F32)
            dob = dob_ref[:, cols].astype(F32)
            dsc_ref[:, cols] += jnp.sum(dob * ypre, axis=0, keepdims=True)
            dyb = (dob * scg).astype(BF16)
            dwp_ref[g] += lax.dot_general(pb, dyb, (TN, ((), ())), preferred_element_type=F32)
            dpo = lax.dot_general(dyb, wpg, (NT, ((), ())), preferred_element_type=F32)
            dyh = (dobh_ref[:, cols].astype(F32) * scg).astype(BF16)
            dpoh = lax.dot_general(dyh, wpg, (NT, ((), ())), preferred_element_type=F32)
            dpoh = jnp.where(last, 0.0, dpoh * _inv_count(th, w))
            ext = jnp.concatenate([dpo * _inv_count(t, w), dpoh], axis=0)
            dz = _window_sum(ext, w, False)[:tm, :] - dpo
            dproj_ref[:, 2 * A_W + GD * g:2 * A_W + GD * (g + 1)] = dz.astype(BF16)

        @pl.when(last)
        def _():
            r = lax.broadcasted_iota(jnp.int32, (HD, HD), 0)
            c = lax.broadcasted_iota(jnp.int32, (HD, HD), 1)
            for h in range(HEADS):
                dws_ref[h] = jnp.where(r >= c, dws_ref[h], 0.0)
                dbs_ref[h] = jnp.sum(dbfull[h], axis=-1, keepdims=True)

    return _call(
        "mixer_bwd", (nb,), body, [proj, proj, proj, proj, dmix, dmix, dmix, wt, wtt, bst, gv, wpool, scale],
        [pl.BlockSpec((tm, A_W), lambda i: (i, 0)),
         pl.BlockSpec((tm, A_W), lambda i: (i, 1)),
         pl.BlockSpec((tm, A_W), lambda i: (i, 2)),
         pl.BlockSpec((HALO, A_W), lambda i: (jnp.maximum(i * hb - 1, 0), 2)),
         pl.BlockSpec((tm, A_W), lambda i: (i, 0)),
         pl.BlockSpec((tm, A_W), lambda i: (i, 1)),
         pl.BlockSpec((HALO, A_W), lambda i: (jnp.minimum((i + 1) * hb, s // HALO - 1), 1)),
         _full((HEADS, HD, HD)), _full((HEADS, HD, HD)), _full((HEADS, HD, HD)), _full((1, A_W)),
         _full((4, GD, GD)), _full((1, A_W))],
        [jax.ShapeDtypeStruct((s, IN_W), BF16),
         jax.ShapeDtypeStruct((HEADS, HD, HD), F32),
         jax.ShapeDtypeStruct((HEADS, HD, 1), F32),
         jax.ShapeDtypeStruct((1, A_W), F32),
         jax.ShapeDtypeStruct((4, GD, GD), F32),
         jax.ShapeDtypeStruct((1, A_W), F32)],
        [pl.BlockSpec((tm, IN_W), lambda i: (i, 0)),
         _full((HEADS, HD, HD)), _full((HEADS, HD, 1)), _full((1, A_W)), _full((4, GD, GD)), _full((1, A_W))],
        [pltpu.VMEM((HEADS, HD, HD), F32)], vmem_mb=48, carry=carry)


def _inproj(h1, win_g, tm, carry):
    s = h1.shape[0]
    cw = IN_W // N_CHIPS

    def epi(p, cols, ids, extra, outs):
        outs[0][:, cols] = p

    return _matmul(
        "inproj", (N_CHIPS, s // tm), None, [h1, win_g],
        [pl.BlockSpec((tm, D), lambda j, i: (i, 0)), pl.BlockSpec((None, D, cw), lambda j, i: (j, 0, 0))],
        [jax.ShapeDtypeStruct((s, IN_W), F32)], [pl.BlockSpec((tm, cw), lambda j, i: (i, j))], NN, epi,
        nc=256, vmem_mb=32, carry=carry)


def _outproj(mixed, wout, x, g_ffn, tm, carry):
    s = x.shape[0]

    def epi(p, cols, ids, extra, outs):
        outs[0][:, cols] = extra[0][:, cols] + p

    def tail(ids, extra, outs):
        x2 = outs[0][...]
        outs[1][...] = (x2 * _row_rsqrt(x2) * extra[1][...]).astype(BF16)

    row = pl.BlockSpec((tm, D), lambda i: (i, 0))
    return _matmul(
        "outproj", (s // tm,), None, [mixed, wout, x, g_ffn], [row, _full((D, D)), row, _full((1, D))],
        [jax.ShapeDtypeStruct((s, D), F32), jax.ShapeDtypeStruct((s, D), BF16)], [row, row], NN, epi, tail,
        vmem_mb=48, carry=carry)


def _up(h2, wup_g, tm, tn, carry):
    s = h2.shape[0]
    per = D // tn

    def epi(p, cols, ids, extra, outs):
        a = jnp.maximum(p, 0.0)
        outs[0][:, cols] = (a * a).astype(BF16)

    return _matmul(
        "up", (D_FF // tn, s // tm), None, [h2, wup_g],
        [pl.BlockSpec((tm, D), lambda j, i: (i, 0)), pl.BlockSpec((None, D, tn), lambda j, i: (j // per, 0, j % per))],
        [jax.ShapeDtypeStruct((s, D_FF), BF16)], [pl.BlockSpec((tm, tn), lambda j, i: (i, j))], NN, epi,
        vmem_mb=48, carry=carry)


def _down(act, wdown, x2, tgt, g_final, tm, tk):
    s = x2.shape[0]

    def epi(acc, ids, extra, outs):
        x2_ref, t_ref, g_ref = extra
        dxb_ref, dgf_ref, loss_ref = outs
        g = g_ref[...]
        _zero_when(ids[0] == 0, dgf_ref, loss_ref)

        def block(rows):
            x3 = x2_ref[rows, :] + acc[rows, :]
            r = _row_rsqrt(x3)
            xh = x3 * r
            diff = xh * g - t_ref[rows, :]
            dy = diff * (1.0 / D)
            dxh = dy * g
            dx = r * (dxh - xh * jnp.mean(dxh * xh, axis=-1, keepdims=True))
            dxb_ref[rows, :] = dx.astype(BF16)
            dgf_ref[...] += jnp.sum(dy * xh, axis=0, keepdims=True)
            loss_ref[...] += jnp.sum(diff * diff, axis=0, keepdims=True)

        return block

    row = pl.BlockSpec((tm, D), lambda i, k: (i, 0))
    vec = pl.BlockSpec((1, D), lambda i, k: (0, 0))
    return _matmul(
        "down", (s // tm, D_FF // tk), 1, [act, wdown, x2, tgt, g_final],
        [pl.BlockSpec((tm, tk), lambda i, k: (i, k)), pl.BlockSpec((tk, D), lambda i, k: (k, 0)), row, row, vec],
        [jax.ShapeDtypeStruct((s, D), BF16), jax.ShapeDtypeStruct((1, D), F32), jax.ShapeDtypeStruct((1, D), F32)],
        [row, vec, vec], NN, epi, acc_shape=(tm, D), vmem_mb=56)


def _dact(dx3b, wdown, act, tm, tn, carry):
    s = dx3b.shape[0]

    def epi(p, cols, ids, extra, outs):
        outs[0][:, cols] = (p * (2.0 * jnp.sqrt(extra[0][:, cols].astype(F32)))).astype(BF16)

    tile = pl.BlockSpec((tm, tn), lambda j, i: (i, j))
    return _matmul(
        "dact", (D_FF // tn, s // tm), None, [dx3b, wdown, act],
        [pl.BlockSpec((tm, D), lambda j, i: (i, 0)), pl.BlockSpec((tn, D), lambda j, i: (j, 0)), tile],
        [jax.ShapeDtypeStruct((s, D_FF), BF16)], [tile], NT, epi, vmem_mb=56, carry=carry)


def _dweight(name, lhs, rhs, n_shards, rows, cols, tm, nc=512, carry=None, half=None):
    s = lhs.shape[0]
    if half is not None:
        rows = rows // 2
    nblk = rows // tm

    def first_block():
        if half is None:
            return 0
        return (_my_core() if half == "own" else 1 - _my_core()) * nblk

    def epi(p, cs, ids, extra, outs):
        outs[0][:, cs] = p.astype(BF16)

    return _matmul(
        name, (n_shards, nblk), None, [lhs, rhs],
        [pl.BlockSpec((s, tm), lambda j, i: (0, first_block() + i)), pl.BlockSpec((s, cols), lambda j, i: (0, j))],
        [jax.ShapeDtypeStruct((n_shards, rows, cols), BF16)],
        [pl.BlockSpec((None, tm, cols), lambda j, i: (j, i, 0))], TN, epi, nc=nc, vmem_mb=56, carry=carry)


def _dh2(da, wup_g, x2, dx3, g_ffn, tm, tk, carry):
    s = x2.shape[0]
    per = D // tk

    def epi(acc, ids, extra, outs):
        x2_ref, dx3_ref, g_ref = extra
        g = g_ref[...]
        _zero_when(ids[0] == 0, outs[1])

        def block(rows):
            dx, dg = _norm_bwd(acc[rows, :], x2_ref[rows, :], g, dx3_ref[rows, :].astype(F32))
            outs[0][rows, :] = dx.astype(BF16)
            outs[1][...] += dg

        return block

    row = pl.BlockSpec((tm, D), lambda i, k: (i, 0))
    vec = pl.BlockSpec((1, D), lambda i, k: (0, 0))
    return _matmul(
        "dh2", (s // tm, D_FF // tk), 1, [da, wup_g, x2, dx3, g_ffn],
        [pl.BlockSpec((tm, tk), lambda i, k: (i, k)),
         pl.BlockSpec((None, D, tk), lambda i, k: (k // per, 0, k % per)), row, row, vec],
        [jax.ShapeDtypeStruct((s, D), BF16), jax.ShapeDtypeStruct((1, D), F32)],
        [row, vec], NT, epi, acc_shape=(tm, D), vmem_mb=56, carry=carry)


def _dmixed(dx2b, wout, tm, carry):
    s = dx2b.shape[0]

    def epi(p, cols, ids, extra, outs):
        outs[0][:, cols] = p.astype(BF16)

    row = pl.BlockSpec((tm, D), lambda i: (i, 0))
    return _matmul(
        "dmixed", (s // tm,), None, [dx2b, wout], [row, _full((D, D))],
        [jax.ShapeDtypeStruct((s, D), BF16)], [row], NT, epi, vmem_mb=40, carry=carry)


def _dh1(dproj, win_g, x, dx2, g_mix, tm, carry):
    s = x.shape[0]
    cw = IN_W // N_CHIPS

    def epi(acc, ids, extra, outs):
        x_ref, dx2_ref, g_ref = extra
        g = g_ref[...]
        _zero_when(ids[0] == 0, outs[1])

        def block(rows):
            dx, dg = _norm_bwd(acc[rows, :], x_ref[rows, :], g, dx2_ref[rows, :].astype(F32))
            outs[0][rows, :] = dx
            outs[1][...] += dg

        return block

    row = pl.BlockSpec((tm, D), lambda i, j: (i, 0))
    vec = pl.BlockSpec((1, D), lambda i, j: (0, 0))
    return _matmul(
        "dh1", (s // tm, N_CHIPS), 1, [dproj, win_g, x, dx2, g_mix],
        [pl.BlockSpec((tm, cw), lambda i, j: (i, j)), pl.BlockSpec((None, D, cw), lambda i, j: (j, 0, 0)),
         row, row, vec],
        [jax.ShapeDtypeStruct((s, D), F32), jax.ShapeDtypeStruct((1, D), F32)],
        [row, vec], NT, epi, acc_shape=(tm, D), vmem_mb=52, carry=carry)


def _cast_place(name, w, tr, carry=None):
    rows, cols = w.shape

    def body(ins, outs, scratch, ids):
        outs[0][...] = ins[0][...].astype(BF16)

    return _call(name, (rows // tr,), body, [w], [pl.BlockSpec((tr, cols), lambda r: (r, 0))],
                 [jax.ShapeDtypeStruct((N_CHIPS, rows, cols), BF16)],
                 [pl.BlockSpec((None, tr, cols), lambda r: (_my_chip(), r, 0))], vmem_mb=32, carry=carry)


def _small_allreduce(part):
    rows = part.shape[0]
    rh = rows // 2

    def body(p_ref, o_ref, sib_ref, slots, send_sems, recv_sems):
        x, y, c = _place()
        k = 2 * x + y
        sib = (x, y, 1 - c)
        half = pl.ds(pl.multiple_of(c * rh, 8), rh)
        cp = _remote(p_ref, sib_ref, send_sems.at[0], recv_sems.at[0], sib)
        cp.start()
        cp.wait()
        slots[k] = p_ref[half, :] + sib_ref[half, :]
        cps = []
        for j, (cx, cy) in enumerate(_other_chips(x, y)):
            cp = _remote(slots.at[k], slots.at[k], send_sems.at[1 + j], recv_sems.at[1 + j], (cx, cy, c))
            cp.start()
            cps.append(cp)
        for cp in cps:
            cp.wait()
        o_ref[half, :] = ((slots[0] + slots[1]) + slots[2]) + slots[3]
        cp = _remote(o_ref.at[half], o_ref.at[half], send_sems.at[4], recv_sems.at[4], sib)
        cp.start()
        cp.wait()

    vm = pl.BlockSpec(memory_space=pltpu.VMEM)
    return pl.pallas_call(
        body, name="small_allreduce", in_specs=[vm], out_specs=vm,
        out_shape=jax.ShapeDtypeStruct(part.shape, F32),
        scratch_shapes=[pltpu.VMEM(part.shape, F32), pltpu.VMEM((N_CHIPS, rh, LANES), F32),
                        pltpu.SemaphoreType.DMA((5,)), pltpu.SemaphoreType.DMA((5,))],
        compiler_params=pltpu.CompilerParams(vmem_limit_bytes=32 * MIB),
    )(part)


def _comm_only(name, carry):
    ns, nl = len(carry.srcs), len(carry.lands)
    lands_in = [l for l in carry.lands if not isinstance(l, jax.ShapeDtypeStruct)]
    assert len(lands_in) in (0, nl)

    def body(*refs):
        srcs = refs[:ns]
        lands = refs[ns + len(lands_in):ns + len(lands_in) + nl]
        ssem, rsem = refs[ns + len(lands_in) + nl:]
        carry.start(srcs, lands, ssem, rsem)
        if carry.middle is not None:
            carry.middle(srcs, lands, ssem, rsem)
        carry.finish(srcs, lands, ssem, rsem)

    return pl.pallas_call(
        body, name=name, in_specs=[ANY] * (ns + len(lands_in)), out_specs=[ANY] * nl,
        out_shape=[jax.ShapeDtypeStruct(l.shape, l.dtype) for l in carry.lands],
        input_output_aliases={ns + i: i for i in range(len(lands_in))},
        scratch_shapes=[pltpu.SemaphoreType.DMA((carry.n_sems,)), pltpu.SemaphoreType.DMA((carry.n_sems,))],
    )(*carry.srcs, *lands_in)


def _share_carry(grads):
    n = len(grads)

    def copies(srcs, lands, ssem, rsem):
        x, y, c = _place()
        return [_remote(lands[w].at[c], lands[w].at[c], ssem.at[w], rsem.at[w], (x, y, 1 - c)) for w in range(n)]

    def start(*a):
        for cp in copies(*a):
            cp.start()

    def finish(*a):
        for cp in copies(*a):
            cp.wait()

    return _Carry([], grads, n, start, finish)


def _add_sibling(dw, got, tr):
    rh, cols = dw.shape[-2:]

    def body(ins, outs, scratch, ids):
        outs[0][...] = (ins[0][...].astype(F32) + ins[1][...].astype(F32)).astype(BF16)

    blk = pl.BlockSpec((None, tr, cols), lambda j, r: (j, r, 0))
    mine = pl.BlockSpec((None, None, tr, cols), lambda j, r: (j, _my_core(), r, 0)) if dw.ndim == 4 else blk
    return _call("add_sibling", (N_CHIPS, rh // tr), body, [dw, got], [mine, blk],
                 [jax.ShapeDtypeStruct((N_CHIPS, rh, cols), BF16)], [blk], vmem_mb=32)[0]


def _add_chips(sums, got, tr):
    _, rh, cols = sums.shape

    def body(ins, outs, scratch, ids):
        b = ins[1][...].astype(F32)
        outs[0][...] = ((ins[0][...].astype(F32) + b[0]) + b[1]) + b[2]

    return _call("add_chips", (rh // tr,), body, [sums, got],
                 [pl.BlockSpec((None, tr, cols), lambda r: (_my_chip(), r, 0)),
                  pl.BlockSpec((3, tr, cols), lambda r: (0, r, 0))],
                 [jax.ShapeDtypeStruct((2, rh, cols), F32)],
                 [pl.BlockSpec((None, tr, cols), lambda r: (_my_core(), r, 0))], vmem_mb=32)[0]


def _adamw_math(w, g, m, v):
    m = ADAM_B1 * m + (1.0 - ADAM_B1) * g
    v = ADAM_B2 * v + (1.0 - ADAM_B2) * (g * g)
    m_hat = m / (1.0 - ADAM_B1 ** ADAM_STEP)
    v_hat = v / (1.0 - ADAM_B2 ** ADAM_STEP)
    delta = -ADAM_LR * (m_hat / (jnp.sqrt(v_hat) + ADAM_EPS) + ADAM_WD * w)
    return delta, m, v


def _adamw(name, w, g, m, v, tr, carry=None):
    rows, cols = w.shape

    def body(ins, outs, scratch, ids):
        g_val = ins[1][...]
        outs[0][...] = g_val
        outs[1][...], outs[2][...], outs[3][...] = _adamw_math(ins[0][...], g_val, ins[2][...], ins[3][...])

    blk = pl.BlockSpec((tr, cols), lambda r: (r, 0))
    return _call(name, (rows // tr,), body, [w, g, m, v], [blk] * 4,
                 [jax.ShapeDtypeStruct(w.shape, F32)] * 4, [blk] * 4, vmem_mb=40, carry=carry)


def _rows(a):
    return a.reshape(-1, LANES)


def kernel(x, g_mix, w_in, g_v, w_s, b_s, w_pool, pool_scale, w_out, g_ffn, w_up, w_down, g_final, loss_target, m_g_mix, m_w_in, m_g_v, m_w_s, m_b_s, m_w_pool, m_pool_scale, m_w_out, m_g_ffn, m_w_up, m_w_down, m_g_final, v_g_mix, v_w_in, v_g_v, v_w_s, v_b_s, v_w_pool, v_pool_scale, v_w_out, v_g_ffn, v_w_up, v_w_down, v_g_final):
    tm = 512
    xs = x[0]
    tgt = loss_target[0]
    chip = _my_chip()

    win_g = _cast_place("cast_w_in", w_in[0], 256)[0]
    wpool_g = _cast_place("cast_w_pool", w_pool[0].reshape(4 * 64, GD), 128)[0]
    wout_g = _cast_place("cast_w_out", w_out[0], 128)[0]
    wdown_g, win_g, wpool_g = _cast_place("cast_w_down", w_down[0], 256, _gather_near([win_g, wpool_g], (0, 1, 2)))
    wup_g, win_g, wpool_g = _cast_place("cast_w_up", w_up[0], 256, _gather_near([win_g, wpool_g], (1, 2, 2)))
    h1, win_g, wpool_g = _norm1(xs, g_mix, tm, _gather_far([win_g, wpool_g]))
    wpool_f = wpool_g.reshape(N_CHIPS, 4, 64, GD).transpose(1, 0, 2, 3).reshape(4, GD, GD)
    tril = jnp.tril(jnp.ones((HD, HD), dtype=bool))
    wt = jnp.where(tril[None], w_s[0], 0.0).astype(BF16)
    wtt = wt.transpose(0, 2, 1)
    bst = jnp.broadcast_to(b_s[0][:, :, None], (HEADS, HD, HD))
    gfin = g_final.reshape(1, D)

    proj, wout_g = _inproj(h1, win_g, tm, _gather_whole([wout_g]))
    mixed, wup_g = _mixer_fwd(proj, wt, bst, g_v, wpool_f, pool_scale, tm, _gather_near([wup_g]))
    wout_f = wout_g.reshape(D, D)
    x2, h2, wup_g = _outproj(mixed, wout_f, xs, g_ffn, 256, _gather_far([wup_g]))
    act, wdown_g = _up(h2, wup_g, 2 * tm, 2048, _gather_whole([wdown_g]))
    wdown_f = wdown_g.reshape(D_FF, D)
    dx3b, dgf, lossv = _down(act, wdown_f, x2, tgt, gfin, tm, 2048)

    halves = lambda dw, rows, cols: dw.reshape(N_CHIPS, 2, rows // (2 * N_CHIPS), cols)
    cw = IN_W // N_CHIPS
    dwdown = halves(_dweight("dw_down", act, dx3b, 1, D_FF, D, 512)[0], D_FF, D)
    da, sib_down = _dact(dx3b, wdown_f, act, 2 * tm, 2048, _sibling_carry(dwdown))
    sum_down = _add_sibling(dwdown, sib_down, 256)
    dwup, got_down = _dweight("dw_up", h2, da, N_CHIPS, D, D, 512, carry=_chips_carry(sum_down, None, (0, 3, 4)))
    dwup = halves(dwup, D_FF, D)
    dx2b, dgffn, got_down, sib_up = _dh2(da, wup_g, x2, dx3b, g_ffn, tm, 2048,
                                         _join(_chips_carry(sum_down, got_down, (3, 4, 4)), _sibling_carry(dwup)))
    half_down = _add_chips(sum_down, got_down, 256)
    sum_up = _add_sibling(dwup, sib_up, 256)
    dwout, got_up = _dweight("dw_out", mixed, dx2b, 1, D, D, 512, carry=_chips_carry(sum_up, None, (0, 1, 4)))
    dwout = halves(dwout, D, D)
    dmix, got_up, sib_out, half_down = _dmixed(
        dx2b, wout_f, 256,
        _join(_chips_carry(sum_up, got_up, (1, 2, 4)), _sibling_carry(dwout), _share_carry([half_down])))
    sum_out = _add_sibling(dwout, sib_out, 256)
    g_down, d_down, nm_down, nv_down = _adamw("adamw_down", w_down[0], half_down.reshape(D, D), m_w_down[0],
                                              v_w_down[0], 128)
    dproj, dws, dbs, dgv, dwp, dsc, got_up = _mixer_bwd(proj, dmix, wt, wtt, bst, g_v, wpool_f, pool_scale, tm,
                                                        _chips_carry(sum_up, got_up, (2, 4, 4)))
    half_up = _add_chips(sum_up, got_up, 256)
    dwin_sib, got_out, half_up = _dweight(
        "dw_in_sibling", h1, dproj, N_CHIPS, D, cw, 512, nc=256, half="sibling",
        carry=_join(_chips_carry(sum_out, None, (0, 1, 2)), _share_carry([half_up])))
    dwin_own, got_out, sib_in = _dweight(
        "dw_in_own", h1, dproj, N_CHIPS, D, cw, 512, nc=256, half="own",
        carry=_join(_chips_carry(sum_out, got_out, (1, 2, 2)), _sibling_carry(dwin_sib)))
    sum_in = _add_sibling(dwin_own, sib_in, 256)
    g_up, d_up, nm_up, nv_up = _adamw("adamw_up", w_up[0], half_up.reshape(D, D), m_w_up[0], v_w_up[0], 128)
    half_out = _add_chips(sum_out, got_out, 256)
    grad_x, dgmix, got_in, half_out = _dh1(dproj, win_g, xs, dx2b, g_mix, tm,
                                           _join(_chips_carry(sum_in, None), _share_carry([half_out])))
    half_in = _add_chips(sum_in, got_in, 256)
    half_in = _comm_only("share_half_in", _share_carry([half_in]))[0]
    g_out, d_out, nm_out, nv_out = _adamw("adamw_out", w_out[0], half_out.reshape(D // N_CHIPS, D), m_w_out[0],
                                          v_w_out[0], 128)
    g_in, d_in, nm_in, nv_in = _adamw("adamw_in", w_in[0], half_in.reshape(D, cw), m_w_in[0], v_w_in[0], 128)

    pieces = [dgmix, dgv, dws, dbs, dwp, dsc, dgffn, dgf, lossv, jnp.zeros((8 * LANES,), F32)]
    sizes = [p.size // LANES for p in pieces]
    tot = _small_allreduce(jnp.concatenate([_rows(p) for p in pieces], axis=0))
    offs = [sum(sizes[:i]) for i in range(len(sizes))]
    take = lambda i: tot[offs[i]:offs[i] + sizes[i]]
    s_gmix, s_gv, s_ws, s_bs, s_wp, s_sc, s_gffn, s_gf = [take(i) for i in range(8)]
    loss = (0.5 / D) * jnp.sum(take(8))
    s_wp_mine = lax.dynamic_slice_in_dim(s_wp.reshape(4, GD, GD), chip * 64, 64, axis=1)
    small_g = [s_gmix, s_gv, s_ws, s_bs, _rows(s_wp_mine), s_sc, s_gffn, s_gf]
    small_w = [g_mix, g_v, w_s, b_s, w_pool, pool_scale, g_ffn, g_final]
    small_m = [m_g_mix, m_g_v, m_w_s, m_b_s, m_w_pool, m_pool_scale, m_g_ffn, m_g_final]
    small_v = [v_g_mix, v_g_v, v_w_s, v_b_s, v_w_pool, v_pool_scale, v_g_ffn, v_g_final]
    cat = lambda parts: jnp.concatenate([_rows(p) for p in parts], axis=0)
    sg = cat(small_g)
    sg, sd, snm, snv = _adamw("adamw_small", cat(small_w), sg, cat(small_m), cat(small_v), sg.shape[0])
    ssz = [p.size // LANES for p in small_w]
    soff = [sum(ssz[:i]) for i in range(len(ssz))]
    split = lambda a: [a[soff[i]:soff[i] + ssz[i]].reshape(small_w[i].shape) for i in range(len(ssz))]
    gs, ds, nms, nvs = split(sg), split(sd), split(snm), split(snv)

    def ordered(small, w_in_, w_out_, w_up_, w_down_):
        return [small[0], w_in_[None], small[1], small[2], small[3], small[4], small[5], w_out_[None], small[6],
                w_up_[None], w_down_[None], small[7]]

    return (loss, grad_x[None],
            *ordered(gs, g_in, g_out, g_up, g_down),
            *ordered(ds, d_in, d_out, d_up, d_down),
            *ordered(nms, nm_in, nm_out, nm_up, nm_down),
            *ordered(nvs, nv_in, nv_out, nv_up, nv_down))
```

```python
import functools

import jax
import jax.numpy as jnp
from jax import lax
from jax.experimental import pallas as pl
from jax.experimental.pallas import tpu as pltpu

F32 = jnp.float32
BF16 = jnp.bfloat16
EPS = 1e-6
D = 2048
A_W = 1024
HEADS = 8
HD = 128
POOL_WINDOWS = (2, 4, 8, 16)
GD = 256
IN_W = 3072
D_FF = 8192
N_CHIPS = 4
HALO = 16
EPI_ROWS = 128
LANES = 128
MIB = 2 ** 20

ADAM_LR, ADAM_B1, ADAM_B2, ADAM_EPS, ADAM_WD, ADAM_STEP = 0.001, 0.9, 0.999, 1e-08, 0.01, 10

ANY = pl.BlockSpec(memory_space=pl.ANY)
MESH = pl.DeviceIdType.MESH

NN = ((1,), (0,))
NT = ((1,), (1,))
TN = ((0,), (0,))


def _place():
    return lax.axis_index("x"), lax.axis_index("y"), lax.axis_index("c")


def _my_chip():
    return 2 * lax.axis_index("x") + lax.axis_index("y")


def _my_core():
    return lax.axis_index("c")


def _other_chips(x, y):
    return [(1 - x, y), (x, 1 - y), (1 - x, 1 - y)]


def _remote(src, dst, send_sem, recv_sem, dev):
    return pltpu.make_async_remote_copy(src_ref=src, dst_ref=dst, send_sem=send_sem, recv_sem=recv_sem,
                                        device_id=dev, device_id_type=MESH)


class _SemView:
    def __init__(self, sems, base):
        self.sems, self.base = sems, base

    @property
    def at(self):
        return self

    def __getitem__(self, i):
        return self.sems.at[self.base + i]


class _Carry:
    def __init__(self, srcs, lands, n_sems, start, finish, middle=None):
        self.srcs, self.lands, self.n_sems, self.start, self.finish = list(srcs), list(lands), n_sems, start, finish
        self.middle = middle


def _join(*carries):
    def run(which):
        def go(srcs, lands, ssem, rsem):
            so = lo = qo = 0
            for c in carries:
                if getattr(c, which) is not None:
                    getattr(c, which)(srcs[so:so + len(c.srcs)], lands[lo:lo + len(c.lands)],
                                      _SemView(ssem, qo), _SemView(rsem, qo))
                so, lo, qo = so + len(c.srcs), lo + len(c.lands), qo + c.n_sems
        return go

    middle = run("middle") if any(c.middle is not None for c in carries) else None
    return _Carry([s for c in carries for s in c.srcs], [l for c in carries for l in c.lands],
                  sum(c.n_sems for c in carries), run("start"), run("finish"), middle)


GATHER_SEMS = 7


def _gather_copies(lands, ssem, rsem, part=(0, 1, 1)):
    x, y, c = _place()
    k, kx, ky, kd = 2 * x + y, 2 * (1 - x) + y, 2 * x + (1 - y), 2 * (1 - x) + (1 - y)
    to_x, to_y, sib = (1 - x, y, c), (x, 1 - y, c), (x, y, 1 - c)
    out = []
    for w, land in enumerate(lands):
        rh = land.shape[1] // 2
        rq = rh // 2
        half = pl.ds(pl.multiple_of(c * rh, 16), rh)
        quarters = [pl.ds(pl.multiple_of(c * rh + q * rq, 16), rq) for q in range(2)]
        pa, pb, pp = part
        sent = pl.ds(pl.multiple_of(c * rh + pa * rh // pp, 16), (pb - pa) * rh // pp)

        def cp(i, piece, dev, w=w):
            return _remote(piece, piece, ssem.at[GATHER_SEMS * w + i], rsem.at[GATHER_SEMS * w + i], dev)

        out.append(dict(
            ax=cp(0, land.at[k, sent], to_x), ay=cp(1, land.at[k, sent], to_y),
            rx=cp(2, land.at[kx, quarters[0]], to_y), ry=cp(3, land.at[ky, quarters[1]], to_x),
            fx=cp(4, land.at[kx, half], sib), fy=cp(5, land.at[ky, half], sib), fd=cp(6, land.at[kd, half], sib)))
    return out


def _gather_whole(gathered):
    def start(srcs, lands, ssem, rsem):
        for d in _gather_copies(lands, ssem, rsem):
            d["ax"].start()
            d["ay"].start()

    def middle(srcs, lands, ssem, rsem):
        for d in _gather_copies(lands, ssem, rsem):
            d["ax"].wait_recv()
            d["rx"].start()
            d["fx"].start()
            d["ay"].wait_recv()
            d["ry"].start()
            d["fy"].start()

    def finish(srcs, lands, ssem, rsem):
        for d in _gather_copies(lands, ssem, rsem):
            d["rx"].wait_recv()
            d["ry"].wait_recv()
            d["fd"].start()
            for name in ("fx", "fy", "fd"):
                d[name].wait_recv()
            for cp in d.values():
                cp.wait_send()

    return _Carry([], gathered, GATHER_SEMS * len(gathered), start, finish, middle)


def _gather_near(gathered, part=(0, 1, 1)):
    def start(srcs, lands, ssem, rsem):
        for d in _gather_copies(lands, ssem, rsem, part):
            d["ax"].start()
            d["ay"].start()

    def finish(srcs, lands, ssem, rsem):
        for d in _gather_copies(lands, ssem, rsem, part):
            for name in ("ax", "ay"):
                d[name].wait_recv()
                d[name].wait_send()

    return _Carry([], gathered, GATHER_SEMS * len(gathered), start, finish)


def _gather_far(gathered):
    def start(srcs, lands, ssem, rsem):
        for d in _gather_copies(lands, ssem, rsem):
            for name in ("rx", "ry", "fx", "fy"):
                d[name].start()

    def finish(srcs, lands, ssem, rsem):
        for d in _gather_copies(lands, ssem, rsem):
            d["rx"].wait_recv()
            d["ry"].wait_recv()
            d["fd"].start()
            for name in ("fx", "fy", "fd"):
                d[name].wait_recv()
            for name in ("rx", "ry", "fx", "fy", "fd"):
                d[name].wait_send()

    return _Carry([], gathered, GATHER_SEMS * len(gathered), start, finish)


def _sibling_carry(dw):
    def copies(srcs, lands, ssem, rsem):
        x, y, c = _place()
        return [_remote(srcs[0].at[j, 1 - c], lands[0].at[j], ssem.at[j], rsem.at[j], (x, y, 1 - c))
                for j in range(N_CHIPS)]

    def start(*a):
        for cp in copies(*a):
            cp.start()

    def finish(*a):
        for cp in copies(*a):
            cp.wait()

    return _Carry([dw], [jax.ShapeDtypeStruct((N_CHIPS,) + dw.shape[-2:], dw.dtype)], N_CHIPS, start, finish)


def _chips_carry(sums, land, part=(0, 1, 1)):
    a, b, p = part
    rh = sums.shape[1]
    rows = pl.ds(a * rh // p, (b - a) * rh // p)

    def copies(srcs, lands, ssem, rsem):
        x, y, c = _place()
        out = []
        for j, (cx, cy) in enumerate(_other_chips(x, y)):
            out.append(_remote(srcs[0].at[2 * cx + cy, rows], lands[0].at[j, rows], ssem.at[j], rsem.at[j],
                               (cx, cy, c)))
        return out

    def start(*a):
        for cp in copies(*a):
            cp.start()

    def finish(*a):
        for cp in copies(*a):
            cp.wait()

    if land is None:
        land = jax.ShapeDtypeStruct((3,) + sums.shape[1:], sums.dtype)
    return _Carry([sums], [land], 3, start, finish)


def _call(name, grid, body, ins, in_specs, out_shapes, out_specs, scratch=(), vmem_mb=48, carry=None):
    n_in, n_out, n_sc = len(ins), len(out_shapes), len(scratch)
    ins, in_specs = list(ins), list(in_specs)
    out_shapes, out_specs, scratch = list(out_shapes), list(out_specs), list(scratch)
    aliases = {}
    if carry is not None:
        ins += carry.srcs
        in_specs += [ANY] * len(carry.srcs)
        for land in carry.lands:
            if not isinstance(land, jax.ShapeDtypeStruct):
                aliases[len(ins)] = len(out_shapes)
                ins.append(land)
                in_specs.append(ANY)
                land = jax.ShapeDtypeStruct(land.shape, land.dtype)
            out_shapes.append(land)
            out_specs.append(ANY)
        scratch += [pltpu.SemaphoreType.DMA((carry.n_sems,)), pltpu.SemaphoreType.DMA((carry.n_sems,))]
    n_in_all, n_out_all = len(ins), len(out_shapes)
    ins = [pltpu.with_memory_space_constraint(a, pltpu.HBM) for a in ins]

    def kbody(*refs):
        in_refs, out_refs, sc = refs[:n_in_all], refs[n_in_all:n_in_all + n_out_all], refs[n_in_all + n_out_all:]
        ids = tuple(pl.program_id(a) for a in range(len(grid)))
        if carry is not None:
            first = functools.reduce(jnp.logical_and, [i == 0 for i in ids])
            last = functools.reduce(jnp.logical_and, [i == g - 1 for i, g in zip(ids, grid)])
            comm = (in_refs[n_in:n_in + len(carry.srcs)], out_refs[n_out:], sc[n_sc], sc[n_sc + 1])

            @pl.when(first)
            def _():
                carry.start(*comm)

            if carry.middle is not None:
                step, total = ids[0], grid[0]
                for i, g in zip(ids[1:], grid[1:]):
                    step, total = step * g + i, total * g

                @pl.when(step == (2 * total) // 3)
                def _():
                    carry.middle(*comm)

        body(in_refs[:n_in], out_refs[:n_out], sc[:n_sc], ids)
        if carry is not None:
            @pl.when(last)
            def _():
                carry.finish(*comm)

    return pl.pallas_call(
        kbody, name=name, grid=grid, in_specs=in_specs, out_specs=out_specs,
        out_shape=[pltpu.HBM(o.shape, o.dtype) for o in out_shapes],
        scratch_shapes=scratch, input_output_aliases=aliases,
        compiler_params=pltpu.CompilerParams(dimension_semantics=("arbitrary",) * len(grid),
                                             vmem_limit_bytes=vmem_mb * MIB),
    )(*ins)


def _matmul(name, grid, kaxis, ins, in_specs, out_shapes, out_specs, dims, epi, tail=None, acc_shape=None, nc=512,
            vmem_mb=48, carry=None):
    nk = grid[kaxis] if kaxis is not None else 1

    def body(in_refs, out_refs, scratch, ids):
        a_ref, b_ref = in_refs[0], in_refs[1]
        n = b_ref.shape[0] if dims == NT else b_ref.shape[1]

        def prod(a, c0):
            b = b_ref[c0:c0 + nc, :] if dims == NT else b_ref[:, c0:c0 + nc]
            return lax.dot_general(a, b, (dims, ((), ())), preferred_element_type=F32)

        if kaxis is None:
            a = a_ref[...]
            for c0 in range(0, n, nc):
                epi(prod(a, c0), slice(c0, c0 + nc), ids, in_refs[2:], out_refs)
            if tail is not None:
                tail(ids, in_refs[2:], out_refs)
        else:
            acc = scratch[0]
            tm = acc.shape[0]
            last = ids[kaxis] == nk - 1
            _zero_when(ids[kaxis] == 0, acc)

            @pl.when(jnp.logical_not(last))
            def _():
                a = a_ref[...]
                for c0 in range(0, n, nc):
                    acc[:, c0:c0 + nc] += prod(a, c0)

            @pl.when(last)
            def _():
                block = epi(acc, ids, in_refs[2:], out_refs)
                for r0 in range(0, tm, tm // 2):
                    a = a_ref[r0:r0 + tm // 2, :]
                    for c0 in range(0, n, nc):
                        acc[r0:r0 + tm // 2, c0:c0 + nc] += prod(a, c0)
                    for q0 in range(r0, r0 + tm // 2, EPI_ROWS):
                        block(slice(q0, q0 + EPI_ROWS))

    return _call(name, grid, body, ins, in_specs, out_shapes, out_specs,
                 [pltpu.VMEM(acc_shape, F32)] if kaxis is not None else [], vmem_mb, carry)


def _row_rsqrt(xf):
    return lax.rsqrt(jnp.mean(xf * xf, axis=-1, keepdims=True) + EPS)


def _norm_bwd(dh, xf, g, resid):
    r = _row_rsqrt(xf)
    xh = xf * r
    dg = jnp.sum(dh * xh, axis=0, keepdims=True)
    dxh = dh * g
    dx = resid + r * (dxh - xh * jnp.mean(dxh * xh, axis=-1, keepdims=True))
    return dx, dg


def _zero_when(first, *refs):
    @pl.when(first)
    def _():
        for ref in refs:
            ref[...] = jnp.zeros_like(ref)


_GELU_K = 0.7978845608028654
_GELU_C = 0.044715


def _gelu(x):
    t = jnp.tanh(_GELU_K * (x + _GELU_C * x * x * x))
    return 0.5 * x * (1.0 + t)


def _gelu_and_grad(x):
    x2 = x * x
    t = jnp.tanh(_GELU_K * (x + _GELU_C * x * x2))
    g = 0.5 * x * (1.0 + t)
    dg = 0.5 * (1.0 + t) + 0.5 * x * (1.0 - t * t) * (_GELU_K * (1.0 + 3.0 * _GELU_C * x2))
    return g, dg


def _window_sum(ext, w, causal):
    n = ext.shape[0]
    s, d = ext, 1
    while d < w:
        s = s + pltpu.roll(s, d if causal else n - d, 0)
        d *= 2
    return s


def _inv_count(t, w):
    return 1.0 / jnp.minimum(t + 1, w).astype(F32)


def _pooled(z_ref, zh_ref, g, w, i, tm):
    cols = slice(GD * g, GD * (g + 1))
    zb = z_ref[:, cols]
    zh = jnp.where(i > 0, zh_ref[:, cols], 0.0)
    ext = jnp.concatenate([zh, zb], axis=0)
    s = _window_sum(ext, w, True)[HALO:, :]
    t = i * tm + lax.broadcasted_iota(jnp.int32, (tm, 1), 0)
    return s * _inv_count(t, w) - zb


def _full(shape, n_axes=1):
    return pl.BlockSpec(shape, lambda *ids: (0,) * len(shape))


def _norm1(x, g, tm, carry):
    s = x.shape[0]

    def body(ins, outs, scratch, ids):
        xf = ins[0][...]
        outs[0][...] = (xf * _row_rsqrt(xf) * ins[1][...]).astype(BF16)

    row = pl.BlockSpec((tm, D), lambda i: (i, 0))
    return _call("norm1", (s // tm,), body, [x, g], [row, _full((1, D))],
                 [jax.ShapeDtypeStruct((s, D), BF16)], [row], vmem_mb=32, carry=carry)


def _mixer_fwd(proj, wt, bst, gv, wpool, scale, tm, carry):
    s = proj.shape[0]
    nq = tm // HD

    def body(ins, outs, scratch, ids):
        up_ref, vp_ref, z_ref, zh_ref, wt_ref, bst_ref, gv_ref, wp_ref, sc_ref = ins
        out_ref = outs[0]
        i = ids[0]
        for h in range(HEADS):
            cols = slice(HD * h, HD * (h + 1))
            gvh = gv_ref[:, cols]
            bcol = bst_ref[h]
            wth = wt_ref[h]

            def chunk(q, c_):
                rows = pl.ds(pl.multiple_of(q * HD, HD), HD)
                u = _gelu(up_ref[rows, cols])
                v = _gelu(vp_ref[rows, cols])
                vh = (v * _row_rsqrt(v) * gvh).astype(BF16)
                mixed = jnp.dot(wth, vh, preferred_element_type=F32) + bcol
                out_ref[rows, cols] = (u * mixed).astype(BF16)
                return c_

            lax.fori_loop(0, nq, chunk, 0, unroll=True)
        for g, w in enumerate(POOL_WINDOWS):
            cols = slice(GD * g, GD * (g + 1))
            pooled = _pooled(z_ref, zh_ref, g, w, i, tm)
            yv = jnp.dot(pooled.astype(BF16), wp_ref[g], preferred_element_type=F32)
            out_ref[:, A_W + GD * g:A_W + GD * (g + 1)] = (yv * sc_ref[:, cols]).astype(BF16)

    hb = tm // HALO
    return _call(
        "mixer_fwd", (s // tm,), body, [proj, proj, proj, proj, wt, bst, gv, wpool, scale],
        [pl.BlockSpec((tm, A_W), lambda i: (i, 0)),
         pl.BlockSpec((tm, A_W), lambda i: (i, 1)),
         pl.BlockSpec((tm, A_W), lambda i: (i, 2)),
         pl.BlockSpec((HALO, A_W), lambda i: (jnp.maximum(i * hb - 1, 0), 2)),
         _full((HEADS, HD, HD)), _full((HEADS, HD, HD)), _full((1, A_W)), _full((4, GD, GD)), _full((1, A_W))],
        [jax.ShapeDtypeStruct((s, D), BF16)], [pl.BlockSpec((tm, D), lambda i: (i, 0))], vmem_mb=40, carry=carry)


def _mixer_bwd(proj, dmix, wt, wtt, bst, gv, wpool, scale, tm, carry):
    s = proj.shape[0]
    nb = s // tm
    nq = tm // HD
    hb = tm // HALO

    def body(ins, outs, scratch, ids):
        (up_ref, vp_ref, z_ref, zh_ref, doa_ref, dob_ref, dobh_ref, wt_ref, wtt_ref, bst_ref, gv_ref, wp_ref,
         sc_ref) = ins
        dproj_ref, dws_ref, dbs_ref, dgv_ref, dwp_ref, dsc_ref = outs
        dbfull = scratch[0]
        i = ids[0]
        last = i == nb - 1
        _zero_when(i == 0, dws_ref, dbfull, dgv_ref, dwp_ref, dsc_ref)

        for h in range(HEADS):
            cols = slice(HD * h, HD * (h + 1))
            gvh = gv_ref[:, cols]
            bcol = bst_ref[h]
            wth = wt_ref[h]
            wtth = wtt_ref[h]

            def chunk(q, c_):
                rows = pl.ds(pl.multiple_of(q * HD, HD), HD)
                u, du_dup = _gelu_and_grad(up_ref[rows, cols])
                v, dv_dvp = _gelu_and_grad(vp_ref[rows, cols])
                rv = _row_rsqrt(v)
                vn = v * rv
                vh = (vn * gvh).astype(BF16)
                mixed = jnp.dot(wth, vh, preferred_element_type=F32) + bcol
                doa = doa_ref[rows, cols].astype(F32)
                dmx = doa * u
                dmxb = dmx.astype(BF16)
                dbfull[h] += dmx
                dws_ref[h] += lax.dot_general(dmxb, vh, (NT, ((), ())), preferred_element_type=F32)
                dvh = jnp.dot(wtth, dmxb, preferred_element_type=F32)
                dgv_ref[:, cols] += jnp.sum(dvh * vn, axis=0, keepdims=True)
                dvn = dvh * gvh
                dv = rv * (dvn - vn * jnp.mean(dvn * vn, axis=-1, keepdims=True))
                dproj_ref[rows, cols] = (doa * mixed * du_dup).astype(BF16)
                dproj_ref[rows, A_W + HD * h:A_W + HD * (h + 1)] = (dv * dv_dvp).astype(BF16)
                return c_

            lax.fori_loop(0, nq, chunk, 0, unroll=True)

        t = i * tm + lax.broadcasted_iota(jnp.int32, (tm, 1), 0)
        th = (i + 1) * tm + lax.broadcasted_iota(jnp.int32, (HALO, 1), 0)
        for g, w in enumerate(POOL_WINDOWS):
            cols = slice(GD * g, GD * (g + 1))
            wpg = wp_ref[g]
            scg = sc_ref[:, cols]
            pb = _pooled(z_ref, zh_ref, g, w, i, tm).astype(BF16)
            ypre = jnp.dot(pb, wpg, preferred_element_type=F32)
            dob = dob_ref[:, cols].astype(F32)
            dsc_ref[:, cols] += jnp.sum(dob * ypre, axis=0, keepdims=True)
            dyb = (dob * scg).astype(BF16)
            dwp_ref[g] += lax.dot_general(pb, dyb, (TN, ((), ())), preferred_element_type=F32)
            dpo = lax.dot_general(dyb, wpg, (NT, ((), ())), preferred_element_type=F32)
            dyh = (dobh_ref[:, cols].astype(F32) * scg).astype(BF16)
            dpoh = lax.dot_general(dyh, wpg, (NT, ((), ())), preferred_element_type=F32)
            dpoh = jnp.where(last, 0.0, dpoh * _inv_count(th, w))
            ext = jnp.concatenate([dpo * _inv_count(t, w), dpoh], axis=0)
            dz = _window_sum(ext, w, False)[:tm, :] - dpo
            dproj_ref[:, 2 * A_W + GD * g:2 * A_W + GD * (g + 1)] = dz.astype(BF16)

        @pl.when(last)
        def _():
            r = lax.broadcasted_iota(jnp.int32, (HD, HD), 0)
            c = lax.broadcasted_iota(jnp.int32, (HD, HD), 1)
            for h in range(HEADS):
                dws_ref[h] = jnp.where(r >= c, dws_ref[h], 0.0)
                dbs_ref[h] = jnp.sum(dbfull[h], axis=-1, keepdims=True)

    return _call(
        "mixer_bwd", (nb,), body, [proj, proj, proj, proj, dmix, dmix, dmix, wt, wtt, bst, gv, wpool, scale],
        [pl.BlockSpec((tm, A_W), lambda i: (i, 0)),
         pl.BlockSpec((tm, A_W), lambda i: (i, 1)),
         pl.BlockSpec((tm, A_W), lambda i: (i, 2)),
         pl.BlockSpec((HALO, A_W), lambda i: (jnp.maximum(i * hb - 1, 0), 2)),
         pl.BlockSpec((tm, A_W), lambda i: (i, 0)),
         pl.BlockSpec((tm, A_W), lambda i: (i, 1)),
         pl.BlockSpec((HALO, A_W), lambda i: (jnp.minimum((i + 1) * hb, s // HALO - 1), 1)),
         _full((HEADS, HD, HD)), _full((HEADS, HD, HD)), _full((HEADS, HD, HD)), _full((1, A_W)),
         _full((4, GD, GD)), _full((1, A_W))],
        [jax.ShapeDtypeStruct((s, IN_W), BF16),
         jax.ShapeDtypeStruct((HEADS, HD, HD), F32),
         jax.ShapeDtypeStruct((HEADS, HD, 1), F32),
         jax.ShapeDtypeStruct((1, A_W), F32),
         jax.ShapeDtypeStruct((4, GD, GD), F32),
         jax.ShapeDtypeStruct((1, A_W), F32)],
        [pl.BlockSpec((tm, IN_W), lambda i: (i, 0)),
         _full((HEADS, HD, HD)), _full((HEADS, HD, 1)), _full((1, A_W)), _full((4, GD, GD)), _full((1, A_W))],
        [pltpu.VMEM((HEADS, HD, HD), F32)], vmem_mb=48, carry=carry)


def _inproj(h1, win_g, tm, carry):
    s = h1.shape[0]
    cw = IN_W // N_CHIPS

    def epi(p, cols, ids, extra, outs):
        outs[0][:, cols] = p

    return _matmul(
        "inproj", (N_CHIPS, s // tm), None, [h1, win_g],
        [pl.BlockSpec((tm, D), lambda j, i: (i, 0)), pl.BlockSpec((None, D, cw), lambda j, i: (j, 0, 0))],
        [jax.ShapeDtypeStruct((s, IN_W), F32)], [pl.BlockSpec((tm, cw), lambda j, i: (i, j))], NN, epi,
        nc=256, vmem_mb=32, carry=carry)


def _outproj(mixed, wout, x, g_ffn, tm, carry):
    s = x.shape[0]

    def epi(p, cols, ids, extra, outs):
        outs[0][:, cols] = extra[0][:, cols] + p

    def tail(ids, extra, outs):
        x2 = outs[0][...]
        outs[1][...] = (x2 * _row_rsqrt(x2) * extra[1][...]).astype(BF16)

    row = pl.BlockSpec((tm, D), lambda i: (i, 0))
    return _matmul(
        "outproj", (s // tm,), None, [mixed, wout, x, g_ffn], [row, _full((D, D)), row, _full((1, D))],
        [jax.ShapeDtypeStruct((s, D), F32), jax.ShapeDtypeStruct((s, D), BF16)], [row, row], NN, epi, tail,
        vmem_mb=48, carry=carry)


def _up(h2, wup_g, tm, tn, carry):
    s = h2.shape[0]
    per = D // tn

    def epi(p, cols, ids, extra, outs):
        a = jnp.maximum(p, 0.0)
        outs[0][:, cols] = (a * a).astype(BF16)

    return _matmul(
        "up", (D_FF // tn, s // tm), None, [h2, wup_g],
        [pl.BlockSpec((tm, D), lambda j, i: (i, 0)), pl.BlockSpec((None, D, tn), lambda j, i: (j // per, 0, j % per))],
        [jax.ShapeDtypeStruct((s, D_FF), BF16)], [pl.BlockSpec((tm, tn), lambda j, i: (i, j))], NN, epi,
        vmem_mb=48, carry=carry)


def _down(act, wdown, x2, tgt, g_final, tm, tk):
    s = x2.shape[0]

    def epi(acc, ids, extra, outs):
        x2_ref, t_ref, g_ref = extra
        dxb_ref, dgf_ref, loss_ref = outs
        g = g_ref[...]
        _zero_when(ids[0] == 0, dgf_ref, loss_ref)

        def block(rows):
            x3 = x2_ref[rows, :] + acc[rows, :]
            r = _row_rsqrt(x3)
            xh = x3 * r
            diff = xh * g - t_ref[rows, :]
            dy = diff * (1.0 / D)
            dxh = dy * g
            dx = r * (dxh - xh * jnp.mean(dxh * xh, axis=-1, keepdims=True))
            dxb_ref[rows, :] = dx.astype(BF16)
            dgf_ref[...] += jnp.sum(dy * xh, axis=0, keepdims=True)
            loss_ref[...] += jnp.sum(diff * diff, axis=0, keepdims=True)

        return block

    row = pl.BlockSpec((tm, D), lambda i, k: (i, 0))
    vec = pl.BlockSpec((1, D), lambda i, k: (0, 0))
    return _matmul(
        "down", (s // tm, D_FF // tk), 1, [act, wdown, x2, tgt, g_final],
        [pl.BlockSpec((tm, tk), lambda i, k: (i, k)), pl.BlockSpec((tk, D), lambda i, k: (k, 0)), row, row, vec],
        [jax.ShapeDtypeStruct((s, D), BF16), jax.ShapeDtypeStruct((1, D), F32), jax.ShapeDtypeStruct((1, D), F32)],
        [row, vec, vec], NN, epi, acc_shape=(tm, D), vmem_mb=56)


def _dact(dx3b, wdown, act, tm, tn, carry):
    s = dx3b.shape[0]

    def epi(p, cols, ids, extra, outs):
        outs[0][:, cols] = (p * (2.0 * jnp.sqrt(extra[0][:, cols].astype(F32)))).astype(BF16)

    tile = pl.BlockSpec((tm, tn), lambda j, i: (i, j))
    return _matmul(
        "dact", (D_FF // tn, s // tm), None, [dx3b, wdown, act],
        [pl.BlockSpec((tm, D), lambda j, i: (i, 0)), pl.BlockSpec((tn, D), lambda j, i: (j, 0)), tile],
        [jax.ShapeDtypeStruct((s, D_FF), BF16)], [tile], NT, epi, vmem_mb=56, carry=carry)


def _dweight(name, lhs, rhs, n_shards, rows, cols, tm, nc=512, carry=None):
    s = lhs.shape[0]

    def epi(p, cs, ids, extra, outs):
        outs[0][:, cs] = p.astype(BF16)

    return _matmul(
        name, (n_shards, rows // tm), None, [lhs, rhs],
        [pl.BlockSpec((s, tm), lambda j, i: (0, i)), pl.BlockSpec((s, cols), lambda j, i: (0, j))],
        [jax.ShapeDtypeStruct((n_shards, rows, cols), BF16)],
        [pl.BlockSpec((None, tm, cols), lambda j, i: (j, i, 0))], TN, epi, nc=nc, vmem_mb=56, carry=carry)


def _dh2(da, wup_g, x2, dx3, g_ffn, tm, tk, carry):
    s = x2.shape[0]
    per = D // tk

    def epi(acc, ids, extra, outs):
        x2_ref, dx3_ref, g_ref = extra
        g = g_ref[...]
        _zero_when(ids[0] == 0, outs[1])

        def block(rows):
            dx, dg = _norm_bwd(acc[rows, :], x2_ref[rows, :], g, dx3_ref[rows, :].astype(F32))
            outs[0][rows, :] = dx.astype(BF16)
            outs[1][...] += dg

        return block

    row = pl.BlockSpec((tm, D), lambda i, k: (i, 0))
    vec = pl.BlockSpec((1, D), lambda i, k: (0, 0))
    return _matmul(
        "dh2", (s // tm, D_FF // tk), 1, [da, wup_g, x2, dx3, g_ffn],
        [pl.BlockSpec((tm, tk), lambda i, k: (i, k)),
         pl.BlockSpec((None, D, tk), lambda i, k: (k // per, 0, k % per)), row, row, vec],
        [jax.ShapeDtypeStruct((s, D), BF16), jax.ShapeDtypeStruct((1, D), F32)],
        [row, vec], NT, epi, acc_shape=(tm, D), vmem_mb=56, carry=carry)


def _dmixed(dx2b, wout, tm, carry):
    s = dx2b.shape[0]

    def epi(p, cols, ids, extra, outs):
        outs[0][:, cols] = p.astype(BF16)

    row = pl.BlockSpec((tm, D), lambda i: (i, 0))
    return _matmul(
        "dmixed", (s // tm,), None, [dx2b, wout], [row, _full((D, D))],
        [jax.ShapeDtypeStruct((s, D), BF16)], [row], NT, epi, vmem_mb=40, carry=carry)


def _dh1(dproj, win_g, x, dx2, g_mix, tm, carry):
    s = x.shape[0]
    cw = IN_W // N_CHIPS

    def epi(acc, ids, extra, outs):
        x_ref, dx2_ref, g_ref = extra
        g = g_ref[...]
        _zero_when(ids[0] == 0, outs[1])

        def block(rows):
            dx, dg = _norm_bwd(acc[rows, :], x_ref[rows, :], g, dx2_ref[rows, :].astype(F32))
            outs[0][rows, :] = dx
            outs[1][...] += dg

        return block

    row = pl.BlockSpec((tm, D), lambda i, j: (i, 0))
    vec = pl.BlockSpec((1, D), lambda i, j: (0, 0))
    return _matmul(
        "dh1", (s // tm, N_CHIPS), 1, [dproj, win_g, x, dx2, g_mix],
        [pl.BlockSpec((tm, cw), lambda i, j: (i, j)), pl.BlockSpec((None, D, cw), lambda i, j: (j, 0, 0)),
         row, row, vec],
        [jax.ShapeDtypeStruct((s, D), F32), jax.ShapeDtypeStruct((1, D), F32)],
        [row, vec], NT, epi, acc_shape=(tm, D), vmem_mb=52, carry=carry)


def _cast_place(name, w, tr, carry=None):
    rows, cols = w.shape

    def body(ins, outs, scratch, ids):
        outs[0][...] = ins[0][...].astype(BF16)

    return _call(name, (rows // tr,), body, [w], [pl.BlockSpec((tr, cols), lambda r: (r, 0))],
                 [jax.ShapeDtypeStruct((N_CHIPS, rows, cols), BF16)],
                 [pl.BlockSpec((None, tr, cols), lambda r: (_my_chip(), r, 0))], vmem_mb=32, carry=carry)


def _small_allreduce(part):
    rows = part.shape[0]
    rh = rows // 2

    def body(p_ref, o_ref, sib_ref, slots, send_sems, recv_sems):
        x, y, c = _place()
        k = 2 * x + y
        sib = (x, y, 1 - c)
        half = pl.ds(pl.multiple_of(c * rh, 8), rh)
        cp = _remote(p_ref, sib_ref, send_sems.at[0], recv_sems.at[0], sib)
        cp.start()
        cp.wait()
        slots[k] = p_ref[half, :] + sib_ref[half, :]
        cps = []
        for j, (cx, cy) in enumerate(_other_chips(x, y)):
            cp = _remote(slots.at[k], slots.at[k], send_sems.at[1 + j], recv_sems.at[1 + j], (cx, cy, c))
            cp.start()
            cps.append(cp)
        for cp in cps:
            cp.wait()
        o_ref[half, :] = ((slots[0] + slots[1]) + slots[2]) + slots[3]
        cp = _remote(o_ref.at[half], o_ref.at[half], send_sems.at[4], recv_sems.at[4], sib)
        cp.start()
        cp.wait()

    vm = pl.BlockSpec(memory_space=pltpu.VMEM)
    return pl.pallas_call(
        body, name="small_allreduce", in_specs=[vm], out_specs=vm,
        out_shape=jax.ShapeDtypeStruct(part.shape, F32),
        scratch_shapes=[pltpu.VMEM(part.shape, F32), pltpu.VMEM((N_CHIPS, rh, LANES), F32),
                        pltpu.SemaphoreType.DMA((5,)), pltpu.SemaphoreType.DMA((5,))],
        compiler_params=pltpu.CompilerParams(vmem_limit_bytes=32 * MIB),
    )(part)


def _comm_only(name, carry):
    ns, nl = len(carry.srcs), len(carry.lands)
    lands_in = [l for l in carry.lands if not isinstance(l, jax.ShapeDtypeStruct)]
    assert len(lands_in) in (0, nl)

    def body(*refs):
        srcs = refs[:ns]
        lands = refs[ns + len(lands_in):ns + len(lands_in) + nl]
        ssem, rsem = refs[ns + len(lands_in) + nl:]
        carry.start(srcs, lands, ssem, rsem)
        if carry.middle is not None:
            carry.middle(srcs, lands, ssem, rsem)
        carry.finish(srcs, lands, ssem, rsem)

    return pl.pallas_call(
        body, name=name, in_specs=[ANY] * (ns + len(lands_in)), out_specs=[ANY] * nl,
        out_shape=[jax.ShapeDtypeStruct(l.shape, l.dtype) for l in carry.lands],
        input_output_aliases={ns + i: i for i in range(len(lands_in))},
        scratch_shapes=[pltpu.SemaphoreType.DMA((carry.n_sems,)), pltpu.SemaphoreType.DMA((carry.n_sems,))],
    )(*carry.srcs, *lands_in)


def _share_carry(grads):
    n = len(grads)

    def copies(srcs, lands, ssem, rsem):
        x, y, c = _place()
        return [_remote(lands[w].at[c], lands[w].at[c], ssem.at[w], rsem.at[w], (x, y, 1 - c)) for w in range(n)]

    def start(*a):
        for cp in copies(*a):
            cp.start()

    def finish(*a):
        for cp in copies(*a):
            cp.wait()

    return _Carry([], grads, n, start, finish)


def _add_sibling(dw, got, tr):
    _, _, rh, cols = dw.shape

    def body(ins, outs, scratch, ids):
        outs[0][...] = (ins[0][...].astype(F32) + ins[1][...].astype(F32)).astype(BF16)

    other = lambda j: (_my_chip() + 1 + j) % N_CHIPS
    blk = pl.BlockSpec((None, tr, cols), lambda j, r: (other(j), r, 0))
    return _call("add_sibling", (N_CHIPS - 1, rh // tr), body, [dw, got],
                 [pl.BlockSpec((None, None, tr, cols), lambda j, r: (other(j), _my_core(), r, 0)), blk],
                 [jax.ShapeDtypeStruct((N_CHIPS, rh, cols), BF16)], [blk], vmem_mb=32)[0]


def _add_chips(dw, sib, got, tr):
    _, _, rh, cols = dw.shape

    def body(ins, outs, scratch, ids):
        b = ins[2][...].astype(F32)
        outs[0][...] = (((ins[0][...].astype(F32) + ins[1][...].astype(F32)) + b[0]) + b[1]) + b[2]

    return _call("add_chips", (rh // tr,), body, [dw, sib, got],
                 [pl.BlockSpec((None, None, tr, cols), lambda r: (_my_chip(), _my_core(), r, 0)),
                  pl.BlockSpec((None, tr, cols), lambda r: (_my_chip(), r, 0)),
                  pl.BlockSpec((3, tr, cols), lambda r: (0, r, 0))],
                 [jax.ShapeDtypeStruct((2, rh, cols), F32)],
                 [pl.BlockSpec((None, tr, cols), lambda r: (_my_core(), r, 0))], vmem_mb=32)[0]


def _adamw_math(w, g, m, v):
    m = ADAM_B1 * m + (1.0 - ADAM_B1) * g
    v = ADAM_B2 * v + (1.0 - ADAM_B2) * (g * g)
    m_hat = m / (1.0 - ADAM_B1 ** ADAM_STEP)
    v_hat = v / (1.0 - ADAM_B2 ** ADAM_STEP)
    delta = -ADAM_LR * (m_hat / (jnp.sqrt(v_hat) + ADAM_EPS) + ADAM_WD * w)
    return delta, m, v


def _adamw(name, w, g, m, v, tr, carry=None):
    rows, cols = w.shape

    def body(ins, outs, scratch, ids):
        g_val = ins[1][...]
        outs[0][...] = g_val
        outs[1][...], outs[2][...], outs[3][...] = _adamw_math(ins[0][...], g_val, ins[2][...], ins[3][...])

    blk = pl.BlockSpec((tr, cols), lambda r: (r, 0))
    return _call(name, (rows // tr,), body, [w, g, m, v], [blk] * 4,
                 [jax.ShapeDtypeStruct(w.shape, F32)] * 4, [blk] * 4, vmem_mb=40, carry=carry)


def _rows(a):
    return a.reshape(-1, LANES)


def kernel(x, g_mix, w_in, g_v, w_s, b_s, w_pool, pool_scale, w_out, g_ffn, w_up, w_down, g_final, loss_target, m_g_mix, m_w_in, m_g_v, m_w_s, m_b_s, m_w_pool, m_pool_scale, m_w_out, m_g_ffn, m_w_up, m_w_down, m_g_final, v_g_mix, v_w_in, v_g_v, v_w_s, v_b_s, v_w_pool, v_pool_scale, v_w_out, v_g_ffn, v_w_up, v_w_down, v_g_final):
    tm = 512
    xs = x[0]
    tgt = loss_target[0]
    chip = _my_chip()

    win_g = _cast_place("cast_w_in", w_in[0], 256)[0]
    wpool_g = _cast_place("cast_w_pool", w_pool[0].reshape(4 * 64, GD), 128)[0]
    wout_g = _cast_place("cast_w_out", w_out[0], 128)[0]
    wdown_g, win_g, wpool_g = _cast_place("cast_w_down", w_down[0], 256, _gather_near([win_g, wpool_g], (0, 1, 2)))
    wup_g, win_g, wpool_g = _cast_place("cast_w_up", w_up[0], 256, _gather_near([win_g, wpool_g], (1, 2, 2)))
    h1, win_g, wpool_g = _norm1(xs, g_mix, tm, _gather_far([win_g, wpool_g]))
    wpool_f = wpool_g.reshape(N_CHIPS, 4, 64, GD).transpose(1, 0, 2, 3).reshape(4, GD, GD)
    tril = jnp.tril(jnp.ones((HD, HD), dtype=bool))
    wt = jnp.where(tril[None], w_s[0], 0.0).astype(BF16)
    wtt = wt.transpose(0, 2, 1)
    bst = jnp.broadcast_to(b_s[0][:, :, None], (HEADS, HD, HD))
    gfin = g_final.reshape(1, D)

    proj, wout_g = _inproj(h1, win_g, tm, _gather_whole([wout_g]))
    mixed, wup_g = _mixer_fwd(proj, wt, bst, g_v, wpool_f, pool_scale, tm, _gather_near([wup_g]))
    wout_f = wout_g.reshape(D, D)
    x2, h2, wup_g = _outproj(mixed, wout_f, xs, g_ffn, 256, _gather_far([wup_g]))
    act, wdown_g = _up(h2, wup_g, 2 * tm, 2048, _gather_whole([wdown_g]))
    wdown_f = wdown_g.reshape(D_FF, D)
    dx3b, dgf, lossv = _down(act, wdown_f, x2, tgt, gfin, tm, 2048)

    halves = lambda dw, rows, cols: dw.reshape(N_CHIPS, 2, rows // (2 * N_CHIPS), cols)
    cw = IN_W // N_CHIPS
    dwdown = halves(_dweight("dw_down", act, dx3b, 1, D_FF, D, 512)[0], D_FF, D)
    da, sib_down = _dact(dx3b, wdown_f, act, 2 * tm, 2048, _sibling_carry(dwdown))
    sum_down = _add_sibling(dwdown, sib_down, 256)
    dwup, got_down = _dweight("dw_up", h2, da, N_CHIPS, D, D, 512, carry=_chips_carry(sum_down, None, (0, 3, 4)))
    dwup = halves(dwup, D_FF, D)
    dx2b, dgffn, got_down, sib_up = _dh2(da, wup_g, x2, dx3b, g_ffn, tm, 2048,
                                         _join(_chips_carry(sum_down, got_down, (3, 4, 4)), _sibling_carry(dwup)))
    half_down = _add_chips(dwdown, sib_down, got_down, 256)
    sum_up = _add_sibling(dwup, sib_up, 256)
    dwout, got_up = _dweight("dw_out", mixed, dx2b, 1, D, D, 512, carry=_chips_carry(sum_up, None, (0, 1, 4)))
    dwout = halves(dwout, D, D)
    dmix, got_up, sib_out, half_down = _dmixed(
        dx2b, wout_f, 256,
        _join(_chips_carry(sum_up, got_up, (1, 2, 4)), _sibling_carry(dwout), _share_carry([half_down])))
    sum_out = _add_sibling(dwout, sib_out, 256)
    g_down, d_down, nm_down, nv_down = _adamw("adamw_down", w_down[0], half_down.reshape(D, D), m_w_down[0],
                                              v_w_down[0], 128)
    dproj, dws, dbs, dgv, dwp, dsc, got_up = _mixer_bwd(proj, dmix, wt, wtt, bst, g_v, wpool_f, pool_scale, tm,
                                                        _chips_carry(sum_up, got_up, (2, 4, 4)))
    half_up = _add_chips(dwup, sib_up, got_up, 256)
    dwin, got_out, half_up = _dweight("dw_in", h1, dproj, N_CHIPS, D, cw, 512, nc=256,
                                      carry=_join(_chips_carry(sum_out, None), _share_carry([half_up])))
    dwin = halves(dwin, N_CHIPS * D, cw)
    sib_in = _comm_only("sibling_in", _sibling_carry(dwin))[0]
    sum_in = _add_sibling(dwin, sib_in, 256)
    g_up, d_up, nm_up, nv_up = _adamw("adamw_up", w_up[0], half_up.reshape(D, D), m_w_up[0], v_w_up[0], 128)
    half_out = _add_chips(dwout, sib_out, got_out, 256)
    grad_x, dgmix, got_in, half_out = _dh1(dproj, win_g, xs, dx2b, g_mix, tm,
                                           _join(_chips_carry(sum_in, None), _share_carry([half_out])))
    half_in = _add_chips(dwin, sib_in, got_in, 256)
    half_in = _comm_only("share_half_in", _share_carry([half_in]))[0]
    g_out, d_out, nm_out, nv_out = _adamw("adamw_out", w_out[0], half_out.reshape(D // N_CHIPS, D), m_w_out[0],
                                          v_w_out[0], 128)
    g_in, d_in, nm_in, nv_in = _adamw("adamw_in", w_in[0], half_in.reshape(D, cw), m_w_in[0], v_w_in[0], 128)

    pieces = [dgmix, dgv, dws, dbs, dwp, dsc, dgffn, dgf, lossv, jnp.zeros((8 * LANES,), F32)]
    sizes = [p.size // LANES for p in pieces]
    tot = _small_allreduce(jnp.concatenate([_rows(p) for p in pieces], axis=0))
    offs = [sum(sizes[:i]) for i in range(len(sizes))]
    take = lambda i: tot[offs[i]:offs[i] + sizes[i]]
    s_gmix, s_gv, s_ws, s_bs, s_wp, s_sc, s_gffn, s_gf = [take(i) for i in range(8)]
    loss = (0.5 / D) * jnp.sum(take(8))
    s_wp_mine = lax.dynamic_slice_in_dim(s_wp.reshape(4, GD, GD), chip * 64, 64, axis=1)
    small_g = [s_gmix, s_gv, s_ws, s_bs, _rows(s_wp_mine), s_sc, s_gffn, s_gf]
    small_w = [g_mix, g_v, w_s, b_s, w_pool, pool_scale, g_ffn, g_final]
    small_m = [m_g_mix, m_g_v, m_w_s, m_b_s, m_w_pool, m_pool_scale, m_g_ffn, m_g_final]
    small_v = [v_g_mix, v_g_v, v_w_s, v_b_s, v_w_pool, v_pool_scale, v_g_ffn, v_g_final]
    cat = lambda parts: jnp.concatenate([_rows(p) for p in parts], axis=0)
    sg = cat(small_g)
    sg, sd, snm, snv = _adamw("adamw_small", cat(small_w), sg, cat(small_m), cat(small_v), sg.shape[0])
    ssz = [p.size // LANES for p in small_w]
    soff = [sum(ssz[:i]) for i in range(len(ssz))]
    split = lambda a: [a[soff[i]:soff[i] + ssz[i]].reshape(small_w[i].shape) for i in range(len(ssz))]
    gs, ds, nms, nvs = split(sg), split(sd), split(snm), split(snv)

    def ordered(small, w_in_, w_out_, w_up_, w_down_):
        return [small[0], w_in_[None], small[1], small[2], small[3], small[4], small[5], w_out_[None], small[6],
                w_up_[None], w_down_[None], small[7]]

    return (loss, grad_x[None],
            *ordered(gs, g_in, g_out, g_up, g_down),
            *ordered(ds, d_in, d_out, d_up, d_down),
            *ordered(nms, nm_in, nm_out, nm_up, nm_down),
            *ordered(nvs, nv_in, nv_out, nv_up, nv_down))
```

```python
import functools

import jax
import jax.numpy as jnp
from jax import lax
from jax.experimental import pallas as pl
from jax.experimental.pallas import tpu as pltpu

F32 = jnp.float32
BF16 = jnp.bfloat16
EPS = 1e-6
D = 2048
A_W = 1024
HEADS = 8
HD = 128
POOL_WINDOWS = (2, 4, 8, 16)
GD = 256
IN_W = 3072
D_FF = 8192
N_CHIPS = 4
HALO = 16
EPI_ROWS = 128
LANES = 128
MIB = 2 ** 20

ADAM_LR, ADAM_B1, ADAM_B2, ADAM_EPS, ADAM_WD, ADAM_STEP = 0.001, 0.9, 0.999, 1e-08, 0.01, 10

ANY = pl.BlockSpec(memory_space=pl.ANY)
MESH = pl.DeviceIdType.MESH

NN = ((1,), (0,))
NT = ((1,), (1,))
TN = ((0,), (0,))


def _place():
    return lax.axis_index("x"), lax.axis_index("y"), lax.axis_index("c")


def _my_chip():
    return 2 * lax.axis_index("x") + lax.axis_index("y")


def _my_core():
    return lax.axis_index("c")


def _other_chips(x, y):
    return [(1 - x, y), (x, 1 - y), (1 - x, 1 - y)]


def _remote(src, dst, send_sem, recv_sem, dev):
    return pltpu.make_async_remote_copy(src_ref=src, dst_ref=dst, send_sem=send_sem, recv_sem=recv_sem,
                                        device_id=dev, device_id_type=MESH)


class _SemView:
    def __init__(self, sems, base):
        self.sems, self.base = sems, base

    @property
    def at(self):
        return self

    def __getitem__(self, i):
        return self.sems.at[self.base + i]


class _Carry:
    def __init__(self, srcs, lands, n_sems, start, finish, middle=None):
        self.srcs, self.lands, self.n_sems, self.start, self.finish = list(srcs), list(lands), n_sems, start, finish
        self.middle = middle


def _join(*carries):
    def run(which):
        def go(srcs, lands, ssem, rsem):
            so = lo = qo = 0
            for c in carries:
                if getattr(c, which) is not None:
                    getattr(c, which)(srcs[so:so + len(c.srcs)], lands[lo:lo + len(c.lands)],
                                      _SemView(ssem, qo), _SemView(rsem, qo))
                so, lo, qo = so + len(c.srcs), lo + len(c.lands), qo + c.n_sems
        return go

    middle = run("middle") if any(c.middle is not None for c in carries) else None
    return _Carry([s for c in carries for s in c.srcs], [l for c in carries for l in c.lands],
                  sum(c.n_sems for c in carries), run("start"), run("finish"), middle)


GATHER_SEMS = 7


def _gather_copies(lands, ssem, rsem, part=(0, 1, 1)):
    x, y, c = _place()
    k, kx, ky, kd = 2 * x + y, 2 * (1 - x) + y, 2 * x + (1 - y), 2 * (1 - x) + (1 - y)
    to_x, to_y, sib = (1 - x, y, c), (x, 1 - y, c), (x, y, 1 - c)
    out = []
    for w, land in enumerate(lands):
        rh = land.shape[1] // 2
        rq = rh // 2
        half = pl.ds(pl.multiple_of(c * rh, 16), rh)
        quarters = [pl.ds(pl.multiple_of(c * rh + q * rq, 16), rq) for q in range(2)]
        pa, pb, pp = part
        sent = pl.ds(pl.multiple_of(c * rh + pa * rh // pp, 16), (pb - pa) * rh // pp)

        def cp(i, piece, dev, w=w):
            return _remote(piece, piece, ssem.at[GATHER_SEMS * w + i], rsem.at[GATHER_SEMS * w + i], dev)

        out.append(dict(
            ax=cp(0, land.at[k, sent], to_x), ay=cp(1, land.at[k, sent], to_y),
            rx=cp(2, land.at[kx, quarters[0]], to_y), ry=cp(3, land.at[ky, quarters[1]], to_x),
            fx=cp(4, land.at[kx, half], sib), fy=cp(5, land.at[ky, half], sib), fd=cp(6, land.at[kd, half], sib)))
    return out


def _gather_whole(gathered):
    def start(srcs, lands, ssem, rsem):
        for d in _gather_copies(lands, ssem, rsem):
            d["ax"].start()
            d["ay"].start()

    def middle(srcs, lands, ssem, rsem):
        for d in _gather_copies(lands, ssem, rsem):
            d["ax"].wait_recv()
            d["rx"].start()
            d["fx"].start()
            d["ay"].wait_recv()
            d["ry"].start()
            d["fy"].start()

    def finish(srcs, lands, ssem, rsem):
        for d in _gather_copies(lands, ssem, rsem):
            d["rx"].wait_recv()
            d["ry"].wait_recv()
            d["fd"].start()
            for name in ("fx", "fy", "fd"):
                d[name].wait_recv()
            for cp in d.values():
                cp.wait_send()

    return _Carry([], gathered, GATHER_SEMS * len(gathered), start, finish, middle)


def _gather_near(gathered, part=(0, 1, 1)):
    def start(srcs, lands, ssem, rsem):
        for d in _gather_copies(lands, ssem, rsem, part):
            d["ax"].start()
            d["ay"].start()

    def finish(srcs, lands, ssem, rsem):
        for d in _gather_copies(lands, ssem, rsem, part):
            for name in ("ax", "ay"):
                d[name].wait_recv()
                d[name].wait_send()

    return _Carry([], gathered, GATHER_SEMS * len(gathered), start, finish)


def _gather_far(gathered):
    def start(srcs, lands, ssem, rsem):
        for d in _gather_copies(lands, ssem, rsem):
            for name in ("rx", "ry", "fx", "fy"):
                d[name].start()

    def finish(srcs, lands, ssem, rsem):
        for d in _gather_copies(lands, ssem, rsem):
            d["rx"].wait_recv()
            d["ry"].wait_recv()
            d["fd"].start()
            for name in ("fx", "fy", "fd"):
                d[name].wait_recv()
            for name in ("rx", "ry", "fx", "fy", "fd"):
                d[name].wait_send()

    return _Carry([], gathered, GATHER_SEMS * len(gathered), start, finish)


def _sibling_carry(dw):
    def copies(srcs, lands, ssem, rsem):
        x, y, c = _place()
        return [_remote(srcs[0].at[j, 1 - c], lands[0].at[j], ssem.at[j], rsem.at[j], (x, y, 1 - c))
                for j in range(N_CHIPS)]

    def start(*a):
        for cp in copies(*a):
            cp.start()

    def finish(*a):
        for cp in copies(*a):
            cp.wait()

    return _Carry([dw], [jax.ShapeDtypeStruct((N_CHIPS,) + dw.shape[-2:], dw.dtype)], N_CHIPS, start, finish)


def _chips_carry(sums, land, part=(0, 1, 1)):
    a, b, p = part
    rh = sums.shape[1]
    rows = pl.ds(a * rh // p, (b - a) * rh // p)

    def copies(srcs, lands, ssem, rsem):
        x, y, c = _place()
        out = []
        for j, (cx, cy) in enumerate(_other_chips(x, y)):
            out.append(_remote(srcs[0].at[2 * cx + cy, rows], lands[0].at[j, rows], ssem.at[j], rsem.at[j],
                               (cx, cy, c)))
        return out

    def start(*a):
        for cp in copies(*a):
            cp.start()

    def finish(*a):
        for cp in copies(*a):
            cp.wait()

    if land is None:
        land = jax.ShapeDtypeStruct((3,) + sums.shape[1:], sums.dtype)
    return _Carry([sums], [land], 3, start, finish)


def _call(name, grid, body, ins, in_specs, out_shapes, out_specs, scratch=(), vmem_mb=48, carry=None):
    n_in, n_out, n_sc = len(ins), len(out_shapes), len(scratch)
    ins, in_specs = list(ins), list(in_specs)
    out_shapes, out_specs, scratch = list(out_shapes), list(out_specs), list(scratch)
    aliases = {}
    if carry is not None:
        ins += carry.srcs
        in_specs += [ANY] * len(carry.srcs)
        for land in carry.lands:
            if not isinstance(land, jax.ShapeDtypeStruct):
                aliases[len(ins)] = len(out_shapes)
                ins.append(land)
                in_specs.append(ANY)
                land = jax.ShapeDtypeStruct(land.shape, land.dtype)
            out_shapes.append(land)
            out_specs.append(ANY)
        scratch += [pltpu.SemaphoreType.DMA((carry.n_sems,)), pltpu.SemaphoreType.DMA((carry.n_sems,))]
    n_in_all, n_out_all = len(ins), len(out_shapes)
    ins = [pltpu.with_memory_space_constraint(a, pltpu.HBM) for a in ins]

    def kbody(*refs):
        in_refs, out_refs, sc = refs[:n_in_all], refs[n_in_all:n_in_all + n_out_all], refs[n_in_all + n_out_all:]
        ids = tuple(pl.program_id(a) for a in range(len(grid)))
        if carry is not None:
            first = functools.reduce(jnp.logical_and, [i == 0 for i in ids])
            last = functools.reduce(jnp.logical_and, [i == g - 1 for i, g in zip(ids, grid)])
            comm = (in_refs[n_in:n_in + len(carry.srcs)], out_refs[n_out:], sc[n_sc], sc[n_sc + 1])

            @pl.when(first)
            def _():
                carry.start(*comm)

            if carry.middle is not None:
                step, total = ids[0], grid[0]
                for i, g in zip(ids[1:], grid[1:]):
                    step, total = step * g + i, total * g

                @pl.when(step == (2 * total) // 3)
                def _():
                    carry.middle(*comm)

        body(in_refs[:n_in], out_refs[:n_out], sc[:n_sc], ids)
        if carry is not None:
            @pl.when(last)
            def _():
                carry.finish(*comm)

    return pl.pallas_call(
        kbody, name=name, grid=grid, in_specs=in_specs, out_specs=out_specs,
        out_shape=[pltpu.HBM(o.shape, o.dtype) for o in out_shapes],
        scratch_shapes=scratch, input_output_aliases=aliases,
        compiler_params=pltpu.CompilerParams(dimension_semantics=("arbitrary",) * len(grid),
                                             vmem_limit_bytes=vmem_mb * MIB),
    )(*ins)


def _matmul(name, grid, kaxis, ins, in_specs, out_shapes, out_specs, dims, epi, tail=None, acc_shape=None, nc=512,
            vmem_mb=48, carry=None):
    nk = grid[kaxis] if kaxis is not None else 1

    def body(in_refs, out_refs, scratch, ids):
        a_ref, b_ref = in_refs[0], in_refs[1]
        n = b_ref.shape[0] if dims == NT else b_ref.shape[1]

        def prod(a, c0):
            b = b_ref[c0:c0 + nc, :] if dims == NT else b_ref[:, c0:c0 + nc]
            return lax.dot_general(a, b, (dims, ((), ())), preferred_element_type=F32)

        if kaxis is None:
            a = a_ref[...]
            for c0 in range(0, n, nc):
                epi(prod(a, c0), slice(c0, c0 + nc), ids, in_refs[2:], out_refs)
            if tail is not None:
                tail(ids, in_refs[2:], out_refs)
        else:
            acc = scratch[0]
            tm = acc.shape[0]
            last = ids[kaxis] == nk - 1
            _zero_when(ids[kaxis] == 0, acc)

            @pl.when(jnp.logical_not(last))
            def _():
                a = a_ref[...]
                for c0 in range(0, n, nc):
                    acc[:, c0:c0 + nc] += prod(a, c0)

            @pl.when(last)
            def _():
                block = epi(acc, ids, in_refs[2:], out_refs)
                for r0 in range(0, tm, tm // 2):
                    a = a_ref[r0:r0 + tm // 2, :]
                    for c0 in range(0, n, nc):
                        acc[r0:r0 + tm // 2, c0:c0 + nc] += prod(a, c0)
                    for q0 in range(r0, r0 + tm // 2, EPI_ROWS):
                        block(slice(q0, q0 + EPI_ROWS))

    return _call(name, grid, body, ins, in_specs, out_shapes, out_specs,
                 [pltpu.VMEM(acc_shape, F32)] if kaxis is not None else [], vmem_mb, carry)


def _row_rsqrt(xf):
    return lax.rsqrt(jnp.mean(xf * xf, axis=-1, keepdims=True) + EPS)


def _norm_bwd(dh, xf, g, resid):
    r = _row_rsqrt(xf)
    xh = xf * r
    dg = jnp.sum(dh * xh, axis=0, keepdims=True)
    dxh = dh * g
    dx = resid + r * (dxh - xh * jnp.mean(dxh * xh, axis=-1, keepdims=True))
    return dx, dg


def _zero_when(first, *refs):
    @pl.when(first)
    def _():
        for ref in refs:
            ref[...] = jnp.zeros_like(ref)


_GELU_K = 0.7978845608028654
_GELU_C = 0.044715


def _gelu(x):
    t = jnp.tanh(_GELU_K * (x + _GELU_C * x * x * x))
    return 0.5 * x * (1.0 + t)


def _gelu_and_grad(x):
    x2 = x * x
    t = jnp.tanh(_GELU_K * (x + _GELU_C * x * x2))
    g = 0.5 * x * (1.0 + t)
    dg = 0.5 * (1.0 + t) + 0.5 * x * (1.0 - t * t) * (_GELU_K * (1.0 + 3.0 * _GELU_C * x2))
    return g, dg


def _window_sum(ext, w, causal):
    n = ext.shape[0]
    s, d = ext, 1
    while d < w:
        s = s + pltpu.roll(s, d if causal else n - d, 0)
        d *= 2
    return s


def _inv_count(t, w):
    return 1.0 / jnp.minimum(t + 1, w).astype(F32)


def _pooled(z_ref, zh_ref, g, w, i, tm):
    cols = slice(GD * g, GD * (g + 1))
    zb = z_ref[:, cols]
    zh = jnp.where(i > 0, zh_ref[:, cols], 0.0)
    ext = jnp.concatenate([zh, zb], axis=0)
    s = _window_sum(ext, w, True)[HALO:, :]
    t = i * tm + lax.broadcasted_iota(jnp.int32, (tm, 1), 0)
    return s * _inv_count(t, w) - zb


def _full(shape, n_axes=1):
    return pl.BlockSpec(shape, lambda *ids: (0,) * len(shape))


def _norm1(x, g, tm, carry):
    s = x.shape[0]

    def body(ins, outs, scratch, ids):
        xf = ins[0][...]
        outs[0][...] = (xf * _row_rsqrt(xf) * ins[1][...]).astype(BF16)

    row = pl.BlockSpec((tm, D), lambda i: (i, 0))
    return _call("norm1", (s // tm,), body, [x, g], [row, _full((1, D))],
                 [jax.ShapeDtypeStruct((s, D), BF16)], [row], vmem_mb=32, carry=carry)


def _mixer_fwd(proj, wt, bst, gv, wpool, scale, tm, carry):
    s = proj.shape[0]
    nq = tm // HD

    def body(ins, outs, scratch, ids):
        up_ref, vp_ref, z_ref, zh_ref, wt_ref, bst_ref, gv_ref, wp_ref, sc_ref = ins
        out_ref = outs[0]
        i = ids[0]
        for h in range(HEADS):
            cols = slice(HD * h, HD * (h + 1))
            gvh = gv_ref[:, cols]
            bcol = bst_ref[h]
            wth = wt_ref[h]

            def chunk(q, c_):
                rows = pl.ds(pl.multiple_of(q * HD, HD), HD)
                u = _gelu(up_ref[rows, cols])
                v = _gelu(vp_ref[rows, cols])
                vh = (v * _row_rsqrt(v) * gvh).astype(BF16)
                mixed = jnp.dot(wth, vh, preferred_element_type=F32) + bcol
                out_ref[rows, cols] = (u * mixed).astype(BF16)
                return c_

            lax.fori_loop(0, nq, chunk, 0, unroll=True)
        for g, w in enumerate(POOL_WINDOWS):
            cols = slice(GD * g, GD * (g + 1))
            pooled = _pooled(z_ref, zh_ref, g, w, i, tm)
            yv = jnp.dot(pooled.astype(BF16), wp_ref[g], preferred_element_type=F32)
            out_ref[:, A_W + GD * g:A_W + GD * (g + 1)] = (yv * sc_ref[:, cols]).astype(BF16)

    hb = tm // HALO
    return _call(
        "mixer_fwd", (s // tm,), body, [proj, proj, proj, proj, wt, bst, gv, wpool, scale],
        [pl.BlockSpec((tm, A_W), lambda i: (i, 0)),
         pl.BlockSpec((tm, A_W), lambda i: (i, 1)),
         pl.BlockSpec((tm, A_W), lambda i: (i, 2)),
         pl.BlockSpec((HALO, A_W), lambda i: (jnp.maximum(i * hb - 1, 0), 2)),
         _full((HEADS, HD, HD)), _full((HEADS, HD, HD)), _full((1, A_W)), _full((4, GD, GD)), _full((1, A_W))],
        [jax.ShapeDtypeStruct((s, D), BF16)], [pl.BlockSpec((tm, D), lambda i: (i, 0))], vmem_mb=40, carry=carry)


def _mixer_bwd(proj, dmix, wt, wtt, bst, gv, wpool, scale, tm, carry):
    s = proj.shape[0]
    nb = s // tm
    nq = tm // HD
    hb = tm // HALO

    def body(ins, outs, scratch, ids):
        (up_ref, vp_ref, z_ref, zh_ref, doa_ref, dob_ref, dobh_ref, wt_ref, wtt_ref, bst_ref, gv_ref, wp_ref,
         sc_ref) = ins
        dproj_ref, dws_ref, dbs_ref, dgv_ref, dwp_ref, dsc_ref = outs
        dbfull = scratch[0]
        i = ids[0]
        last = i == nb - 1
        _zero_when(i == 0, dws_ref, dbfull, dgv_ref, dwp_ref, dsc_ref)

        for h in range(HEADS):
            cols = slice(HD * h, HD * (h + 1))
            gvh = gv_ref[:, cols]
            bcol = bst_ref[h]
            wth = wt_ref[h]
            wtth = wtt_ref[h]

            def chunk(q, c_):
                rows = pl.ds(pl.multiple_of(q * HD, HD), HD)
                u, du_dup = _gelu_and_grad(up_ref[rows, cols])
                v, dv_dvp = _gelu_and_grad(vp_ref[rows, cols])
                rv = _row_rsqrt(v)
                vn = v * rv
                vh = (vn * gvh).astype(BF16)
                mixed = jnp.dot(wth, vh, preferred_element_type=F32) + bcol
                doa = doa_ref[rows, cols].astype(F32)
                dmx = doa * u
                dmxb = dmx.astype(BF16)
                dbfull[h] += dmx
                dws_ref[h] += lax.dot_general(dmxb, vh, (NT, ((), ())), preferred_element_type=F32)
                dvh = jnp.dot(wtth, dmxb, preferred_element_type=F32)
                dgv_ref[:, cols] += jnp.sum(dvh * vn, axis=0, keepdims=True)
                dvn = dvh * gvh
                dv = rv * (dvn - vn * jnp.mean(dvn * vn, axis=-1, keepdims=True))
                dproj_ref[rows, cols] = (doa * mixed * du_dup).astype(BF16)
                dproj_ref[rows, A_W + HD * h:A_W + HD * (h + 1)] = (dv * dv_dvp).astype(BF16)
                return c_

            lax.fori_loop(0, nq, chunk, 0, unroll=True)

        t = i * tm + lax.broadcasted_iota(jnp.int32, (tm, 1), 0)
        th = (i + 1) * tm + lax.broadcasted_iota(jnp.int32, (HALO, 1), 0)
        for g, w in enumerate(POOL_WINDOWS):
            cols = slice(GD * g, GD * (g + 1))
            wpg = wp_ref[g]
            scg = sc_ref[:, cols]
            pb = _pooled(z_ref, zh_ref, g, w, i, tm).astype(BF16)
            ypre = jnp.dot(pb, wpg, preferred_element_type=F32)
            dob = dob_ref[:, cols].astype(F32)
            dsc_ref[:, cols] += jnp.sum(dob * ypre, axis=0, keepdims=True)
            dyb = (dob * scg).astype(BF16)
            dwp_ref[g] += lax.dot_general(pb, dyb, (TN, ((), ())), preferred_element_type=F32)
            dpo = lax.dot_general(dyb, wpg, (NT, ((), ())), preferred_element_type=F32)
            dyh = (dobh_ref[:, cols].astype(F32) * scg).astype(BF16)
            dpoh = lax.dot_general(dyh, wpg, (NT, ((), ())), preferred_element_type=F32)
            dpoh = jnp.where(last, 0.0, dpoh * _inv_count(th, w))
            ext = jnp.concatenate([dpo * _inv_count(t, w), dpoh], axis=0)
            dz = _window_sum(ext, w, False)[:tm, :] - dpo
            dproj_ref[:, 2 * A_W + GD * g:2 * A_W + GD * (g + 1)] = dz.astype(BF16)

        @pl.when(last)
        def _():
            r = lax.broadcasted_iota(jnp.int32, (HD, HD), 0)
            c = lax.broadcasted_iota(jnp.int32, (HD, HD), 1)
            for h in range(HEADS):
                dws_ref[h] = jnp.where(r >= c, dws_ref[h], 0.0)
                dbs_ref[h] = jnp.sum(dbfull[h], axis=-1, keepdims=True)

    return _call(
        "mixer_bwd", (nb,), body, [proj, proj, proj, proj, dmix, dmix, dmix, wt, wtt, bst, gv, wpool, scale],
        [pl.BlockSpec((tm, A_W), lambda i: (i, 0)),
         pl.BlockSpec((tm, A_W), lambda i: (i, 1)),
         pl.BlockSpec((tm, A_W), lambda i: (i, 2)),
         pl.BlockSpec((HALO, A_W), lambda i: (jnp.maximum(i * hb - 1, 0), 2)),
         pl.BlockSpec((tm, A_W), lambda i: (i, 0)),
         pl.BlockSpec((tm, A_W), lambda i: (i, 1)),
         pl.BlockSpec((HALO, A_W), lambda i: (jnp.minimum((i + 1) * hb, s // HALO - 1), 1)),
         _full((HEADS, HD, HD)), _full((HEADS, HD, HD)), _full((HEADS, HD, HD)), _full((1, A_W)),
         _full((4, GD, GD)), _full((1, A_W))],
        [jax.ShapeDtypeStruct((s, IN_W), BF16),
         jax.ShapeDtypeStruct((HEADS, HD, HD), F32),
         jax.ShapeDtypeStruct((HEADS, HD, 1), F32),
         jax.ShapeDtypeStruct((1, A_W), F32),
         jax.ShapeDtypeStruct((4, GD, GD), F32),
         jax.ShapeDtypeStruct((1, A_W), F32)],
        [pl.BlockSpec((tm, IN_W), lambda i: (i, 0)),
         _full((HEADS, HD, HD)), _full((HEADS, HD, 1)), _full((1, A_W)), _full((4, GD, GD)), _full((1, A_W))],
        [pltpu.VMEM((HEADS, HD, HD), F32)], vmem_mb=48, carry=carry)


def _inproj(h1, win_g, tm, carry):
    s = h1.shape[0]
    cw = IN_W // N_CHIPS

    def epi(p, cols, ids, extra, outs):
        outs[0][:, cols] = p

    return _matmul(
        "inproj", (N_CHIPS, s // tm), None, [h1, win_g],
        [pl.BlockSpec((tm, D), lambda j, i: (i, 0)), pl.BlockSpec((None, D, cw), lambda j, i: (j, 0, 0))],
        [jax.ShapeDtypeStruct((s, IN_W), F32)], [pl.BlockSpec((tm, cw), lambda j, i: (i, j))], NN, epi,
        nc=256, vmem_mb=32, carry=carry)


def _outproj(mixed, wout, x, g_ffn, tm, carry):
    s = x.shape[0]

    def epi(p, cols, ids, extra, outs):
        outs[0][:, cols] = extra[0][:, cols] + p

    def tail(ids, extra, outs):
        x2 = outs[0][...]
        outs[1][...] = (x2 * _row_rsqrt(x2) * extra[1][...]).astype(BF16)

    row = pl.BlockSpec((tm, D), lambda i: (i, 0))
    return _matmul(
        "outproj", (s // tm,), None, [mixed, wout, x, g_ffn], [row, _full((D, D)), row, _full((1, D))],
        [jax.ShapeDtypeStruct((s, D), F32), jax.ShapeDtypeStruct((s, D), BF16)], [row, row], NN, epi, tail,
        vmem_mb=48, carry=carry)


def _up(h2, wup_g, tm, tn, carry):
    s = h2.shape[0]
    per = D // tn

    def epi(p, cols, ids, extra, outs):
        a = jnp.maximum(p, 0.0)
        outs[0][:, cols] = (a * a).astype(BF16)

    return _matmul(
        "up", (D_FF // tn, s // tm), None, [h2, wup_g],
        [pl.BlockSpec((tm, D), lambda j, i: (i, 0)), pl.BlockSpec((None, D, tn), lambda j, i: (j // per, 0, j % per))],
        [jax.ShapeDtypeStruct((s, D_FF), BF16)], [pl.BlockSpec((tm, tn), lambda j, i: (i, j))], NN, epi,
        vmem_mb=48, carry=carry)


def _down(act, wdown, x2, tgt, g_final, tm, tk):
    s = x2.shape[0]

    def epi(acc, ids, extra, outs):
        x2_ref, t_ref, g_ref = extra
        dxb_ref, dgf_ref, loss_ref = outs
        g = g_ref[...]
        _zero_when(ids[0] == 0, dgf_ref, loss_ref)

        def block(rows):
            x3 = x2_ref[rows, :] + acc[rows, :]
            r = _row_rsqrt(x3)
            xh = x3 * r
            diff = xh * g - t_ref[rows, :]
            dy = diff * (1.0 / D)
            dxh = dy * g
            dx = r * (dxh - xh * jnp.mean(dxh * xh, axis=-1, keepdims=True))
            dxb_ref[rows, :] = dx.astype(BF16)
            dgf_ref[...] += jnp.sum(dy * xh, axis=0, keepdims=True)
            loss_ref[...] += jnp.sum(diff * diff, axis=0, keepdims=True)

        return block

    row = pl.BlockSpec((tm, D), lambda i, k: (i, 0))
    vec = pl.BlockSpec((1, D), lambda i, k: (0, 0))
    return _matmul(
        "down", (s // tm, D_FF // tk), 1, [act, wdown, x2, tgt, g_final],
        [pl.BlockSpec((tm, tk), lambda i, k: (i, k)), pl.BlockSpec((tk, D), lambda i, k: (k, 0)), row, row, vec],
        [jax.ShapeDtypeStruct((s, D), BF16), jax.ShapeDtypeStruct((1, D), F32), jax.ShapeDtypeStruct((1, D), F32)],
        [row, vec, vec], NN, epi, acc_shape=(tm, D), vmem_mb=56)


def _dact(dx3b, wdown, act, tm, tn, carry):
    s = dx3b.shape[0]

    def epi(p, cols, ids, extra, outs):
        outs[0][:, cols] = (p * (2.0 * jnp.sqrt(extra[0][:, cols].astype(F32)))).astype(BF16)

    tile = pl.BlockSpec((tm, tn), lambda j, i: (i, j))
    return _matmul(
        "dact", (D_FF // tn, s // tm), None, [dx3b, wdown, act],
        [pl.BlockSpec((tm, D), lambda j, i: (i, 0)), pl.BlockSpec((tn, D), lambda j, i: (j, 0)), tile],
        [jax.ShapeDtypeStruct((s, D_FF), BF16)], [tile], NT, epi, vmem_mb=56, carry=carry)


def _dweight(name, lhs, rhs, n_shards, rows, cols, tm, nc=512, carry=None):
    s = lhs.shape[0]

    def epi(p, cs, ids, extra, outs):
        outs[0][:, cs] = p.astype(BF16)

    return _matmul(
        name, (n_shards, rows // tm), None, [lhs, rhs],
        [pl.BlockSpec((s, tm), lambda j, i: (0, i)), pl.BlockSpec((s, cols), lambda j, i: (0, j))],
        [jax.ShapeDtypeStruct((n_shards, rows, cols), BF16)],
        [pl.BlockSpec((None, tm, cols), lambda j, i: (j, i, 0))], TN, epi, nc=nc, vmem_mb=56, carry=carry)


def _dh2(da, wup_g, x2, dx3, g_ffn, tm, tk, carry):
    s = x2.shape[0]
    per = D // tk

    def epi(acc, ids, extra, outs):
        x2_ref, dx3_ref, g_ref = extra
        g = g_ref[...]
        _zero_when(ids[0] == 0, outs[1])

        def block(rows):
            dx, dg = _norm_bwd(acc[rows, :], x2_ref[rows, :], g, dx3_ref[rows, :].astype(F32))
            outs[0][rows, :] = dx.astype(BF16)
            outs[1][...] += dg

        return block

    row = pl.BlockSpec((tm, D), lambda i, k: (i, 0))
    vec = pl.BlockSpec((1, D), lambda i, k: (0, 0))
    return _matmul(
        "dh2", (s // tm, D_FF // tk), 1, [da, wup_g, x2, dx3, g_ffn],
        [pl.BlockSpec((tm, tk), lambda i, k: (i, k)),
         pl.BlockSpec((None, D, tk), lambda i, k: (k // per, 0, k % per)), row, row, vec],
        [jax.ShapeDtypeStruct((s, D), BF16), jax.ShapeDtypeStruct((1, D), F32)],
        [row, vec], NT, epi, acc_shape=(tm, D), vmem_mb=56, carry=carry)


def _dmixed(dx2b, wout, tm, carry):
    s = dx2b.shape[0]

    def epi(p, cols, ids, extra, outs):
        outs[0][:, cols] = p.astype(BF16)

    row = pl.BlockSpec((tm, D), lambda i: (i, 0))
    return _matmul(
        "dmixed", (s // tm,), None, [dx2b, wout], [row, _full((D, D))],
        [jax.ShapeDtypeStruct((s, D), BF16)], [row], NT, epi, vmem_mb=40, carry=carry)


def _dh1(dproj, win_g, x, dx2, g_mix, tm, carry):
    s = x.shape[0]
    cw = IN_W // N_CHIPS

    def epi(acc, ids, extra, outs):
        x_ref, dx2_ref, g_ref = extra
        g = g_ref[...]
        _zero_when(ids[0] == 0, outs[1])

        def block(rows):
            dx, dg = _norm_bwd(acc[rows, :], x_ref[rows, :], g, dx2_ref[rows, :].astype(F32))
            outs[0][rows, :] = dx
            outs[1][...] += dg

        return block

    row = pl.BlockSpec((tm, D), lambda i, j: (i, 0))
    vec = pl.BlockSpec((1, D), lambda i, j: (0, 0))
    return _matmul(
        "dh1", (s // tm, N_CHIPS), 1, [dproj, win_g, x, dx2, g_mix],
        [pl.BlockSpec((tm, cw), lambda i, j: (i, j)), pl.BlockSpec((None, D, cw), lambda i, j: (j, 0, 0)),
         row, row, vec],
        [jax.ShapeDtypeStruct((s, D), F32), jax.ShapeDtypeStruct((1, D), F32)],
        [row, vec], NT, epi, acc_shape=(tm, D), vmem_mb=52, carry=carry)


def _cast_place(name, w, tr, carry=None):
    rows, cols = w.shape

    def body(ins, outs, scratch, ids):
        outs[0][...] = ins[0][...].astype(BF16)

    return _call(name, (rows // tr,), body, [w], [pl.BlockSpec((tr, cols), lambda r: (r, 0))],
                 [jax.ShapeDtypeStruct((N_CHIPS, rows, cols), BF16)],
                 [pl.BlockSpec((None, tr, cols), lambda r: (_my_chip(), r, 0))], vmem_mb=32, carry=carry)


def _small_allreduce(part):
    rows = part.shape[0]
    rh = rows // 2

    def body(p_ref, o_ref, sib_ref, slots, send_sems, recv_sems):
        x, y, c = _place()
        k = 2 * x + y
        sib = (x, y, 1 - c)
        half = pl.ds(pl.multiple_of(c * rh, 8), rh)
        cp = _remote(p_ref, sib_ref, send_sems.at[0], recv_sems.at[0], sib)
        cp.start()
        cp.wait()
        slots[k] = p_ref[half, :] + sib_ref[half, :]
        cps = []
        for j, (cx, cy) in enumerate(_other_chips(x, y)):
            cp = _remote(slots.at[k], slots.at[k], send_sems.at[1 + j], recv_sems.at[1 + j], (cx, cy, c))
            cp.start()
            cps.append(cp)
        for cp in cps:
            cp.wait()
        o_ref[half, :] = ((slots[0] + slots[1]) + slots[2]) + slots[3]
        cp = _remote(o_ref.at[half], o_ref.at[half], send_sems.at[4], recv_sems.at[4], sib)
        cp.start()
        cp.wait()

    vm = pl.BlockSpec(memory_space=pltpu.VMEM)
    return pl.pallas_call(
        body, name="small_allreduce", in_specs=[vm], out_specs=vm,
        out_shape=jax.ShapeDtypeStruct(part.shape, F32),
        scratch_shapes=[pltpu.VMEM(part.shape, F32), pltpu.VMEM((N_CHIPS, rh, LANES), F32),
                        pltpu.SemaphoreType.DMA((5,)), pltpu.SemaphoreType.DMA((5,))],
        compiler_params=pltpu.CompilerParams(vmem_limit_bytes=32 * MIB),
    )(part)


def _comm_only(name, carry):
    ns, nl = len(carry.srcs), len(carry.lands)
    lands_in = [l for l in carry.lands if not isinstance(l, jax.ShapeDtypeStruct)]
    assert len(lands_in) in (0, nl)

    def body(*refs):
        srcs = refs[:ns]
        lands = refs[ns + len(lands_in):ns + len(lands_in) + nl]
        ssem, rsem = refs[ns + len(lands_in) + nl:]
        carry.start(srcs, lands, ssem, rsem)
        if carry.middle is not None:
            carry.middle(srcs, lands, ssem, rsem)
        carry.finish(srcs, lands, ssem, rsem)

    return pl.pallas_call(
        body, name=name, in_specs=[ANY] * (ns + len(lands_in)), out_specs=[ANY] * nl,
        out_shape=[jax.ShapeDtypeStruct(l.shape, l.dtype) for l in carry.lands],
        input_output_aliases={ns + i: i for i in range(len(lands_in))},
        scratch_shapes=[pltpu.SemaphoreType.DMA((carry.n_sems,)), pltpu.SemaphoreType.DMA((carry.n_sems,))],
    )(*carry.srcs, *lands_in)


def _share_carry(grads):
    n = len(grads)

    def copies(srcs, lands, ssem, rsem):
        x, y, c = _place()
        return [_remote(lands[w].at[c], lands[w].at[c], ssem.at[w], rsem.at[w], (x, y, 1 - c)) for w in range(n)]

    def start(*a):
        for cp in copies(*a):
            cp.start()

    def finish(*a):
        for cp in copies(*a):
            cp.wait()

    return _Carry([], grads, n, start, finish)


def _add_sibling(dw, got, tr):
    _, _, rh, cols = dw.shape

    def body(ins, outs, scratch, ids):
        outs[0][...] = (ins[0][...].astype(F32) + ins[1][...].astype(F32)).astype(BF16)

    other = lambda j: (_my_chip() + 1 + j) % N_CHIPS
    blk = pl.BlockSpec((None, tr, cols), lambda j, r: (other(j), r, 0))
    return _call("add_sibling", (N_CHIPS - 1, rh // tr), body, [dw, got],
                 [pl.BlockSpec((None, None, tr, cols), lambda j, r: (other(j), _my_core(), r, 0)), blk],
                 [jax.ShapeDtypeStruct((N_CHIPS, rh, cols), BF16)], [blk], vmem_mb=32)[0]


def _add_chips(dw, sib, got, tr, carry=None):
    _, _, rh, cols = dw.shape

    def body(ins, outs, scratch, ids):
        b = ins[2][...].astype(F32)
        outs[0][...] = (((ins[0][...].astype(F32) + ins[1][...].astype(F32)) + b[0]) + b[1]) + b[2]

    return _call("add_chips", (rh // tr,), body, [dw, sib, got],
                 [pl.BlockSpec((None, None, tr, cols), lambda r: (_my_chip(), _my_core(), r, 0)),
                  pl.BlockSpec((None, tr, cols), lambda r: (_my_chip(), r, 0)),
                  pl.BlockSpec((3, tr, cols), lambda r: (0, r, 0))],
                 [jax.ShapeDtypeStruct((2, rh, cols), F32)],
                 [pl.BlockSpec((None, tr, cols), lambda r: (_my_core(), r, 0))], vmem_mb=32, carry=carry)


def _adamw_math(w, g, m, v):
    m = ADAM_B1 * m + (1.0 - ADAM_B1) * g
    v = ADAM_B2 * v + (1.0 - ADAM_B2) * (g * g)
    m_hat = m / (1.0 - ADAM_B1 ** ADAM_STEP)
    v_hat = v / (1.0 - ADAM_B2 ** ADAM_STEP)
    delta = -ADAM_LR * (m_hat / (jnp.sqrt(v_hat) + ADAM_EPS) + ADAM_WD * w)
    return delta, m, v


def _adamw(name, w, g, m, v, tr, carry=None):
    rows, cols = w.shape

    def body(ins, outs, scratch, ids):
        g_val = ins[1][...]
        outs[0][...] = g_val
        outs[1][...], outs[2][...], outs[3][...] = _adamw_math(ins[0][...], g_val, ins[2][...], ins[3][...])

    blk = pl.BlockSpec((tr, cols), lambda r: (r, 0))
    return _call(name, (rows // tr,), body, [w, g, m, v], [blk] * 4,
                 [jax.ShapeDtypeStruct(w.shape, F32)] * 4, [blk] * 4, vmem_mb=40, carry=carry)


def _rows(a):
    return a.reshape(-1, LANES)


def kernel(x, g_mix, w_in, g_v, w_s, b_s, w_pool, pool_scale, w_out, g_ffn, w_up, w_down, g_final, loss_target, m_g_mix, m_w_in, m_g_v, m_w_s, m_b_s, m_w_pool, m_pool_scale, m_w_out, m_g_ffn, m_w_up, m_w_down, m_g_final, v_g_mix, v_w_in, v_g_v, v_w_s, v_b_s, v_w_pool, v_pool_scale, v_w_out, v_g_ffn, v_w_up, v_w_down, v_g_final):
    tm = 512
    xs = x[0]
    tgt = loss_target[0]
    chip = _my_chip()

    win_g = _cast_place("cast_w_in", w_in[0], 256)[0]
    wpool_g = _cast_place("cast_w_pool", w_pool[0].reshape(4 * 64, GD), 128)[0]
    wout_g = _cast_place("cast_w_out", w_out[0], 128)[0]
    wdown_g, win_g, wpool_g = _cast_place("cast_w_down", w_down[0], 256, _gather_near([win_g, wpool_g], (0, 1, 2)))
    wup_g, win_g, wpool_g = _cast_place("cast_w_up", w_up[0], 256, _gather_near([win_g, wpool_g], (1, 2, 2)))
    h1, win_g, wpool_g = _norm1(xs, g_mix, tm, _gather_far([win_g, wpool_g]))
    wpool_f = wpool_g.reshape(N_CHIPS, 4, 64, GD).transpose(1, 0, 2, 3).reshape(4, GD, GD)
    tril = jnp.tril(jnp.ones((HD, HD), dtype=bool))
    wt = jnp.where(tril[None], w_s[0], 0.0).astype(BF16)
    wtt = wt.transpose(0, 2, 1)
    bst = jnp.broadcast_to(b_s[0][:, :, None], (HEADS, HD, HD))
    gfin = g_final.reshape(1, D)

    proj, wout_g = _inproj(h1, win_g, tm, _gather_whole([wout_g]))
    mixed, wup_g = _mixer_fwd(proj, wt, bst, g_v, wpool_f, pool_scale, tm, _gather_near([wup_g]))
    wout_f = wout_g.reshape(D, D)
    x2, h2, wup_g = _outproj(mixed, wout_f, xs, g_ffn, 256, _gather_far([wup_g]))
    act, wdown_g = _up(h2, wup_g, 2 * tm, 2048, _gather_whole([wdown_g]))
    wdown_f = wdown_g.reshape(D_FF, D)
    dx3b, dgf, lossv = _down(act, wdown_f, x2, tgt, gfin, tm, 2048)

    halves = lambda dw, rows, cols: dw.reshape(N_CHIPS, 2, rows // (2 * N_CHIPS), cols)
    cw = IN_W // N_CHIPS
    dwdown = halves(_dweight("dw_down", act, dx3b, 1, D_FF, D, 512)[0], D_FF, D)
    da, sib_down = _dact(dx3b, wdown_f, act, 2 * tm, 2048, _sibling_carry(dwdown))
    sum_down = _add_sibling(dwdown, sib_down, 256)
    dwup, got_down = _dweight("dw_up", h2, da, N_CHIPS, D, D, 512, carry=_chips_carry(sum_down, None, (0, 3, 4)))
    dwup = halves(dwup, D_FF, D)
    dx2b, dgffn, got_down, sib_up = _dh2(da, wup_g, x2, dx3b, g_ffn, tm, 2048,
                                         _join(_chips_carry(sum_down, got_down, (3, 4, 4)), _sibling_carry(dwup)))
    half_down = _add_chips(dwdown, sib_down, got_down, 256)[0]
    sum_up = _add_sibling(dwup, sib_up, 256)
    dwout, got_up = _dweight("dw_out", mixed, dx2b, 1, D, D, 512, carry=_chips_carry(sum_up, None, (0, 1, 4)))
    dwout = halves(dwout, D, D)
    dmix, got_up, sib_out, half_down = _dmixed(
        dx2b, wout_f, 256,
        _join(_chips_carry(sum_up, got_up, (1, 2, 4)), _sibling_carry(dwout), _share_carry([half_down])))
    sum_out = _add_sibling(dwout, sib_out, 256)
    g_down, d_down, nm_down, nv_down = _adamw("adamw_down", w_down[0], half_down.reshape(D, D), m_w_down[0],
                                              v_w_down[0], 128)
    dproj, dws, dbs, dgv, dwp, dsc, got_up = _mixer_bwd(proj, dmix, wt, wtt, bst, g_v, wpool_f, pool_scale, tm,
                                                        _chips_carry(sum_up, got_up, (2, 4, 4)))
    dwin, got_out = _dweight("dw_in", h1, dproj, N_CHIPS, D, cw, 512, nc=256, carry=_chips_carry(sum_out, None))
    dwin = halves(dwin, N_CHIPS * D, cw)
    half_up, sib_in = _add_chips(dwup, sib_up, got_up, 256, _sibling_carry(dwin))
    sum_in = _add_sibling(dwin, sib_in, 256)
    half_out = _add_chips(dwout, sib_out, got_out, 256)[0]
    grad_x, dgmix, got_in, half_out, half_up = _dh1(
        dproj, win_g, xs, dx2b, g_mix, tm, _join(_chips_carry(sum_in, None), _share_carry([half_out, half_up])))
    g_up, d_up, nm_up, nv_up = _adamw("adamw_up", w_up[0], half_up.reshape(D, D), m_w_up[0], v_w_up[0], 128)
    half_in = _add_chips(dwin, sib_in, got_in, 256)[0]
    half_in = _comm_only("share_half_in", _share_carry([half_in]))[0]
    g_out, d_out, nm_out, nv_out = _adamw("adamw_out", w_out[0], half_out.reshape(D // N_CHIPS, D), m_w_out[0],
                                          v_w_out[0], 128)
    g_in, d_in, nm_in, nv_in = _adamw("adamw_in", w_in[0], half_in.reshape(D, cw), m_w_in[0], v_w_in[0], 128)

    pieces = [dgmix, dgv, dws, dbs, dwp, dsc, dgffn, dgf, lossv, jnp.zeros((8 * LANES,), F32)]
    sizes = [p.size // LANES for p in pieces]
    tot = _small_allreduce(jnp.concatenate([_rows(p) for p in pieces], axis=0))
    offs = [sum(sizes[:i]) for i in range(len(sizes))]
    take = lambda i: tot[offs[i]:offs[i] + sizes[i]]
    s_gmix, s_gv, s_ws, s_bs, s_wp, s_sc, s_gffn, s_gf = [take(i) for i in range(8)]
    loss = (0.5 / D) * jnp.sum(take(8))
    s_wp_mine = lax.dynamic_slice_in_dim(s_wp.reshape(4, GD, GD), chip * 64, 64, axis=1)
    small_g = [s_gmix, s_gv, s_ws, s_bs, _rows(s_wp_mine), s_sc, s_gffn, s_gf]
    small_w = [g_mix, g_v, w_s, b_s, w_pool, pool_scale, g_ffn, g_final]
    small_m = [m_g_mix, m_g_v, m_w_s, m_b_s, m_w_pool, m_pool_scale, m_g_ffn, m_g_final]
    small_v = [v_g_mix, v_g_v, v_w_s, v_b_s, v_w_pool, v_pool_scale, v_g_ffn, v_g_final]
    cat = lambda parts: jnp.concatenate([_rows(p) for p in parts], axis=0)
    sg = cat(small_g)
    sg, sd, snm, snv = _adamw("adamw_small", cat(small_w), sg, cat(small_m), cat(small_v), sg.shape[0])
    ssz = [p.size // LANES for p in small_w]
    soff = [sum(ssz[:i]) for i in range(len(ssz))]
    split = lambda a: [a[soff[i]:soff[i] + ssz[i]].reshape(small_w[i].shape) for i in range(len(ssz))]
    gs, ds, nms, nvs = split(sg), split(sd), split(snm), split(snv)

    def ordered(small, w_in_, w_out_, w_up_, w_down_):
        return [small[0], w_in_[None], small[1], small[2], small[3], small[4], small[5], w_out_[None], small[6],
                w_up_[None], w_down_[None], small[7]]

    return (loss, grad_x[None],
            *ordered(gs, g_in, g_out, g_up, g_down),
            *ordered(ds, d_in, d_out, d_up, d_down),
            *ordered(nms, nm_in, nm_out, nm_up, nm_down),
            *ordered(nvs, nv_in, nv_out, nv_up, nv_down))
```

```python
import functools

import jax
import jax.numpy as jnp
from jax import lax
from jax.experimental import pallas as pl
from jax.experimental.pallas import tpu as pltpu

F32 = jnp.float32
BF16 = jnp.bfloat16
EPS = 1e-6
D = 2048
A_W = 1024
HEADS = 8
HD = 128
POOL_WINDOWS = (2, 4, 8, 16)
GD = 256
IN_W = 3072
D_FF = 8192
N_CHIPS = 4
HALO = 16
EPI_ROWS = 128
LAST_STEP_PARTS = 2
LANES = 128
MIB = 2 ** 20

ADAM_LR, ADAM_B1, ADAM_B2, ADAM_EPS, ADAM_WD, ADAM_STEP = 0.001, 0.9, 0.999, 1e-08, 0.01, 10

ANY = pl.BlockSpec(memory_space=pl.ANY)
MESH = pl.DeviceIdType.MESH

NN = ((1,), (0,))
NT = ((1,), (1,))
TN = ((0,), (0,))


def _place():
    return lax.axis_index("x"), lax.axis_index("y"), lax.axis_index("c")


def _my_chip():
    return 2 * lax.axis_index("x") + lax.axis_index("y")


def _my_core():
    return lax.axis_index("c")


def _other_chips(x, y):
    return [(1 - x, y), (x, 1 - y), (1 - x, 1 - y)]


def _remote(src, dst, send_sem, recv_sem, dev):
    return pltpu.make_async_remote_copy(src_ref=src, dst_ref=dst, send_sem=send_sem, recv_sem=recv_sem,
                                        device_id=dev, device_id_type=MESH)


class _SemView:
    def __init__(self, sems, base):
        self.sems, self.base = sems, base

    @property
    def at(self):
        return self

    def __getitem__(self, i):
        return self.sems.at[self.base + i]


class _Carry:
    def __init__(self, srcs, lands, n_sems, start, finish, middle=None):
        self.srcs, self.lands, self.n_sems, self.start, self.finish = list(srcs), list(lands), n_sems, start, finish
        self.middle = middle


def _join(*carries):
    def run(which):
        def go(srcs, lands, ssem, rsem):
            so = lo = qo = 0
            for c in carries:
                if getattr(c, which) is not None:
                    getattr(c, which)(srcs[so:so + len(c.srcs)], lands[lo:lo + len(c.lands)],
                                      _SemView(ssem, qo), _SemView(rsem, qo))
                so, lo, qo = so + len(c.srcs), lo + len(c.lands), qo + c.n_sems
        return go

    middle = run("middle") if any(c.middle is not None for c in carries) else None
    return _Carry([s for c in carries for s in c.srcs], [l for c in carries for l in c.lands],
                  sum(c.n_sems for c in carries), run("start"), run("finish"), middle)


GATHER_SEMS = 7


def _gather_copies(lands, ssem, rsem, part=(0, 1, 1)):
    x, y, c = _place()
    k, kx, ky, kd = 2 * x + y, 2 * (1 - x) + y, 2 * x + (1 - y), 2 * (1 - x) + (1 - y)
    to_x, to_y, sib = (1 - x, y, c), (x, 1 - y, c), (x, y, 1 - c)
    out = []
    for w, land in enumerate(lands):
        rh = land.shape[1] // 2
        rq = rh // 2
        half = pl.ds(pl.multiple_of(c * rh, 16), rh)
        quarters = [pl.ds(pl.multiple_of(c * rh + q * rq, 16), rq) for q in range(2)]
        pa, pb, pp = part
        sent = pl.ds(pl.multiple_of(c * rh + pa * rh // pp, 16), (pb - pa) * rh // pp)

        def cp(i, piece, dev, w=w):
            return _remote(piece, piece, ssem.at[GATHER_SEMS * w + i], rsem.at[GATHER_SEMS * w + i], dev)

        out.append(dict(
            ax=cp(0, land.at[k, sent], to_x), ay=cp(1, land.at[k, sent], to_y),
            rx=cp(2, land.at[kx, quarters[0]], to_y), ry=cp(3, land.at[ky, quarters[1]], to_x),
            fx=cp(4, land.at[kx, half], sib), fy=cp(5, land.at[ky, half], sib), fd=cp(6, land.at[kd, half], sib)))
    return out


def _gather_whole(gathered):
    def start(srcs, lands, ssem, rsem):
        for d in _gather_copies(lands, ssem, rsem):
            d["ax"].start()
            d["ay"].start()

    def middle(srcs, lands, ssem, rsem):
        for d in _gather_copies(lands, ssem, rsem):
            d["ax"].wait_recv()
            d["rx"].start()
            d["fx"].start()
            d["ay"].wait_recv()
            d["ry"].start()
            d["fy"].start()

    def finish(srcs, lands, ssem, rsem):
        for d in _gather_copies(lands, ssem, rsem):
            d["rx"].wait_recv()
            d["ry"].wait_recv()
            d["fd"].start()
            for name in ("fx", "fy", "fd"):
                d[name].wait_recv()
            for cp in d.values():
                cp.wait_send()

    return _Carry([], gathered, GATHER_SEMS * len(gathered), start, finish, middle)


def _gather_near(gathered, part=(0, 1, 1)):
    def start(srcs, lands, ssem, rsem):
        for d in _gather_copies(lands, ssem, rsem, part):
            d["ax"].start()
            d["ay"].start()

    def finish(srcs, lands, ssem, rsem):
        for d in _gather_copies(lands, ssem, rsem, part):
            for name in ("ax", "ay"):
                d[name].wait_recv()
                d[name].wait_send()

    return _Carry([], gathered, GATHER_SEMS * len(gathered), start, finish)


def _gather_far(gathered):
    def start(srcs, lands, ssem, rsem):
        for d in _gather_copies(lands, ssem, rsem):
            for name in ("rx", "ry", "fx", "fy"):
                d[name].start()

    def finish(srcs, lands, ssem, rsem):
        for d in _gather_copies(lands, ssem, rsem):
            d["rx"].wait_recv()
            d["ry"].wait_recv()
            d["fd"].start()
            for name in ("fx", "fy", "fd"):
                d[name].wait_recv()
            for name in ("rx", "ry", "fx", "fy", "fd"):
                d[name].wait_send()

    return _Carry([], gathered, GATHER_SEMS * len(gathered), start, finish)


def _sibling_carry(dw):
    def copies(srcs, lands, ssem, rsem):
        x, y, c = _place()
        return [_remote(srcs[0].at[j, 1 - c], lands[0].at[j], ssem.at[j], rsem.at[j], (x, y, 1 - c))
                for j in range(N_CHIPS)]

    def start(*a):
        for cp in copies(*a):
            cp.start()

    def finish(*a):
        for cp in copies(*a):
            cp.wait()

    return _Carry([dw], [jax.ShapeDtypeStruct((N_CHIPS,) + dw.shape[-2:], dw.dtype)], N_CHIPS, start, finish)


def _chips_carry(sums, land, part=(0, 1, 1)):
    a, b, p = part
    rh = sums.shape[1]
    rows = pl.ds(a * rh // p, (b - a) * rh // p)

    def copies(srcs, lands, ssem, rsem):
        x, y, c = _place()
        out = []
        for j, (cx, cy) in enumerate(_other_chips(x, y)):
            out.append(_remote(srcs[0].at[2 * cx + cy, rows], lands[0].at[j, rows], ssem.at[j], rsem.at[j],
                               (cx, cy, c)))
        return out

    def start(*a):
        for cp in copies(*a):
            cp.start()

    def finish(*a):
        for cp in copies(*a):
            cp.wait()

    if land is None:
        land = jax.ShapeDtypeStruct((3,) + sums.shape[1:], sums.dtype)
    return _Carry([sums], [land], 3, start, finish)


def _call(name, grid, body, ins, in_specs, out_shapes, out_specs, scratch=(), vmem_mb=48, carry=None):
    n_in, n_out, n_sc = len(ins), len(out_shapes), len(scratch)
    ins, in_specs = list(ins), list(in_specs)
    out_shapes, out_specs, scratch = list(out_shapes), list(out_specs), list(scratch)
    aliases = {}
    if carry is not None:
        ins += carry.srcs
        in_specs += [ANY] * len(carry.srcs)
        for land in carry.lands:
            if not isinstance(land, jax.ShapeDtypeStruct):
                aliases[len(ins)] = len(out_shapes)
                ins.append(land)
                in_specs.append(ANY)
                land = jax.ShapeDtypeStruct(land.shape, land.dtype)
            out_shapes.append(land)
            out_specs.append(ANY)
        scratch += [pltpu.SemaphoreType.DMA((carry.n_sems,)), pltpu.SemaphoreType.DMA((carry.n_sems,))]
    n_in_all, n_out_all = len(ins), len(out_shapes)
    ins = [pltpu.with_memory_space_constraint(a, pltpu.HBM) for a in ins]

    def kbody(*refs):
        in_refs, out_refs, sc = refs[:n_in_all], refs[n_in_all:n_in_all + n_out_all], refs[n_in_all + n_out_all:]
        ids = tuple(pl.program_id(a) for a in range(len(grid)))
        if carry is not None:
            first = functools.reduce(jnp.logical_and, [i == 0 for i in ids])
            last = functools.reduce(jnp.logical_and, [i == g - 1 for i, g in zip(ids, grid)])
            comm = (in_refs[n_in:n_in + len(carry.srcs)], out_refs[n_out:], sc[n_sc], sc[n_sc + 1])

            @pl.when(first)
            def _():
                carry.start(*comm)

            if carry.middle is not None:
                step, total = ids[0], grid[0]
                for i, g in zip(ids[1:], grid[1:]):
                    step, total = step * g + i, total * g

                @pl.when(step == (2 * total) // 3)
                def _():
                    carry.middle(*comm)

        body(in_refs[:n_in], out_refs[:n_out], sc[:n_sc], ids)
        if carry is not None:
            @pl.when(last)
            def _():
                carry.finish(*comm)

    return pl.pallas_call(
        kbody, name=name, grid=grid, in_specs=in_specs, out_specs=out_specs,
        out_shape=[pltpu.HBM(o.shape, o.dtype) for o in out_shapes],
        scratch_shapes=scratch, input_output_aliases=aliases,
        compiler_params=pltpu.CompilerParams(dimension_semantics=("arbitrary",) * len(grid),
                                             vmem_limit_bytes=vmem_mb * MIB),
    )(*ins)


def _matmul(name, grid, kaxis, ins, in_specs, out_shapes, out_specs, dims, epi, tail=None, acc_shape=None, nc=512,
            vmem_mb=48, carry=None):
    nk = grid[kaxis] if kaxis is not None else 1

    def body(in_refs, out_refs, scratch, ids):
        a_ref, b_ref = in_refs[0], in_refs[1]
        n = b_ref.shape[0] if dims == NT else b_ref.shape[1]

        def prod(a, c0):
            b = b_ref[c0:c0 + nc, :] if dims == NT else b_ref[:, c0:c0 + nc]
            return lax.dot_general(a, b, (dims, ((), ())), preferred_element_type=F32)

        if kaxis is None:
            a = a_ref[...]
            for c0 in range(0, n, nc):
                epi(prod(a, c0), slice(c0, c0 + nc), ids, in_refs[2:], out_refs)
            if tail is not None:
                tail(ids, in_refs[2:], out_refs)
        else:
            acc = scratch[0]
            tm = acc.shape[0]
            last = ids[kaxis] == nk - 1
            _zero_when(ids[kaxis] == 0, acc)

            @pl.when(jnp.logical_not(last))
            def _():
                a = a_ref[...]
                for c0 in range(0, n, nc):
                    acc[:, c0:c0 + nc] += prod(a, c0)

            @pl.when(last)
            def _():
                block = epi(acc, ids, in_refs[2:], out_refs)
                for r0 in range(0, tm, tm // LAST_STEP_PARTS):
                    a = a_ref[r0:r0 + tm // LAST_STEP_PARTS, :]
                    for c0 in range(0, n, nc):
                        acc[r0:r0 + tm // LAST_STEP_PARTS, c0:c0 + nc] += prod(a, c0)
                    for q0 in range(r0, r0 + tm // LAST_STEP_PARTS, EPI_ROWS):
                        block(slice(q0, q0 + EPI_ROWS))

    return _call(name, grid, body, ins, in_specs, out_shapes, out_specs,
                 [pltpu.VMEM(acc_shape, F32)] if kaxis is not None else [], vmem_mb, carry)


def _row_rsqrt(xf):
    return lax.rsqrt(jnp.mean(xf * xf, axis=-1, keepdims=True) + EPS)


def _norm_bwd(dh, xf, g, resid):
    r = _row_rsqrt(xf)
    xh = xf * r
    dg = jnp.sum(dh * xh, axis=0, keepdims=True)
    dxh = dh * g
    dx = resid + r * (dxh - xh * jnp.mean(dxh * xh, axis=-1, keepdims=True))
    return dx, dg


def _zero_when(first, *refs):
    @pl.when(first)
    def _():
        for ref in refs:
            ref[...] = jnp.zeros_like(ref)


_GELU_K = 0.7978845608028654
_GELU_C = 0.044715


def _gelu(x):
    t = jnp.tanh(_GELU_K * (x + _GELU_C * x * x * x))
    return 0.5 * x * (1.0 + t)


def _gelu_and_grad(x):
    x2 = x * x
    t = jnp.tanh(_GELU_K * (x + _GELU_C * x * x2))
    g = 0.5 * x * (1.0 + t)
    dg = 0.5 * (1.0 + t) + 0.5 * x * (1.0 - t * t) * (_GELU_K * (1.0 + 3.0 * _GELU_C * x2))
    return g, dg


def _window_sum(ext, w, causal):
    n = ext.shape[0]
    s, d = ext, 1
    while d < w:
        s = s + pltpu.roll(s, d if causal else n - d, 0)
        d *= 2
    return s


def _inv_count(t, w):
    return 1.0 / jnp.minimum(t + 1, w).astype(F32)


def _pooled(z_ref, zh_ref, g, w, i, tm):
    cols = slice(GD * g, GD * (g + 1))
    zb = z_ref[:, cols]
    zh = jnp.where(i > 0, zh_ref[:, cols], 0.0)
    ext = jnp.concatenate([zh, zb], axis=0)
    s = _window_sum(ext, w, True)[HALO:, :]
    t = i * tm + lax.broadcasted_iota(jnp.int32, (tm, 1), 0)
    return s * _inv_count(t, w) - zb


def _full(shape, n_axes=1):
    return pl.BlockSpec(shape, lambda *ids: (0,) * len(shape))


def _norm1(x, g, tm, carry):
    s = x.shape[0]

    def body(ins, outs, scratch, ids):
        xf = ins[0][...]
        outs[0][...] = (xf * _row_rsqrt(xf) * ins[1][...]).astype(BF16)

    row = pl.BlockSpec((tm, D), lambda i: (i, 0))
    return _call("norm1", (s // tm,), body, [x, g], [row, _full((1, D))],
                 [jax.ShapeDtypeStruct((s, D), BF16)], [row], vmem_mb=32, carry=carry)


def _mixer_fwd(proj, wt, bst, gv, wpool, scale, tm, carry):
    s = proj.shape[0]
    nq = tm // HD

    def body(ins, outs, scratch, ids):
        up_ref, vp_ref, z_ref, zh_ref, wt_ref, bst_ref, gv_ref, wp_ref, sc_ref = ins
        out_ref = outs[0]
        i = ids[0]
        for h in range(HEADS):
            cols = slice(HD * h, HD * (h + 1))
            gvh = gv_ref[:, cols]
            bcol = bst_ref[h]
            wth = wt_ref[h]

            def chunk(q, c_):
                rows = pl.ds(pl.multiple_of(q * HD, HD), HD)
                u = _gelu(up_ref[rows, cols])
                v = _gelu(vp_ref[rows, cols])
                vh = (v * _row_rsqrt(v) * gvh).astype(BF16)
                mixed = jnp.dot(wth, vh, preferred_element_type=F32) + bcol
                out_ref[rows, cols] = (u * mixed).astype(BF16)
                return c_

            lax.fori_loop(0, nq, chunk, 0, unroll=True)
        for g, w in enumerate(POOL_WINDOWS):
            cols = slice(GD * g, GD * (g + 1))
            pooled = _pooled(z_ref, zh_ref, g, w, i, tm)
            yv = jnp.dot(pooled.astype(BF16), wp_ref[g], preferred_element_type=F32)
            out_ref[:, A_W + GD * g:A_W + GD * (g + 1)] = (yv * sc_ref[:, cols]).astype(BF16)

    hb = tm // HALO
    return _call(
        "mixer_fwd", (s // tm,), body, [proj, proj, proj, proj, wt, bst, gv, wpool, scale],
        [pl.BlockSpec((tm, A_W), lambda i: (i, 0)),
         pl.BlockSpec((tm, A_W), lambda i: (i, 1)),
         pl.BlockSpec((tm, A_W), lambda i: (i, 2)),
         pl.BlockSpec((HALO, A_W), lambda i: (jnp.maximum(i * hb - 1, 0), 2)),
         _full((HEADS, HD, HD)), _full((HEADS, HD, HD)), _full((1, A_W)), _full((4, GD, GD)), _full((1, A_W))],
        [jax.ShapeDtypeStruct((s, D), BF16)], [pl.BlockSpec((tm, D), lambda i: (i, 0))], vmem_mb=40, carry=carry)


def _mixer_bwd(proj, dmix, wt, wtt, bst, gv, wpool, scale, tm, carry):
    s = proj.shape[0]
    nb = s // tm
    nq = tm // HD
    hb = tm // HALO

    def body(ins, outs, scratch, ids):
        (up_ref, vp_ref, z_ref, zh_ref, doa_ref, dob_ref, dobh_ref, wt_ref, wtt_ref, bst_ref, gv_ref, wp_ref,
         sc_ref) = ins
        dproj_ref, dws_ref, dbs_ref, dgv_ref, dwp_ref, dsc_ref = outs
        dbfull = scratch[0]
        i = ids[0]
        last = i == nb - 1
        _zero_when(i == 0, dws_ref, dbfull, dgv_ref, dwp_ref, dsc_ref)

        for h in range(HEADS):
            cols = slice(HD * h, HD * (h + 1))
            gvh = gv_ref[:, cols]
            bcol = bst_ref[h]
            wth = wt_ref[h]
            wtth = wtt_ref[h]

            def chunk(q, c_):
                rows = pl.ds(pl.multiple_of(q * HD, HD), HD)
                u, du_dup = _gelu_and_grad(up_ref[rows, cols])
                v, dv_dvp = _gelu_and_grad(vp_ref[rows, cols])
                rv = _row_rsqrt(v)
                vn = v * rv
                vh = (vn * gvh).astype(BF16)
                mixed = jnp.dot(wth, vh, preferred_element_type=F32) + bcol
                doa = doa_ref[rows, cols].astype(F32)
                dmx = doa * u
                dmxb = dmx.astype(BF16)
                dbfull[h] += dmx
                dws_ref[h] += lax.dot_general(dmxb, vh, (NT, ((), ())), preferred_element_type=F32)
                dvh = jnp.dot(wtth, dmxb, preferred_element_type=F32)
                dgv_ref[:, cols] += jnp.sum(dvh * vn, axis=0, keepdims=True)
                dvn = dvh * gvh
                dv = rv * (dvn - vn * jnp.mean(dvn * vn, axis=-1, keepdims=True))
                dproj_ref[rows, cols] = (doa * mixed * du_dup).astype(BF16)
                dproj_ref[rows, A_W + HD * h:A_W + HD * (h + 1)] = (dv * dv_dvp).astype(BF16)
                return c_

            lax.fori_loop(0, nq, chunk, 0, unroll=True)

        t = i * tm + lax.broadcasted_iota(jnp.int32, (tm, 1), 0)
        th = (i + 1) * tm + lax.broadcasted_iota(jnp.int32, (HALO, 1), 0)
        for g, w in enumerate(POOL_WINDOWS):
            cols = slice(GD * g, GD * (g + 1))
            wpg = wp_ref[g]
            scg = sc_ref[:, cols]
            pb = _pooled(z_ref, zh_ref, g, w, i, tm).astype(BF16)
            ypre = jnp.dot(pb, wpg, preferred_element_type=F32)
            dob = dob_ref[:, cols].astype(F32)
            dsc_ref[:, cols] += jnp.sum(dob * ypre, axis=0, keepdims=True)
            dyb = (dob * scg).astype(BF16)
            dwp_ref[g] += lax.dot_general(pb, dyb, (TN, ((), ())), preferred_element_type=F32)
            dpo = lax.dot_general(dyb, wpg, (NT, ((), ())), preferred_element_type=F32)
            dyh = (dobh_ref[:, cols].astype(F32) * scg).astype(BF16)
            dpoh = lax.dot_general(dyh, wpg, (NT, ((), ())), preferred_element_type=F32)
            dpoh = jnp.where(last, 0.0, dpoh * _inv_count(th, w))
            ext = jnp.concatenate([dpo * _inv_count(t, w), dpoh], axis=0)
            dz = _window_sum(ext, w, False)[:tm, :] - dpo
            dproj_ref[:, 2 * A_W + GD * g:2 * A_W + GD * (g + 1)] = dz.astype(BF16)

        @pl.when(last)
        def _():
            r = lax.broadcasted_iota(jnp.int32, (HD, HD), 0)
            c = lax.broadcasted_iota(jnp.int32, (HD, HD), 1)
            for h in range(HEADS):
                dws_ref[h] = jnp.where(r >= c, dws_ref[h], 0.0)
                dbs_ref[h] = jnp.sum(dbfull[h], axis=-1, keepdims=True)

    return _call(
        "mixer_bwd", (nb,), body, [proj, proj, proj, proj, dmix, dmix, dmix, wt, wtt, bst, gv, wpool, scale],
        [pl.BlockSpec((tm, A_W), lambda i: (i, 0)),
         pl.BlockSpec((tm, A_W), lambda i: (i, 1)),
         pl.BlockSpec((tm, A_W), lambda i: (i, 2)),
         pl.BlockSpec((HALO, A_W), lambda i: (jnp.maximum(i * hb - 1, 0), 2)),
         pl.BlockSpec((tm, A_W), lambda i: (i, 0)),
         pl.BlockSpec((tm, A_W), lambda i: (i, 1)),
         pl.BlockSpec((HALO, A_W), lambda i: (jnp.minimum((i + 1) * hb, s // HALO - 1), 1)),
         _full((HEADS, HD, HD)), _full((HEADS, HD, HD)), _full((HEADS, HD, HD)), _full((1, A_W)),
         _full((4, GD, GD)), _full((1, A_W))],
        [jax.ShapeDtypeStruct((s, IN_W), BF16),
         jax.ShapeDtypeStruct((HEADS, HD, HD), F32),
         jax.ShapeDtypeStruct((HEADS, HD, 1), F32),
         jax.ShapeDtypeStruct((1, A_W), F32),
         jax.ShapeDtypeStruct((4, GD, GD), F32),
         jax.ShapeDtypeStruct((1, A_W), F32)],
        [pl.BlockSpec((tm, IN_W), lambda i: (i, 0)),
         _full((HEADS, HD, HD)), _full((HEADS, HD, 1)), _full((1, A_W)), _full((4, GD, GD)), _full((1, A_W))],
        [pltpu.VMEM((HEADS, HD, HD), F32)], vmem_mb=48, carry=carry)


def _inproj(h1, win_g, tm, carry):
    s = h1.shape[0]
    cw = IN_W // N_CHIPS

    def epi(p, cols, ids, extra, outs):
        outs[0][:, cols] = p

    return _matmul(
        "inproj", (N_CHIPS, s // tm), None, [h1, win_g],
        [pl.BlockSpec((tm, D), lambda j, i: (i, 0)), pl.BlockSpec((None, D, cw), lambda j, i: (j, 0, 0))],
        [jax.ShapeDtypeStruct((s, IN_W), F32)], [pl.BlockSpec((tm, cw), lambda j, i: (i, j))], NN, epi,
        nc=256, vmem_mb=32, carry=carry)


def _outproj(mixed, wout, x, g_ffn, tm, carry):
    s = x.shape[0]

    def epi(p, cols, ids, extra, outs):
        outs[0][:, cols] = extra[0][:, cols] + p

    def tail(ids, extra, outs):
        x2 = outs[0][...]
        outs[1][...] = (x2 * _row_rsqrt(x2) * extra[1][...]).astype(BF16)

    row = pl.BlockSpec((tm, D), lambda i: (i, 0))
    return _matmul(
        "outproj", (s // tm,), None, [mixed, wout, x, g_ffn], [row, _full((D, D)), row, _full((1, D))],
        [jax.ShapeDtypeStruct((s, D), F32), jax.ShapeDtypeStruct((s, D), BF16)], [row, row], NN, epi, tail,
        vmem_mb=48, carry=carry)


def _up(h2, wup_g, tm, tn, carry):
    s = h2.shape[0]
    per = D // tn

    def epi(p, cols, ids, extra, outs):
        a = jnp.maximum(p, 0.0)
        outs[0][:, cols] = (a * a).astype(BF16)

    return _matmul(
        "up", (D_FF // tn, s // tm), None, [h2, wup_g],
        [pl.BlockSpec((tm, D), lambda j, i: (i, 0)), pl.BlockSpec((None, D, tn), lambda j, i: (j // per, 0, j % per))],
        [jax.ShapeDtypeStruct((s, D_FF), BF16)], [pl.BlockSpec((tm, tn), lambda j, i: (i, j))], NN, epi,
        vmem_mb=48, carry=carry)


def _down(act, wdown, x2, tgt, g_final, tm, tk):
    s = x2.shape[0]

    def epi(acc, ids, extra, outs):
        x2_ref, t_ref, g_ref = extra
        dxb_ref, dgf_ref, loss_ref = outs
        g = g_ref[...]
        _zero_when(ids[0] == 0, dgf_ref, loss_ref)

        def block(rows):
            x3 = x2_ref[rows, :] + acc[rows, :]
            r = _row_rsqrt(x3)
            xh = x3 * r
            diff = xh * g - t_ref[rows, :]
            dy = diff * (1.0 / D)
            dxh = dy * g
            dx = r * (dxh - xh * jnp.mean(dxh * xh, axis=-1, keepdims=True))
            dxb_ref[rows, :] = dx.astype(BF16)
            dgf_ref[...] += jnp.sum(dy * xh, axis=0, keepdims=True)
            loss_ref[...] += jnp.sum(diff * diff, axis=0, keepdims=True)

        return block

    row = pl.BlockSpec((tm, D), lambda i, k: (i, 0))
    vec = pl.BlockSpec((1, D), lambda i, k: (0, 0))
    return _matmul(
        "down", (s // tm, D_FF // tk), 1, [act, wdown, x2, tgt, g_final],
        [pl.BlockSpec((tm, tk), lambda i, k: (i, k)), pl.BlockSpec((tk, D), lambda i, k: (k, 0)), row, row, vec],
        [jax.ShapeDtypeStruct((s, D), BF16), jax.ShapeDtypeStruct((1, D), F32), jax.ShapeDtypeStruct((1, D), F32)],
        [row, vec, vec], NN, epi, acc_shape=(tm, D), vmem_mb=56)


def _dact(dx3b, wdown, act, tm, tn, carry):
    s = dx3b.shape[0]

    def epi(p, cols, ids, extra, outs):
        outs[0][:, cols] = (p * (2.0 * jnp.sqrt(extra[0][:, cols].astype(F32)))).astype(BF16)

    tile = pl.BlockSpec((tm, tn), lambda j, i: (i, j))
    return _matmul(
        "dact", (D_FF // tn, s // tm), None, [dx3b, wdown, act],
        [pl.BlockSpec((tm, D), lambda j, i: (i, 0)), pl.BlockSpec((tn, D), lambda j, i: (j, 0)), tile],
        [jax.ShapeDtypeStruct((s, D_FF), BF16)], [tile], NT, epi, vmem_mb=56, carry=carry)


def _dweight(name, lhs, rhs, n_shards, rows, cols, tm, nc=512, carry=None):
    s = lhs.shape[0]

    def epi(p, cs, ids, extra, outs):
        outs[0][:, cs] = p.astype(BF16)

    return _matmul(
        name, (n_shards, rows // tm), None, [lhs, rhs],
        [pl.BlockSpec((s, tm), lambda j, i: (0, i)), pl.BlockSpec((s, cols), lambda j, i: (0, j))],
        [jax.ShapeDtypeStruct((n_shards, rows, cols), BF16)],
        [pl.BlockSpec((None, tm, cols), lambda j, i: (j, i, 0))], TN, epi, nc=nc, vmem_mb=56, carry=carry)


def _dh2(da, wup_g, x2, dx3, g_ffn, tm, tk, carry):
    s = x2.shape[0]
    per = D // tk

    def epi(acc, ids, extra, outs):
        x2_ref, dx3_ref, g_ref = extra
        g = g_ref[...]
        _zero_when(ids[0] == 0, outs[1])

        def block(rows):
            dx, dg = _norm_bwd(acc[rows, :], x2_ref[rows, :], g, dx3_ref[rows, :].astype(F32))
            outs[0][rows, :] = dx.astype(BF16)
            outs[1][...] += dg

        return block

    row = pl.BlockSpec((tm, D), lambda i, k: (i, 0))
    vec = pl.BlockSpec((1, D), lambda i, k: (0, 0))
    return _matmul(
        "dh2", (s // tm, D_FF // tk), 1, [da, wup_g, x2, dx3, g_ffn],
        [pl.BlockSpec((tm, tk), lambda i, k: (i, k)),
         pl.BlockSpec((None, D, tk), lambda i, k: (k // per, 0, k % per)), row, row, vec],
        [jax.ShapeDtypeStruct((s, D), BF16), jax.ShapeDtypeStruct((1, D), F32)],
        [row, vec], NT, epi, acc_shape=(tm, D), vmem_mb=56, carry=carry)


def _dmixed(dx2b, wout, tm, carry):
    s = dx2b.shape[0]

    def epi(p, cols, ids, extra, outs):
        outs[0][:, cols] = p.astype(BF16)

    row = pl.BlockSpec((tm, D), lambda i: (i, 0))
    return _matmul(
        "dmixed", (s // tm,), None, [dx2b, wout], [row, _full((D, D))],
        [jax.ShapeDtypeStruct((s, D), BF16)], [row], NT, epi, vmem_mb=40, carry=carry)


def _dh1(dproj, win_g, x, dx2, g_mix, tm, carry):
    s = x.shape[0]
    cw = IN_W // N_CHIPS

    def epi(acc, ids, extra, outs):
        x_ref, dx2_ref, g_ref = extra
        g = g_ref[...]
        _zero_when(ids[0] == 0, outs[1])

        def block(rows):
            dx, dg = _norm_bwd(acc[rows, :], x_ref[rows, :], g, dx2_ref[rows, :].astype(F32))
            outs[0][rows, :] = dx
            outs[1][...] += dg

        return block

    row = pl.BlockSpec((tm, D), lambda i, j: (i, 0))
    vec = pl.BlockSpec((1, D), lambda i, j: (0, 0))
    return _matmul(
        "dh1", (s // tm, N_CHIPS), 1, [dproj, win_g, x, dx2, g_mix],
        [pl.BlockSpec((tm, cw), lambda i, j: (i, j)), pl.BlockSpec((None, D, cw), lambda i, j: (j, 0, 0)),
         row, row, vec],
        [jax.ShapeDtypeStruct((s, D), F32), jax.ShapeDtypeStruct((1, D), F32)],
        [row, vec], NT, epi, acc_shape=(tm, D), vmem_mb=52, carry=carry)


def _cast_place(name, w, tr, carry=None):
    rows, cols = w.shape

    def body(ins, outs, scratch, ids):
        outs[0][...] = ins[0][...].astype(BF16)

    return _call(name, (rows // tr,), body, [w], [pl.BlockSpec((tr, cols), lambda r: (r, 0))],
                 [jax.ShapeDtypeStruct((N_CHIPS, rows, cols), BF16)],
                 [pl.BlockSpec((None, tr, cols), lambda r: (_my_chip(), r, 0))], vmem_mb=32, carry=carry)


def _small_allreduce(part):
    rows = part.shape[0]
    rh = rows // 2

    def body(p_ref, o_ref, sib_ref, slots, send_sems, recv_sems):
        x, y, c = _place()
        k = 2 * x + y
        sib = (x, y, 1 - c)
        half = pl.ds(pl.multiple_of(c * rh, 8), rh)
        cp = _remote(p_ref, sib_ref, send_sems.at[0], recv_sems.at[0], sib)
        cp.start()
        cp.wait()
        slots[k] = p_ref[half, :] + sib_ref[half, :]
        cps = []
        for j, (cx, cy) in enumerate(_other_chips(x, y)):
            cp = _remote(slots.at[k], slots.at[k], send_sems.at[1 + j], recv_sems.at[1 + j], (cx, cy, c))
            cp.start()
            cps.append(cp)
        for cp in cps:
            cp.wait()
        o_ref[half, :] = ((slots[0] + slots[1]) + slots[2]) + slots[3]
        cp = _remote(o_ref.at[half], o_ref.at[half], send_sems.at[4], recv_sems.at[4], sib)
        cp.start()
        cp.wait()

    vm = pl.BlockSpec(memory_space=pltpu.VMEM)
    return pl.pallas_call(
        body, name="small_allreduce", in_specs=[vm], out_specs=vm,
        out_shape=jax.ShapeDtypeStruct(part.shape, F32),
        scratch_shapes=[pltpu.VMEM(part.shape, F32), pltpu.VMEM((N_CHIPS, rh, LANES), F32),
                        pltpu.SemaphoreType.DMA((5,)), pltpu.SemaphoreType.DMA((5,))],
        compiler_params=pltpu.CompilerParams(vmem_limit_bytes=32 * MIB),
    )(part)


def _comm_only(name, carry):
    ns, nl = len(carry.srcs), len(carry.lands)
    lands_in = [l for l in carry.lands if not isinstance(l, jax.ShapeDtypeStruct)]
    assert len(lands_in) in (0, nl)

    def body(*refs):
        srcs = refs[:ns]
        lands = refs[ns + len(lands_in):ns + len(lands_in) + nl]
        ssem, rsem = refs[ns + len(lands_in) + nl:]
        carry.start(srcs, lands, ssem, rsem)
        if carry.middle is not None:
            carry.middle(srcs, lands, ssem, rsem)
        carry.finish(srcs, lands, ssem, rsem)

    return pl.pallas_call(
        body, name=name, in_specs=[ANY] * (ns + len(lands_in)), out_specs=[ANY] * nl,
        out_shape=[jax.ShapeDtypeStruct(l.shape, l.dtype) for l in carry.lands],
        input_output_aliases={ns + i: i for i in range(len(lands_in))},
        scratch_shapes=[pltpu.SemaphoreType.DMA((carry.n_sems,)), pltpu.SemaphoreType.DMA((carry.n_sems,))],
    )(*carry.srcs, *lands_in)


def _share_carry(grads):
    n = len(grads)

    def copies(srcs, lands, ssem, rsem):
        x, y, c = _place()
        return [_remote(lands[w].at[c], lands[w].at[c], ssem.at[w], rsem.at[w], (x, y, 1 - c)) for w in range(n)]

    def start(*a):
        for cp in copies(*a):
            cp.start()

    def finish(*a):
        for cp in copies(*a):
            cp.wait()

    return _Carry([], grads, n, start, finish)


def _add_sibling(dw, got, tr):
    _, _, rh, cols = dw.shape

    def body(ins, outs, scratch, ids):
        outs[0][...] = (ins[0][...].astype(F32) + ins[1][...].astype(F32)).astype(BF16)

    other = lambda j: (_my_chip() + 1 + j) % N_CHIPS
    blk = pl.BlockSpec((None, tr, cols), lambda j, r: (other(j), r, 0))
    return _call("add_sibling", (N_CHIPS - 1, rh // tr), body, [dw, got],
                 [pl.BlockSpec((None, None, tr, cols), lambda j, r: (other(j), _my_core(), r, 0)), blk],
                 [jax.ShapeDtypeStruct((N_CHIPS, rh, cols), BF16)], [blk], vmem_mb=32)[0]


def _add_chips(dw, sib, got, tr, carry=None):
    _, _, rh, cols = dw.shape

    def body(ins, outs, scratch, ids):
        b = ins[2][...].astype(F32)
        outs[0][...] = (((ins[0][...].astype(F32) + ins[1][...].astype(F32)) + b[0]) + b[1]) + b[2]

    return _call("add_chips", (rh // tr,), body, [dw, sib, got],
                 [pl.BlockSpec((None, None, tr, cols), lambda r: (_my_chip(), _my_core(), r, 0)),
                  pl.BlockSpec((None, tr, cols), lambda r: (_my_chip(), r, 0)),
                  pl.BlockSpec((3, tr, cols), lambda r: (0, r, 0))],
                 [jax.ShapeDtypeStruct((2, rh, cols), F32)],
                 [pl.BlockSpec((None, tr, cols), lambda r: (_my_core(), r, 0))], vmem_mb=32, carry=carry)


def _adamw_math(w, g, m, v):
    m = ADAM_B1 * m + (1.0 - ADAM_B1) * g
    v = ADAM_B2 * v + (1.0 - ADAM_B2) * (g * g)
    m_hat = m / (1.0 - ADAM_B1 ** ADAM_STEP)
    v_hat = v / (1.0 - ADAM_B2 ** ADAM_STEP)
    delta = -ADAM_LR * (m_hat / (jnp.sqrt(v_hat) + ADAM_EPS) + ADAM_WD * w)
    return delta, m, v


def _adamw(name, w, g, m, v, tr, carry=None):
    rows, cols = w.shape

    def body(ins, outs, scratch, ids):
        g_val = ins[1][...]
        outs[0][...] = g_val
        outs[1][...], outs[2][...], outs[3][...] = _adamw_math(ins[0][...], g_val, ins[2][...], ins[3][...])

    blk = pl.BlockSpec((tr, cols), lambda r: (r, 0))
    return _call(name, (rows // tr,), body, [w, g, m, v], [blk] * 4,
                 [jax.ShapeDtypeStruct(w.shape, F32)] * 4, [blk] * 4, vmem_mb=40, carry=carry)


def _rows(a):
    return a.reshape(-1, LANES)


def kernel(x, g_mix, w_in, g_v, w_s, b_s, w_pool, pool_scale, w_out, g_ffn, w_up, w_down, g_final, loss_target, m_g_mix, m_w_in, m_g_v, m_w_s, m_b_s, m_w_pool, m_pool_scale, m_w_out, m_g_ffn, m_w_up, m_w_down, m_g_final, v_g_mix, v_w_in, v_g_v, v_w_s, v_b_s, v_w_pool, v_pool_scale, v_w_out, v_g_ffn, v_w_up, v_w_down, v_g_final):
    tm = 512
    xs = x[0]
    tgt = loss_target[0]
    chip = _my_chip()

    win_g = _cast_place("cast_w_in", w_in[0], 256)[0]
    wpool_g = _cast_place("cast_w_pool", w_pool[0].reshape(4 * 64, GD), 128)[0]
    wout_g = _cast_place("cast_w_out", w_out[0], 128)[0]
    wdown_g, win_g, wpool_g = _cast_place("cast_w_down", w_down[0], 256, _gather_near([win_g, wpool_g], (0, 1, 2)))
    wup_g, win_g, wpool_g = _cast_place("cast_w_up", w_up[0], 256, _gather_near([win_g, wpool_g], (1, 2, 2)))
    h1, win_g, wpool_g = _norm1(xs, g_mix, tm, _gather_far([win_g, wpool_g]))
    wpool_f = wpool_g.reshape(N_CHIPS, 4, 64, GD).transpose(1, 0, 2, 3).reshape(4, GD, GD)
    tril = jnp.tril(jnp.ones((HD, HD), dtype=bool))
    wt = jnp.where(tril[None], w_s[0], 0.0).astype(BF16)
    wtt = wt.transpose(0, 2, 1)
    bst = jnp.broadcast_to(b_s[0][:, :, None], (HEADS, HD, HD))
    gfin = g_final.reshape(1, D)

    proj, wout_g = _inproj(h1, win_g, tm, _gather_whole([wout_g]))
    mixed, wup_g = _mixer_fwd(proj, wt, bst, g_v, wpool_f, pool_scale, tm, _gather_near([wup_g]))
    wout_f = wout_g.reshape(D, D)
    x2, h2, wup_g = _outproj(mixed, wout_f, xs, g_ffn, 256, _gather_far([wup_g]))
    act, wdown_g = _up(h2, wup_g, 2 * tm, 2048, _gather_whole([wdown_g]))
    wdown_f = wdown_g.reshape(D_FF, D)
    dx3b, dgf, lossv = _down(act, wdown_f, x2, tgt, gfin, tm, 2048)

    halves = lambda dw, rows, cols: dw.reshape(N_CHIPS, 2, rows // (2 * N_CHIPS), cols)
    cw = IN_W // N_CHIPS
    dwdown = halves(_dweight("dw_down", act, dx3b, 1, D_FF, D, 512)[0], D_FF, D)
    da, sib_down = _dact(dx3b, wdown_f, act, 2 * tm, 2048, _sibling_carry(dwdown))
    sum_down = _add_sibling(dwdown, sib_down, 256)
    dwup, got_down = _dweight("dw_up", h2, da, N_CHIPS, D, D, 512, carry=_chips_carry(sum_down, None, (0, 3, 4)))
    dwup = halves(dwup, D_FF, D)
    dx2b, dgffn, got_down, sib_up = _dh2(da, wup_g, x2, dx3b, g_ffn, tm, 2048,
                                         _join(_chips_carry(sum_down, got_down, (3, 4, 4)), _sibling_carry(dwup)))
    half_down = _add_chips(dwdown, sib_down, got_down, 256)[0]
    sum_up = _add_sibling(dwup, sib_up, 256)
    dwout, got_up = _dweight("dw_out", mixed, dx2b, 1, D, D, 512, carry=_chips_carry(sum_up, None, (0, 2, 8)))
    dwout = halves(dwout, D, D)
    dmix, got_up, sib_out, half_down = _dmixed(
        dx2b, wout_f, 256,
        _join(_chips_carry(sum_up, got_up, (2, 4, 8)), _sibling_carry(dwout), _share_carry([half_down])))
    sum_out = _add_sibling(dwout, sib_out, 256)
    g_down, d_down, nm_down, nv_down = _adamw("adamw_down", w_down[0], half_down.reshape(D, D), m_w_down[0],
                                              v_w_down[0], 128)
    dproj, dws, dbs, dgv, dwp, dsc, got_up = _mixer_bwd(proj, dmix, wt, wtt, bst, g_v, wpool_f, pool_scale, tm,
                                                        _chips_carry(sum_up, got_up, (4, 7, 8)))
    dwin, got_up, got_out = _dweight(
        "dw_in", h1, dproj, N_CHIPS, D, cw, 512, nc=256,
        carry=_join(_chips_carry(sum_up, got_up, (7, 8, 8)), _chips_carry(sum_out, None)))
    dwin = halves(dwin, N_CHIPS * D, cw)
    half_up, sib_in = _add_chips(dwup, sib_up, got_up, 256, _sibling_carry(dwin))
    sum_in = _add_sibling(dwin, sib_in, 256)
    half_out = _add_chips(dwout, sib_out, got_out, 256)[0]
    grad_x, dgmix, got_in, half_out, half_up = _dh1(
        dproj, win_g, xs, dx2b, g_mix, tm, _join(_chips_carry(sum_in, None), _share_carry([half_out, half_up])))
    g_up, d_up, nm_up, nv_up = _adamw("adamw_up", w_up[0], half_up.reshape(D, D), m_w_up[0], v_w_up[0], 128)
    half_in = _add_chips(dwin, sib_in, got_in, 256)[0]
    half_in = _comm_only("share_half_in", _share_carry([half_in]))[0]
    g_out, d_out, nm_out, nv_out = _adamw("adamw_out", w_out[0], half_out.reshape(D // N_CHIPS, D), m_w_out[0],
                                          v_w_out[0], 128)
    g_in, d_in, nm_in, nv_in = _adamw("adamw_in", w_in[0], half_in.reshape(D, cw), m_w_in[0], v_w_in[0], 128)

    pieces = [dgmix, dgv, dws, dbs, dwp, dsc, dgffn, dgf, lossv, jnp.zeros((8 * LANES,), F32)]
    sizes = [p.size // LANES for p in pieces]
    tot = _small_allreduce(jnp.concatenate([_rows(p) for p in pieces], axis=0))
    offs = [sum(sizes[:i]) for i in range(len(sizes))]
    take = lambda i: tot[offs[i]:offs[i] + sizes[i]]
    s_gmix, s_gv, s_ws, s_bs, s_wp, s_sc, s_gffn, s_gf = [take(i) for i in range(8)]
    loss = (0.5 / D) * jnp.sum(take(8))
    s_wp_mine = lax.dynamic_slice_in_dim(s_wp.reshape(4, GD, GD), chip * 64, 64, axis=1)
    small_g = [s_gmix, s_gv, s_ws, s_bs, _rows(s_wp_mine), s_sc, s_gffn, s_gf]
    small_w = [g_mix, g_v, w_s, b_s, w_pool, pool_scale, g_ffn, g_final]
    small_m = [m_g_mix, m_g_v, m_w_s, m_b_s, m_w_pool, m_pool_scale, m_g_ffn, m_g_final]
    small_v = [v_g_mix, v_g_v, v_w_s, v_b_s, v_w_pool, v_pool_scale, v_g_ffn, v_g_final]
    cat = lambda parts: jnp.concatenate([_rows(p) for p in parts], axis=0)
    sg = cat(small_g)
    sg, sd, snm, snv = _adamw("adamw_small", cat(small_w), sg, cat(small_m), cat(small_v), sg.shape[0])
    ssz = [p.size // LANES for p in small_w]
    soff = [sum(ssz[:i]) for i in range(len(ssz))]
    split = lambda a: [a[soff[i]:soff[i] + ssz[i]].reshape(small_w[i].shape) for i in range(len(ssz))]
    gs, ds, nms, nvs = split(sg), split(sd), split(snm), split(snv)

    def ordered(small, w_in_, w_out_, w_up_, w_down_):
        return [small[0], w_in_[None], small[1], small[2], small[3], small[4], small[5], w_out_[None], small[6],
                w_up_[None], w_down_[None], small[7]]

    return (loss, grad_x[None],
            *ordered(gs, g_in, g_out, g_up, g_down),
            *ordered(ds, d_in, d_out, d_up, d_down),
            *ordered(nms, nm_in, nm_out, nm_up, nm_down),
            *ordered(nvs, nv_in, nv_out, nv_up, nv_down))
```
